```python
import math, functools
import jax, jax.numpy as jnp
from jax import lax
import numpy as np

D_MODEL = 1024
BATCH = 2
SEQ = 8192
DEPTH = 4
DEC_BATCH = 32
DEC_SEQ = 4
PAST_LEN = 8192
PAGE_SIZE = 128

N_A_LAYERS = DEPTH // 2
N_C_LAYERS = DEPTH - N_A_LAYERS
A_HEADS = 4
A_QK_DIM = 64
A_V_DIM = 2 * A_QK_DIM
A_WIDTH = A_HEADS * A_V_DIM
ROPE_DIM = A_QK_DIM // 4
ROPE_THETA = 500000.0
B_WIDTH = D_MODEL - A_WIDTH
CONV_WIDTH = 31
IN_AB = 3 * A_WIDTH + 2 * B_WIDTH
C_HEAD = 64
C_HEADS = D_MODEL // C_HEAD
LORA_DECAY = 64
LORA_A = 64
LORA_MV = 32
LORA_GATE = 160
D_FF = ((8 * D_MODEL + 3 * 256 - 1) // (3 * 256)) * 256
Q_BLOCK = 128
NORM_EPS = 1e-6
SUBLN_EPS = 1e-5
LN_EPS = 1e-5
GN_EPS = 64e-5

kernel_name = 'diffattn_conformer_rwkv7_hybrid_step'

F32 = jnp.float32


def rmsnorm(x, g, eps=NORM_EPS):
    xf = x.astype(F32)
    y = xf * lax.rsqrt(jnp.mean(xf * xf, axis=-1, keepdims=True) + eps)
    return (y * g.astype(F32)).astype(x.dtype)


def layernorm(x, g, b, eps=LN_EPS):
    xf = x.astype(F32)
    mu = jnp.mean(xf, axis=-1, keepdims=True)
    var = jnp.mean(jnp.square(xf - mu), axis=-1, keepdims=True)
    y = (xf - mu) * lax.rsqrt(var + eps)
    return (y * g.astype(F32) + b.astype(F32)).astype(x.dtype)


def rope_partial(x, pos):
    half = ROPE_DIM // 2
    inv = 1.0 / (ROPE_THETA ** (jnp.arange(0, ROPE_DIM, 2, dtype=F32) / ROPE_DIM))
    ang = pos.astype(F32)[:, None] * inv[None, :]
    cos = jnp.cos(ang)[:, None, None, :]
    sin = jnp.sin(ang)[:, None, None, :]
    xf = x.astype(F32)
    x1 = xf[..., :half]
    x2 = xf[..., half:ROPE_DIM]
    out = jnp.concatenate([x1 * cos - x2 * sin, x2 * cos + x1 * sin, xf[..., ROPE_DIM:]], axis=-1)
    return out.astype(x.dtype)


def diff_lambda(lam_p, layer):
    lf = lam_p.astype(F32)
    lam_init = 0.8 - 0.6 * math.exp(-0.3 * layer)
    lam = jnp.exp(jnp.sum(lf[0] * lf[1])) - jnp.exp(jnp.sum(lf[2] * lf[3])) + lam_init
    return lam, lam_init


def diff_attn_prompt(q, k, v, lam):
    B, T = q.shape[:2]
    nb = T // Q_BLOCK
    scale = A_QK_DIM ** -0.5
    qb = jnp.swapaxes(q.reshape(B, nb, Q_BLOCK, A_HEADS, 2, A_QK_DIM), 0, 1)
    kpos = jnp.arange(T)

    def block(args):
        qi, bi = args
        s = jnp.einsum('bqhmd,bkhmd->bhmqk', qi, k).astype(F32) * scale
        qpos = bi * Q_BLOCK + jnp.arange(Q_BLOCK)
        s = jnp.where(kpos[None, :] <= qpos[:, None], s, -jnp.inf)
        pr = jax.nn.softmax(s, axis=-1)
        att = pr[:, :, 0] - lam * pr[:, :, 1]
        return jnp.einsum('bhqk,bkhe->bqhe', att.astype(v.dtype), v)

    out = lax.map(block, (qb, jnp.arange(nb)))
    return jnp.swapaxes(out, 0, 1).reshape(B, T, A_HEADS, A_V_DIM)


def diff_attn_sample(q, k_new, v_new, k_past, v_past, lam):
    Tn = q.shape[1]
    P = k_past.shape[1]
    scale = A_QK_DIM ** -0.5
    s_past = jnp.einsum('bqhmd,bkhmd->bhmqk', q, k_past).astype(F32) * scale
    s_new = jnp.einsum('bqhmd,bkhmd->bhmqk', q, k_new).astype(F32) * scale
    causal = jnp.arange(Tn)[None, :] <= jnp.arange(Tn)[:, None]
    s_new = jnp.where(causal, s_new, -jnp.inf)
    pr = jax.nn.softmax(jnp.concatenate([s_past, s_new], axis=-1), axis=-1)
    att = (pr[:, :, 0] - lam * pr[:, :, 1]).astype(v_new.dtype)
    return (jnp.einsum('bhqk,bkhe->bqhe', att[..., :P], v_past)
            + jnp.einsum('bhqk,bkhe->bqhe', att[..., P:], v_new))


def conv_module(val, gate, buf, w, b, ln_g, ln_b):
    glu = val * jax.nn.sigmoid(gate)
    xp = jnp.concatenate([buf.astype(glu.dtype), glu], axis=1)
    y = lax.conv_general_dilated(xp, w[:, None, :].astype(xp.dtype), window_strides=(1,), padding='VALID',
                                 dimension_numbers=('NWC', 'WIO', 'NWC'), feature_group_count=B_WIDTH)
    y = layernorm(y + b.astype(y.dtype), ln_g, ln_b)
    return jax.nn.silu(y), xp[:, -(CONV_WIDTH - 1):]


def rwkv7_mix(u, shift_prev, S0, v_first, p, i):
    B, T, D = u.shape
    prev = jnp.concatenate([shift_prev[:, None, :].astype(u.dtype), u[:, :-1]], axis=1)
    xx = prev - u
    xm = u[:, :, None, :] + xx[:, :, None, :] * p['rwkv_mu'][i]
    xr, xw, xk, xv, xa, xg = [xm[:, :, j] for j in range(6)]
    r = xr @ p['rwkv_wr'][i]
    k = xk @ p['rwkv_wk'][i]
    v = xv @ p['rwkv_wv'][i]
    w = -jax.nn.softplus(-(p['rwkv_w0'][i] + jnp.tanh(xw @ p['rwkv_w1'][i]) @ p['rwkv_w2'][i])) - 0.5
    if v_first is None:
        v_first = v
    else:
        j = i - 1
        v = v + (v_first - v) * jax.nn.sigmoid(p['rwkv_v0'][j] + (xv @ p['rwkv_v1'][j]) @ p['rwkv_v2'][j])
    a = jax.nn.sigmoid(p['rwkv_a0'][i] + (xa @ p['rwkv_a1'][i]) @ p['rwkv_a2'][i])
    g = jax.nn.sigmoid(xg @ p['rwkv_g1'][i]) @ p['rwkv_g2'][i]

    def hs(t):
        return t.astype(F32).reshape(B, T, C_HEADS, C_HEAD)

    kk = hs(k * p['rwkv_kk'][i])
    kk = kk / jnp.maximum(jnp.sqrt(jnp.sum(kk * kk, axis=-1, keepdims=True)), 1e-12)
    k = k * (1 + (a - 1) * p['rwkv_ka'][i])
    rh, kh, vh, ah = hs(r), hs(k), hs(v), hs(a)
    decay = jnp.exp(-jnp.exp(hs(w)))

    def tm(t):
        return jnp.swapaxes(t, 0, 1)

    def step(S, inp):
        r_t, d_t, k_t, v_t, a_t, b_t = inp
        sa = jnp.einsum('bhij,bhj->bhi', S, a_t)
        S = S * d_t[:, :, None, :] + sa[..., None] * b_t[:, :, None, :] + v_t[..., None] * k_t[:, :, None, :]
        return S, jnp.einsum('bhij,bhj->bhi', S, r_t)

    S_T, ys = lax.scan(step, S0.astype(F32), (tm(rh), tm(decay), tm(kh), tm(vh), tm(-kk), tm(kk * ah)))
    ys = tm(ys)
    mean = jnp.mean(ys, axis=-1, keepdims=True)
    var = jnp.mean(jnp.square(ys - mean), axis=-1, keepdims=True)
    yn = ((ys - mean) * lax.rsqrt(var + GN_EPS)).reshape(B, T, D)
    yn = yn * p['rwkv_lnx_g'][i].astype(F32) + p['rwkv_lnx_b'][i].astype(F32)
    bonus = (jnp.sum(rh * kh * p['rwkv_rk'][i].astype(F32), axis=-1, keepdims=True) * vh).reshape(B, T, D)
    out = ((yn + bonus).astype(u.dtype) * g) @ p['rwkv_wo'][i]
    return out, u[:, -1], S_T.astype(S0.dtype), v_first


def trunk(x, pos, attend, conv_buf, wkv, shift, p):
    h = x
    Bn, T = x.shape[:2]
    new_k, new_v, new_conv, new_wkv, new_shift = [], [], [], [], []
    v_first = None
    splits = [A_WIDTH, 2 * A_WIDTH, 3 * A_WIDTH, 3 * A_WIDTH + B_WIDTH]
    for l in range(DEPTH):
        u = rmsnorm(h, p['norm_mix'][l])
        i = l // 2
        if l % 2 == 0:
            z = u @ p['w_in_ab'][i]
            q, k, v, gv, gg = jnp.split(z, splits, axis=-1)
            q = rope_partial(q.reshape(Bn, T, A_HEADS, 2, A_QK_DIM), pos)
            k = rope_partial(k.reshape(Bn, T, A_HEADS, 2, A_QK_DIM), pos)
            v = v.reshape(Bn, T, A_HEADS, A_V_DIM)
            lam, lam_init = diff_lambda(p['diff_lambda'][i], l)
            o = attend(i, q, k, v, lam)
            o = rmsnorm(o, p['subln'][i], SUBLN_EPS) * (1.0 - lam_init)
            c, buf = conv_module(gv, gg, conv_buf[i], p['conv_w'][i], p['conv_b'][i],
                                 p['conv_ln_g'][i], p['conv_ln_b'][i])
            mix = jnp.concatenate([o.reshape(Bn, T, A_WIDTH), c], axis=-1) @ p['w_out_ab'][i]
            new_k.append(k.reshape(Bn, T, A_HEADS, 2 * A_QK_DIM))
            new_v.append(v)
            new_conv.append(buf)
        else:
            mix, s_last, S_T, v_first = rwkv7_mix(u, shift[i], wkv[i], v_first, p, i)
            new_wkv.append(S_T)
            new_shift.append(s_last)
        h = h + mix
        f = rmsnorm(h, p['norm_ffn'][l])
        h = h + (jax.nn.silu(f @ p['w_gate'][l]) * (f @ p['w_up'][l])) @ p['w_down'][l]
    y = rmsnorm(h, p['norm_final'])
    return y, jnp.stack(new_k), jnp.stack(new_v), jnp.stack(new_conv), jnp.stack(new_wkv), jnp.stack(new_shift)


def setup_inputs(seed: int = 0) -> dict:
    key = jax.random.key(seed)
    ks = iter(jax.random.split(key, 64))

    def nrm(shape, scale):
        return jax.random.normal(next(ks), shape, F32) * scale

    D = D_MODEL
    n_pages = PAST_LEN // PAGE_SIZE
    n_used = DEC_BATCH * n_pages
    n_pool = n_used + max(1, n_used // 4)
    page_table = jax.random.permutation(next(ks), n_pool)[:n_used].reshape(DEC_BATCH, n_pages).astype(jnp.int32)
    return {
        'x_prompt': nrm((BATCH, SEQ, D), 1.0),
        'x_sample': nrm((DEC_BATCH, DEC_SEQ, D), 1.0),
        'cache_k': nrm((N_A_LAYERS, n_pool, PAGE_SIZE, A_HEADS, 2 * A_QK_DIM), 1.0),
        'cache_v': nrm((N_A_LAYERS, n_pool, PAGE_SIZE, A_HEADS, A_V_DIM), 1.0),
        'state_conv': nrm((N_A_LAYERS, DEC_BATCH, CONV_WIDTH - 1, B_WIDTH), 0.5),
        'state_wkv': nrm((N_C_LAYERS, DEC_BATCH, C_HEADS, C_HEAD, C_HEAD), 0.3),
        'state_shift': nrm((N_C_LAYERS, DEC_BATCH, D), 1.0),
        'page_table': page_table,
        'norm_mix': 1.0 + nrm((DEPTH, D), 0.05),
        'norm_ffn': 1.0 + nrm((DEPTH, D), 0.05),
        'norm_final': 1.0 + nrm((D,), 0.05),
        'w_in_ab': nrm((N_A_LAYERS, D, IN_AB), D ** -0.5),
        'diff_lambda': nrm((N_A_LAYERS, 4, A_QK_DIM), 0.1),
        'subln': 1.0 + nrm((N_A_LAYERS, A_V_DIM), 0.05),
        'conv_w': nrm((N_A_LAYERS, CONV_WIDTH, B_WIDTH), CONV_WIDTH ** -0.5),
        'conv_b': nrm((N_A_LAYERS, B_WIDTH), 0.02),
        'conv_ln_g': 1.0 + nrm((N_A_LAYERS, B_WIDTH), 0.05),
        'conv_ln_b': nrm((N_A_LAYERS, B_WIDTH), 0.02),
        'w_out_ab': nrm((N_A_LAYERS, D, D), D ** -0.5),
        'rwkv_mu': jax.random.uniform(next(ks), (N_C_LAYERS, 6, D), F32),
        'rwkv_wr': nrm((N_C_LAYERS, D, D), D ** -0.5),
        'rwkv_wk': nrm((N_C_LAYERS, D, D), D ** -0.5),
        'rwkv_wv': nrm((N_C_LAYERS, D, D), D ** -0.5),
        'rwkv_w0': -2.5 + nrm((N_C_LAYERS, D), 1.0),
        'rwkv_w1': nrm((N_C_LAYERS, D, LORA_DECAY), D ** -0.5),
        'rwkv_w2': nrm((N_C_LAYERS, LORA_DECAY, D), 0.1 * LORA_DECAY ** -0.5),
        'rwkv_a0': nrm((N_C_LAYERS, D), 0.1),
        'rwkv_a1': nrm((N_C_LAYERS, D, LORA_A), D ** -0.5),
        'rwkv_a2': nrm((N_C_LAYERS, LORA_A, D), 0.1 * LORA_A ** -0.5),
        'rwkv_v0': 1.0 + nrm((N_C_LAYERS - 1, D), 0.1),
        'rwkv_v1': nrm((N_C_LAYERS - 1, D, LORA_MV), D ** -0.5),
        'rwkv_v2': nrm((N_C_LAYERS - 1, LORA_MV, D), 0.1 * LORA_MV ** -0.5),
        'rwkv_g1': nrm((N_C_LAYERS, D, LORA_GATE), D ** -0.5),
        'rwkv_g2': nrm((N_C_LAYERS, LORA_GATE, D), LORA_GATE ** -0.5),
        'rwkv_kk': 0.85 + nrm((N_C_LAYERS, D), 0.05),
        'rwkv_ka': 1.0 + nrm((N_C_LAYERS, D), 0.05),
        'rwkv_rk': nrm((N_C_LAYERS, C_HEADS, C_HEAD), 0.1),
        'rwkv_lnx_g': 1.0 + nrm((N_C_LAYERS, D), 0.05),
        'rwkv_lnx_b': nrm((N_C_LAYERS, D), 0.02),
        'rwkv_wo': nrm((N_C_LAYERS, D, D), D ** -0.5),
        'w_gate': nrm((DEPTH, D, D_FF), D ** -0.5),
        'w_up': nrm((DEPTH, D, D_FF), D ** -0.5),
        'w_down': nrm((DEPTH, D_FF, D), D_FF ** -0.5),
    }


def reference(x_prompt, x_sample, cache_k, cache_v, state_conv, state_wkv, state_shift, page_table,
              norm_mix, norm_ffn, norm_final, w_in_ab, diff_lambda, subln, conv_w, conv_b, conv_ln_g, conv_ln_b,
              w_out_ab, rwkv_mu, rwkv_wr, rwkv_wk, rwkv_wv, rwkv_w0, rwkv_w1, rwkv_w2, rwkv_a0, rwkv_a1, rwkv_a2,
              rwkv_v0, rwkv_v1, rwkv_v2, rwkv_g1, rwkv_g2, rwkv_kk, rwkv_ka, rwkv_rk, rwkv_lnx_g, rwkv_lnx_b,
              rwkv_wo, w_gate, w_up, w_down):
    p = dict(norm_mix=norm_mix, norm_ffn=norm_ffn, norm_final=norm_final, w_in_ab=w_in_ab,
             diff_lambda=diff_lambda, subln=subln, conv_w=conv_w, conv_b=conv_b, conv_ln_g=conv_ln_g,
             conv_ln_b=conv_ln_b, w_out_ab=w_out_ab, rwkv_mu=rwkv_mu, rwkv_wr=rwkv_wr, rwkv_wk=rwkv_wk,
             rwkv_wv=rwkv_wv, rwkv_w0=rwkv_w0, rwkv_w1=rwkv_w1, rwkv_w2=rwkv_w2, rwkv_a0=rwkv_a0,
             rwkv_a1=rwkv_a1, rwkv_a2=rwkv_a2, rwkv_v0=rwkv_v0, rwkv_v1=rwkv_v1, rwkv_v2=rwkv_v2,
             rwkv_g1=rwkv_g1, rwkv_g2=rwkv_g2, rwkv_kk=rwkv_kk, rwkv_ka=rwkv_ka, rwkv_rk=rwkv_rk,
             rwkv_lnx_g=rwkv_lnx_g, rwkv_lnx_b=rwkv_lnx_b, rwkv_wo=rwkv_wo, w_gate=w_gate, w_up=w_up,
             w_down=w_down)

    Bp, Tp = x_prompt.shape[:2]
    dt = x_prompt.dtype
    zero_conv = jnp.zeros((N_A_LAYERS, Bp, CONV_WIDTH - 1, B_WIDTH), dt)
    zero_wkv = jnp.zeros((N_C_LAYERS, Bp, C_HEADS, C_HEAD, C_HEAD), dt)
    zero_shift = jnp.zeros((N_C_LAYERS, Bp, D_MODEL), dt)

    def attend_prompt(i, q, k, v, lam):
        return diff_attn_prompt(q, k, v, lam)

    y_prompt, nk_p, nv_p, nc_p, nw_p, ns_p = trunk(x_prompt, jnp.arange(Tp), attend_prompt,
                                                   zero_conv, zero_wkv, zero_shift, p)

    Bd, n_pages = page_table.shape
    past = n_pages * PAGE_SIZE

    def attend_sample(i, q, k, v, lam):
        kp = cache_k[i, page_table].reshape(Bd, past, A_HEADS, 2, A_QK_DIM)
        vp = cache_v[i, page_table].reshape(Bd, past, A_HEADS, A_V_DIM)
        return diff_attn_sample(q, k, v, kp, vp, lam)

    y_sample, nk_s, nv_s, nc_s, nw_s, ns_s = trunk(x_sample, past + jnp.arange(x_sample.shape[1]), attend_sample,
                                                   state_conv, state_wkv, state_shift, p)
    return (y_prompt, y_sample, nk_p, nv_p, nc_p, nw_p, ns_p, nk_s, nv_s, nc_s, nw_s, ns_s)
```

```python
import functools
import math

import jax
import jax.numpy as jnp
from jax import lax
from jax.experimental import pallas as pl
from jax.experimental.pallas import tpu as pltpu

F32 = jnp.float32
BF16 = jnp.bfloat16

NORM_EPS = 1e-6
SUBLN_EPS = 1e-5
LN_EPS = 1e-5
GN_EPS = 64e-5
ROPE_THETA = 500000.0
ROPE_DIM = 16

A_HEADS = 4
A_QK_DIM = 64
A_V_DIM = 128
A_WIDTH = 512
B_WIDTH = 512
CONV_WIDTH = 31
C_HEAD = 64
PAGE_SIZE = 128

LANES = 128
SUBLANES = 8
QUAD = 4 * C_HEAD
CHUNK = 64
CONV_HALO = 32
VMEM_LIMIT = 56 * 1024 * 1024


def _cp(*sem):
    return pltpu.CompilerParams(dimension_semantics=sem, vmem_limit_bytes=VMEM_LIMIT)


def _mm(a, b):
    return jnp.dot(a.astype(BF16), b.astype(BF16), preferred_element_type=F32)


def _mm_nt(a, b):
    return lax.dot_general(a.astype(BF16), b.astype(BF16), (((1,), (1,)), ((), ())), preferred_element_type=F32)


def _mm_tn(a, b):
    return lax.dot_general(a.astype(BF16), b.astype(BF16), (((0,), (0,)), ((), ())), preferred_element_type=F32)


def _split3(x):
    hi = x.astype(BF16)
    r1 = x - hi.astype(F32)
    mid = r1.astype(BF16)
    lo = (r1 - mid.astype(F32)).astype(BF16)
    return hi, mid, lo


def _sigmoid(x):
    return 1.0 / (1.0 + jnp.exp(-x))


def _rms(x, g, eps):
    ms = jnp.mean(x * x, axis=-1, keepdims=True)
    return x * lax.rsqrt(ms + eps) * g


def _inproj_kernel(x_ref, g_ref, w_ref, c_ref, s1_ref, s2_ref, q_ref, k_ref, v_ref, glu_ref):
    u = _rms(x_ref[...], g_ref[...], NORM_EPS).astype(BF16)

    def proj(j):
        return jnp.dot(u, w_ref[:, j * A_WIDTH:(j + 1) * A_WIDTH], preferred_element_type=F32)

    ct = jnp.concatenate([c_ref[...]] * A_HEADS, axis=1)
    s1t = jnp.concatenate([s1_ref[...]] * A_HEADS, axis=1)
    s2t = jnp.concatenate([s2_ref[...]] * A_HEADS, axis=1)
    half = ROPE_DIM // 2

    def rope(z):
        return z * ct + pltpu.roll(z, A_WIDTH - half, 1) * s1t + pltpu.roll(z, half, 1) * s2t

    q_ref[...] = rope(proj(0))
    k_ref[...] = rope(proj(1))
    v_ref[...] = proj(2)
    glu_ref[...] = proj(3) * _sigmoid(proj(4))


def _in_proj_ab(h, g, w_bf, tabs, tm):
    M, D = h.shape
    ntab = tabs[0].shape[0] // tm
    tab_spec = pl.BlockSpec((tm, LANES), lambda i: (i % ntab, 0))
    row_out = pl.BlockSpec((tm, A_WIDTH), lambda i: (i, 0))
    return pl.pallas_call(
        _inproj_kernel,
        grid=(M // tm,),
        in_specs=[pl.BlockSpec((tm, D), lambda i: (i, 0)),
                  pl.BlockSpec((1, D), lambda i: (0, 0)),
                  pl.BlockSpec(w_bf.shape, lambda i: (0, 0)),
                  tab_spec, tab_spec, tab_spec],
        out_specs=[row_out] * 4,
        out_shape=[jax.ShapeDtypeStruct((M, A_WIDTH), F32)] * 4,
        compiler_params=_cp("parallel"),
        name="in_proj_ab",
    )(h, g, w_bf, *tabs)


def _rope_tables(pos):
    half = ROPE_DIM // 2
    inv = 1.0 / (ROPE_THETA ** (jnp.arange(0, ROPE_DIM, 2, dtype=F32) / ROPE_DIM))
    ang = pos.astype(F32)[:, None] * inv[None, :]
    cos, sin = jnp.cos(ang), jnp.sin(ang)
    n = pos.shape[0]
    rest = A_QK_DIM - ROPE_DIM
    c = jnp.concatenate([cos, cos, jnp.ones((n, rest), F32)], axis=1)
    s1 = jnp.concatenate([-sin, jnp.zeros((n, A_QK_DIM - half), F32)], axis=1)
    s2 = jnp.concatenate([jnp.zeros((n, half), F32), sin, jnp.zeros((n, rest), F32)], axis=1)
    return tuple(jnp.concatenate([t, t], axis=1) for t in (c, s1, s2))


def _diff_lambda(lam_ref, lam_init):
    lp = lam_ref[...]
    s1 = jnp.sum(lp[0:1] * lp[1:2], axis=-1, keepdims=True)
    s2 = jnp.sum(lp[2:3] * lp[3:4], axis=-1, keepdims=True)
    return jnp.exp(s1) - jnp.exp(s2) + lam_init


def _subln(o, g, lam_init):
    return _rms(o, g, SUBLN_EPS) * (1.0 - lam_init)


def _flash_kernel(lam_ref, sub_ref, q_ref, k_ref, v_ref, o_ref, m_ref, l_ref, acc_ref, *, lam_init, tq):
    qi = pl.program_id(2)
    ki = pl.program_id(3)

    @pl.when(ki == 0)
    def _():
        m_ref[...] = jnp.full(m_ref.shape, -jnp.inf, F32)
        l_ref[...] = jnp.zeros(l_ref.shape, F32)
        acc_ref[...] = jnp.zeros(acc_ref.shape, F32)

    @pl.when(ki <= qi)
    def _():
        q = q_ref[...] * (A_QK_DIM ** -0.5)
        lane = lax.broadcasted_iota(jnp.int32, q.shape, 1)
        q1 = jnp.where(lane < A_QK_DIM, q, 0.0)
        q2 = jnp.where(lane >= A_QK_DIM, q, 0.0)
        kb = k_ref[...].astype(BF16)
        s = jnp.concatenate([_mm_nt(q1, kb), _mm_nt(q2, kb)], axis=0)
        row = lax.broadcasted_iota(jnp.int32, s.shape, 0)
        col = lax.broadcasted_iota(jnp.int32, s.shape, 1)
        row = jnp.where(row >= tq, row - tq, row)
        s = jnp.where((ki < qi) | (col <= row), s, -jnp.inf)
        m_prev = m_ref[...]
        m_new = jnp.maximum(m_prev, jnp.max(s, axis=1, keepdims=True))
        alpha = jnp.exp(m_prev - m_new)
        p = jnp.exp(s - m_new)
        l_ref[...] = alpha * l_ref[...] + jnp.sum(p, axis=1, keepdims=True)
        acc_ref[...] = alpha * acc_ref[...] + _mm(p, v_ref[...])
        m_ref[...] = m_new

    @pl.when(ki == qi)
    def _():
        o = acc_ref[...] / l_ref[...]
        lam = _diff_lambda(lam_ref, lam_init)
        o_ref[...] = _subln(o[:tq] - lam * o[tq:], sub_ref[...], lam_init)


def _flash_prompt(q, k, v, lam_p, subln, lam_init, B, T, tq):
    nq = T // tq
    kern = functools.partial(_flash_kernel, lam_init=lam_init, tq=tq)
    return pl.pallas_call(
        kern,
        grid=(B, A_HEADS, nq, nq),
        in_specs=[pl.BlockSpec(lam_p.shape, lambda b, h, i, j: (0, 0)),
                  pl.BlockSpec((1, A_V_DIM), lambda b, h, i, j: (0, 0)),
                  pl.BlockSpec((tq, A_V_DIM), lambda b, h, i, j: (b * nq + i, h)),
                  pl.BlockSpec((tq, A_V_DIM), lambda b, h, i, j: (b * nq + jnp.minimum(i, j), h)),
                  pl.BlockSpec((tq, A_V_DIM), lambda b, h, i, j: (b * nq + jnp.minimum(i, j), h))],
        out_specs=pl.BlockSpec((tq, A_V_DIM), lambda b, h, i, j: (b * nq + i, h)),
        out_shape=jax.ShapeDtypeStruct((B * T, A_WIDTH), F32),
        scratch_shapes=[pltpu.VMEM((2 * tq, 1), F32), pltpu.VMEM((2 * tq, 1), F32),
                        pltpu.VMEM((2 * tq, A_V_DIM), F32)],
        compiler_params=_cp("parallel", "parallel", "parallel", "arbitrary"),
        name="diff_attn_prompt",
    )(lam_p, subln, q, k, v)


def _paged_kernel(pt_ref, lam_ref, sub_ref, qbd_ref, kn_ref, vn_ref, *rest, lam_init, pp, tn):
    k_refs = rest[:pp]
    v_refs = rest[pp:2 * pp]
    o_ref, m_ref, l_ref, acc_ref = rest[2 * pp:]
    j = pl.program_id(1)
    nrow = qbd_ref.shape[1]

    @pl.when(j == 0)
    def _():
        m_ref[...] = jnp.full(m_ref.shape, -jnp.inf, F32)
        l_ref[...] = jnp.zeros(l_ref.shape, F32)
        acc_ref[...] = jnp.zeros(acc_ref.shape, F32)

    qb = (qbd_ref[0] * (A_QK_DIM ** -0.5)).astype(BF16)

    def update(s, vals):
        m_prev = m_ref[...]
        m_new = jnp.maximum(m_prev, jnp.max(s, axis=1, keepdims=True))
        alpha = jnp.exp(m_prev - m_new)
        p = jnp.exp(s - m_new)
        l_ref[...] = alpha * l_ref[...] + jnp.sum(p, axis=1, keepdims=True)
        acc_ref[...] = alpha * acc_ref[...] + _mm(p, vals)
        m_ref[...] = m_new

    s = jnp.concatenate([_mm_nt(qb, kr[0]) for kr in k_refs], axis=1)
    vals = jnp.concatenate([vr[0] for vr in v_refs], axis=0)
    update(s, vals)

    @pl.when(j == pl.num_programs(1) - 1)
    def _():
        sn = _mm_nt(qb, kn_ref[0])
        row = lax.broadcasted_iota(jnp.int32, sn.shape, 0)
        col = lax.broadcasted_iota(jnp.int32, sn.shape, 1)
        tok = row % tn
        update(jnp.where((col <= tok) & (col < tn), sn, -jnp.inf), vn_ref[0])
        o = acc_ref[...] / l_ref[...]
        lam = _diff_lambda(lam_ref, lam_init)
        half = nrow // 2
        d = o[:half] - lam * o[half:]
        r = lax.broadcasted_iota(jnp.int32, d.shape, 0)
        c = lax.broadcasted_iota(jnp.int32, d.shape, 1)
        d = jnp.where(c // A_V_DIM == r // tn, d, 0.0)
        e = d[:, 0:A_V_DIM]
        for hh in range(1, A_HEADS):
            e = e + d[:, hh * A_V_DIM:(hh + 1) * A_V_DIM]
        o_ref[0] = _subln(e, sub_ref[...], lam_init)


def _paged_sample(page_table, q, k_new, v_new, cache_k, cache_v, lam_p, subln, lam_init, Bd, tn):
    n_pages = page_table.shape[1]
    pp = math.gcd(n_pages, 8)
    nrow = 2 * A_HEADS * tn
    rep = jnp.arange(2 * A_HEADS)
    colblk = jnp.arange(A_WIDTH) // A_QK_DIM
    keep = (colblk[None, :] == (2 * (rep % A_HEADS) + rep // A_HEADS)[:, None]).astype(F32)
    q3 = q.reshape(Bd, tn, A_WIDTH)
    qbd = (q3[:, None, :, :] * keep[None, :, None, :]).reshape(Bd, nrow, A_WIDTH)
    pad = ((0, 0), (0, SUBLANES - tn), (0, 0))
    kn = jnp.pad(k_new.reshape(Bd, tn, A_WIDTH), pad)
    vn = jnp.pad(v_new.reshape(Bd, tn, A_WIDTH), pad)

    def page_spec(i):
        return pl.BlockSpec((1, PAGE_SIZE, A_WIDTH), lambda b, j, pt: (pt[b, j * pp + i], 0, 0))

    per_b = lambda b, j, pt: (b, 0, 0)
    kern = functools.partial(_paged_kernel, lam_init=lam_init, pp=pp, tn=tn)
    out = pl.pallas_call(
        kern,
        grid_spec=pltpu.PrefetchScalarGridSpec(
            num_scalar_prefetch=1,
            grid=(Bd, n_pages // pp),
            in_specs=[pl.BlockSpec(lam_p.shape, lambda b, j, pt: (0, 0)),
                      pl.BlockSpec((1, A_V_DIM), lambda b, j, pt: (0, 0)),
                      pl.BlockSpec((1, nrow, A_WIDTH), per_b),
                      pl.BlockSpec((1, SUBLANES, A_WIDTH), per_b),
                      pl.BlockSpec((1, SUBLANES, A_WIDTH), per_b)]
                     + [page_spec(i) for i in range(pp)] * 2,
            out_specs=pl.BlockSpec((1, A_HEADS * tn, A_V_DIM), per_b),
            scratch_shapes=[pltpu.VMEM((nrow, 1), F32), pltpu.VMEM((nrow, 1), F32),
                            pltpu.VMEM((nrow, A_WIDTH), F32)]),
        out_shape=jax.ShapeDtypeStruct((Bd, A_HEADS * tn, A_V_DIM), F32),
        compiler_params=_cp("parallel", "arbitrary"),
        name="diff_attn_paged",
    )(page_table, lam_p, subln, qbd, kn, vn, *([cache_k] * pp), *([cache_v] * pp))
    return out.reshape(Bd, A_HEADS, tn, A_V_DIM).transpose(0, 2, 1, 3).reshape(Bd * tn, A_WIDTH)


def _conv_post(y, b_ref, lg_ref, lb_ref):
    y = y + b_ref[...]
    mu = jnp.mean(y, axis=-1, keepdims=True)
    d = y - mu
    var = jnp.mean(d * d, axis=-1, keepdims=True)
    z = d * lax.rsqrt(var + LN_EPS) * lg_ref[...] + lb_ref[...]
    return z * _sigmoid(z)


def _conv_kernel(glu_ref, halo_ref, buf_ref, w_ref, b_ref, lg_ref, lb_ref, o_ref, xp_ref, *, tt, sub):
    j = pl.program_id(1)
    xp_ref[0:CONV_HALO, :] = jnp.where(j == 0, buf_ref[0], halo_ref[...])
    xp_ref[CONV_HALO:, :] = glu_ref[...]
    lead = CONV_HALO - (CONV_WIDTH - 1)
    w = w_ref[...]
    for r0 in range(0, tt, sub):
        acc = xp_ref[r0 + lead:r0 + lead + sub, :] * w[0:1]
        for tap in range(1, CONV_WIDTH):
            acc = acc + xp_ref[r0 + lead + tap:r0 + lead + tap + sub, :] * w[tap:tap + 1]
        o_ref[r0:r0 + sub, :] = _conv_post(acc, b_ref, lg_ref, lb_ref)


def _conv_prompt(glu, buf, w, b, lg, lb, B, T, tt):
    nt = T // tt
    hpt = tt // CONV_HALO
    bufp = jnp.pad(buf, ((0, 0), (CONV_HALO - (CONV_WIDTH - 1), 0), (0, 0)))
    vec = pl.BlockSpec((1, B_WIDTH), lambda bb, j: (0, 0))
    kern = functools.partial(_conv_kernel, tt=tt, sub=min(tt, 64))
    return pl.pallas_call(
        kern,
        grid=(B, nt),
        in_specs=[pl.BlockSpec((tt, B_WIDTH), lambda bb, j: (bb * nt + j, 0)),
                  pl.BlockSpec((CONV_HALO, B_WIDTH), lambda bb, j: (jnp.maximum((bb * nt + j) * hpt - 1, 0), 0)),
                  pl.BlockSpec((1, CONV_HALO, B_WIDTH), lambda bb, j: (bb, 0, 0)),
                  pl.BlockSpec((CONV_WIDTH, B_WIDTH), lambda bb, j: (0, 0)),
                  vec, vec, vec],
        out_specs=pl.BlockSpec((tt, B_WIDTH), lambda bb, j: (bb * nt + j, 0)),
        out_shape=jax.ShapeDtypeStruct((B * T, B_WIDTH), F32),
        scratch_shapes=[pltpu.VMEM((CONV_HALO + tt, B_WIDTH), F32)],
        compiler_params=_cp("parallel", "arbitrary"),
        name="conv_prompt",
    )(glu, glu, bufp, w, b, lg, lb)


def _conv_step_kernel(xp_ref, w_ref, b_ref, lg_ref, lb_ref, o_ref, *, tn):
    w = w_ref[...]
    for t in range(tn):
        acc = xp_ref[t] * w[0:1]
        for tap in range(1, CONV_WIDTH):
            acc = acc + xp_ref[t + tap] * w[tap:tap + 1]
        o_ref[t] = _conv_post(acc, b_ref, lg_ref, lb_ref)


def _conv_sample(glu, buf, w, b, lg, lb, Bd, tn):
    xp = jnp.concatenate([buf, glu.reshape(Bd, tn, B_WIDTH)], axis=1)
    out = pl.pallas_call(
        functools.partial(_conv_step_kernel, tn=tn),
        out_shape=jax.ShapeDtypeStruct((tn, Bd, B_WIDTH), F32),
        compiler_params=pltpu.CompilerParams(vmem_limit_bytes=VMEM_LIMIT),
        name="conv_sample",
    )(xp.transpose(1, 0, 2), w, b, lg, lb)
    return out.transpose(1, 0, 2).reshape(Bd * tn, B_WIDTH), xp[:, -(CONV_WIDTH - 1):]


def _outproj_kernel(*refs, n_in):
    res_ref, w_ref, o_ref = refs[n_in], refs[n_in + 1], refs[n_in + 2]
    acc = res_ref[...]
    k0 = 0
    for x_ref in refs[:n_in]:
        kw = x_ref.shape[1]
        acc = acc + jnp.dot(x_ref[...].astype(BF16), w_ref[k0:k0 + kw, :], preferred_element_type=F32)
        k0 += kw
    o_ref[...] = acc


def _out_proj(xs, res, w_bf, tm):
    M, D = res.shape
    return pl.pallas_call(
        functools.partial(_outproj_kernel, n_in=len(xs)),
        grid=(M // tm,),
        in_specs=[pl.BlockSpec((tm, x.shape[1]), lambda i: (i, 0)) for x in xs]
                 + [pl.BlockSpec((tm, D), lambda i: (i, 0)), pl.BlockSpec(w_bf.shape, lambda i: (0, 0))],
        out_specs=pl.BlockSpec((tm, D), lambda i: (i, 0)),
        out_shape=jax.ShapeDtypeStruct((M, D), F32),
        compiler_params=_cp("parallel"),
        name="out_proj",
    )(*xs, res, w_bf)


def _ffn_kernel(h_ref, g_ref, wg_ref, wu_ref, wd_ref, *rest, final):
    h = h_ref[...]
    f = _rms(h, g_ref[...], NORM_EPS).astype(BF16)
    gate = jnp.dot(f, wg_ref[...], preferred_element_type=F32)
    up = jnp.dot(f, wu_ref[...], preferred_element_type=F32)
    act = (gate * _sigmoid(gate) * up).astype(BF16)
    out = h + jnp.dot(act, wd_ref[...], preferred_element_type=F32)
    if final:
        gf_ref, o_ref, y_ref = rest
        y_ref[...] = _rms(out, gf_ref[...], NORM_EPS)
    else:
        (o_ref,) = rest
    o_ref[...] = out


def _ffn(h, g, wg, wu, wd, tm, g_final=None):
    M, D = h.shape
    final = g_final is not None
    row = pl.BlockSpec((tm, D), lambda i: (i, 0))
    vec = pl.BlockSpec((1, D), lambda i: (0, 0))
    full = lambda a: pl.BlockSpec(a.shape, lambda i: (0, 0), pipeline_mode=pl.Buffered(1))
    ins = [h, g, wg, wu, wd] + ([g_final] if final else [])
    in_specs = [row, vec, full(wg), full(wu), full(wd)] + ([vec] if final else [])
    sds = jax.ShapeDtypeStruct((M, D), F32)
    return pl.pallas_call(
        functools.partial(_ffn_kernel, final=final),
        grid=(M // tm,),
        in_specs=in_specs,
        out_specs=[row, row] if final else row,
        out_shape=[sds, sds] if final else sds,
        compiler_params=_cp("parallel"),
        name="ffn",
    )(*ins)


def _rwkv_proj_kernel(*refs, has_vmix, seq_tiles, short_t):
    it = iter(refs)
    h_ref, hprev_ref, shift_ref, g_ref, mu_ref = next(it), next(it), next(it), next(it), next(it)
    wr_ref, wk_ref, wv_ref = next(it), next(it), next(it)
    w0_ref, w1_ref, w2_ref = next(it), next(it), next(it)
    a0_ref, a1_ref, a2_ref = next(it), next(it), next(it)
    g1_ref, g2_ref = next(it), next(it)
    if has_vmix:
        v0_ref, v1_ref, v2_ref, vfirst_ref = next(it), next(it), next(it), next(it)
    u_ref, r_ref, k_ref, v_ref, ld_ref, a_ref, gg_ref = (next(it) for _ in range(7))

    i = pl.program_id(0)
    u = _rms(h_ref[...], g_ref[...], NORM_EPS)
    u_ref[...] = u
    rolled = pltpu.roll(u, 1, 0)
    row = lax.broadcasted_iota(jnp.int32, u.shape, 0)
    if short_t:
        prev = jnp.where(row % short_t == 0, shift_ref[...], rolled)
    else:
        up8 = _rms(hprev_ref[...], g_ref[...], NORM_EPS)
        first = jnp.where(i % seq_tiles == 0, shift_ref[0], up8[SUBLANES - 1:SUBLANES])
        prev = jnp.where(row == 0, first, rolled)
    xx = prev - u
    mu = mu_ref[...]
    xr, xw, xk, xv, xa, xg = (u + xx * mu[j:j + 1] for j in range(6))

    r_ref[...] = _mm(xr, wr_ref[...])
    k_ref[...] = _mm(xk, wk_ref[...])
    v = _mm(xv, wv_ref[...])
    wl = w0_ref[...] + _mm(jnp.tanh(_mm(xw, w1_ref[...])), w2_ref[...])
    z = -wl
    w = -(jnp.maximum(z, 0.0) + jnp.log(1.0 + jnp.exp(-jnp.abs(z)))) - 0.5
    ld_ref[...] = -jnp.exp(w)
    if has_vmix:
        mix = _sigmoid(v0_ref[...] + _mm(_mm(xv, v1_ref[...]), v2_ref[...]))
        v = v + (vfirst_ref[...] - v) * mix
    v_ref[...] = v
    a_ref[...] = _sigmoid(a0_ref[...] + _mm(_mm(xa, a1_ref[...]), a2_ref[...]))
    gg_ref[...] = _mm(_sigmoid(_mm(xg, g1_ref[...])), g2_ref[...])


def _rwkv_proj(h, shift, g, mu, wts, vmix, tm, B, T):
    M, D = h.shape
    short = T < tm
    row = pl.BlockSpec((tm, D), lambda i: (i, 0))
    vec = pl.BlockSpec((1, D), lambda i: (0, 0))
    full = lambda a: pl.BlockSpec(a.shape, lambda i: (0, 0))
    if short:
        seq_tiles = 1
        shift_rows = jnp.zeros((B, T, D), F32).at[:, 0].set(shift).reshape(M, D)
        shift_in, shift_spec = shift_rows, row
        hprev_spec = pl.BlockSpec((SUBLANES, D), lambda i: (0, 0))
    else:
        seq_tiles = T // tm
        shift_in = shift.reshape(B, 1, D)
        shift_spec = pl.BlockSpec((1, 1, D), lambda i: (i // seq_tiles, 0, 0))
        hprev_spec = pl.BlockSpec((SUBLANES, D), lambda i: (jnp.maximum(i * (tm // SUBLANES) - 1, 0), 0))
    ins = [h, h, shift_in, g, mu] + list(wts)
    in_specs = [row, hprev_spec, shift_spec, vec, full(mu)] + [full(a) for a in wts]
    if vmix is not None:
        v0, v1, v2, v_first = vmix
        ins += [v0, v1, v2, v_first]
        in_specs += [vec, full(v1), full(v2), row]
    sds = jax.ShapeDtypeStruct((M, D), F32)
    kern = functools.partial(_rwkv_proj_kernel, has_vmix=vmix is not None, seq_tiles=seq_tiles,
                             short_t=T if short else 0)
    return pl.pallas_call(
        kern,
        grid=(M // tm,),
        in_specs=in_specs,
        out_specs=[row] * 7,
        out_shape=[sds] * 7,
        compiler_params=_cp("parallel"),
        name="rwkv_proj",
    )(*ins)


def _scan_kernel(r_ref, k_ref, v_ref, ld_ref, a_ref, g_ref, kkp_ref, ka_ref, rk_ref, lg_ref, lb_ref, h0_ref,
                 z_ref, hout_ref, bdh_ref):
    c = pl.program_id(2)
    C = CHUNK
    rq = lax.broadcasted_iota(jnp.int32, (QUAD, QUAD), 0)
    cq = lax.broadcasted_iota(jnp.int32, (QUAD, QUAD), 1)
    same_head = (rq // C_HEAD) == (cq // C_HEAD)
    block_ones = jnp.where(same_head, 1.0, 0.0).astype(BF16)

    def bd(x):
        return jnp.where(same_head, jnp.concatenate([x] * 4, axis=0), 0.0)

    @pl.when(c == 0)
    def _():
        hc = h0_ref[0, 0]
        bdh_ref[...] = jnp.where(same_head, jnp.concatenate([hc] * 4, axis=1), 0.0)

    r, k, v, ld, a = r_ref[...], k_ref[...], v_ref[...], ld_ref[...], a_ref[...]

    tt = lax.broadcasted_iota(jnp.int32, (C, C), 0)
    ts = lax.broadcasted_iota(jnp.int32, (C, C), 1)
    tri = jnp.where(ts <= tt, 1.0, 0.0).astype(BF16)
    ones_cl = jnp.ones((C, LANES), BF16)
    parts = _split3(ld)
    lc = sum(jnp.dot(tri, p, preferred_element_type=F32) for p in parts)
    lcol = sum(lax.dot_general(p, ones_cl, (((0,), (0,)), ((), ())), preferred_element_type=F32)
               for p in parts)
    lend = lc[C - 1:C, :]
    gam = jnp.exp(lc)
    gam_prev = jnp.exp(lc - ld)
    gam_inv = jnp.exp(-lc)
    gam_end = jnp.exp(lend - lc)
    gam_rows = jnp.exp(jnp.concatenate([lcol, lcol], axis=1))

    kk = k * kkp_ref[...]
    ss = _mm(kk * kk, block_ones)
    kkn = kk / jnp.maximum(jnp.sqrt(ss), 1e-12)
    bv = kkn * a
    k2 = k * (1.0 + (a - 1.0) * ka_ref[...])
    at = -kkn * gam_prev
    rt = r * gam
    lhs = jnp.concatenate([at, rt], axis=0)
    ab = _mm_nt(lhs, bd(bv * gam_inv))
    ak = _mm_nt(lhs, bd(k2 * gam_inv))
    t_i = lax.broadcasted_iota(jnp.int32, (C, QUAD), 0)
    s_i = lax.broadcasted_iota(jnp.int32, (C, QUAD), 1) % C_HEAD
    strict = s_i < t_i
    incl = s_i <= t_i
    a_ab = jnp.where(strict, ab[:C], 0.0)
    m_rb = jnp.where(incl, ab[C:], 0.0)
    a_ak = jnp.where(strict, ak[:C], 0.0)
    m_rk = jnp.where(incl, ak[C:], 0.0)

    p = a_ab
    tinv = jnp.where(s_i == t_i, 1.0, 0.0) + p
    n_sq = C.bit_length() - 2
    p = _mm(p, bd(p))
    for jj in range(n_sq):
        w = bd(p)
        if jj < n_sq - 1:
            pt = _mm(jnp.concatenate([p, tinv], axis=0), w)
            p = pt[:C]
            tinv = tinv + pt[C:]
        else:
            tinv = tinv + _mm(tinv, w)

    avrv = _mm(jnp.concatenate([a_ak, m_rk], axis=0), bd(v))
    x2, rv = avrv[:C], avrv[C:]
    w1 = _mm(tinv, bd(at))
    w2 = _mm(tinv, bd(x2))
    bd_mrb_w1 = bd(w1)
    y1 = rt + _mm(m_rb, bd_mrb_w1)
    y2 = rv + _mm(m_rb, bd(w2))
    be = bv * gam_end
    ke = k2 * gam_end
    pm = jnp.where(same_head, _mm_tn(be, w1), 0.0)
    qm = jnp.where(same_head, _mm_tn(be, w2) + _mm_tn(ke, v), 0.0)

    hmat = bdh_ref[...]
    y = _mm(y1, hmat) + y2
    hnew = gam_rows * hmat + _mm(pm, hmat) + qm
    bdh_ref[...] = hnew

    inv_n = 1.0 / C_HEAD
    mean = _mm(y, block_ones) * inv_n
    d = y - mean
    var = _mm(d * d, block_ones) * inv_n
    yn = d * lax.rsqrt(var + GN_EPS) * lg_ref[...] + lb_ref[...]
    bonus = _mm(r * k2 * rk_ref[...], block_ones) * v
    z_ref[...] = (yn + bonus) * g_ref[...]

    @pl.when(c == pl.num_programs(2) - 1)
    def _():
        hc = hnew[:, 0:C_HEAD]
        for hh in range(1, 4):
            hc = hc + hnew[:, hh * C_HEAD:(hh + 1) * C_HEAD]
        hout_ref[0, 0] = hc


def _rwkv_scan(r, k, v, ld, a, g, kkp, ka, rk, lg, lb, state, B, T):
    D = r.shape[1]
    nq = D // QUAD
    nc = T // CHUNK
    h0 = jnp.swapaxes(state, -1, -2).reshape(B, nq, QUAD, C_HEAD)
    seq = pl.BlockSpec((CHUNK, QUAD), lambda b, q, c: (b * nc + c, q))
    vec = pl.BlockSpec((1, QUAD), lambda b, q, c: (0, q))
    st = pl.BlockSpec((1, 1, QUAD, C_HEAD), lambda b, q, c: (b, q, 0, 0))
    z, hout = pl.pallas_call(
        _scan_kernel,
        grid=(B, nq, nc),
        in_specs=[seq] * 6 + [vec] * 5 + [st],
        out_specs=[seq, st],
        out_shape=[jax.ShapeDtypeStruct((B * T, D), F32), jax.ShapeDtypeStruct((B, nq, QUAD, C_HEAD), F32)],
        scratch_shapes=[pltpu.VMEM((QUAD, QUAD), F32)],
        compiler_params=_cp("parallel", "parallel", "arbitrary"),
        name="rwkv_scan",
    )(r, k, v, ld, a, g, kkp, ka, rk, lg, lb, h0)
    new_state = jnp.swapaxes(hout.reshape(B, D // C_HEAD, C_HEAD, C_HEAD), -1, -2)
    return z, new_state


def _row_tile(M, pref):
    return pref if M % pref == 0 else M


def _trunk(x, pos, conv_buf, wkv, shift, p, paged):
    B, T, D = x.shape
    M = B * T
    tm = _row_tile(M, 256)
    h = x.reshape(M, D)
    depth = p['norm_mix'].shape[0]
    tabs = _rope_tables(pos)
    if paged is not None:
        tabs = tuple(jnp.tile(t, (B, 1)) for t in tabs)
    vec = lambda a: a.reshape(1, -1)
    new_k, new_v, new_conv, new_wkv, new_shift = [], [], [], [], []
    v_first = None
    y = None
    for l in range(depth):
        i = l // 2
        if l % 2 == 0:
            q, k, v, glu = _in_proj_ab(h, vec(p['norm_mix'][l]), p['w_in_ab'][i], tabs, tm)
            lam_init = 0.8 - 0.6 * math.exp(-0.3 * l)
            lam_p, sub = p['diff_lambda'][i], vec(p['subln'][i])
            cw, cb = p['conv_w'][i], vec(p['conv_b'][i])
            clg, clb = vec(p['conv_ln_g'][i]), vec(p['conv_ln_b'][i])
            if paged is None:
                o = _flash_prompt(q, k, v, lam_p, sub, lam_init, B, T, _row_tile(T, 512))
                cm = _conv_prompt(glu, conv_buf[i], cw, cb, clg, clb, B, T, _row_tile(T, 256))
                buf = glu.reshape(B, T, B_WIDTH)[:, -(CONV_WIDTH - 1):]
            else:
                page_table, cache_k, cache_v = paged
                ck = cache_k[i].reshape(cache_k.shape[1], PAGE_SIZE, A_WIDTH)
                cv = cache_v[i].reshape(cache_v.shape[1], PAGE_SIZE, A_WIDTH)
                o = _paged_sample(page_table, q, k, v, ck, cv, lam_p, sub, lam_init, B, T)
                cm, buf = _conv_sample(glu, conv_buf[i], cw, cb, clg, clb, B, T)
            h = _out_proj([o, cm], h, p['w_out_ab'][i], tm)
            new_k.append(k.reshape(B, T, A_HEADS, 2 * A_QK_DIM))
            new_v.append(v.reshape(B, T, A_HEADS, A_V_DIM))
            new_conv.append(buf)
        else:
            wts = [p['rwkv_wr'][i], p['rwkv_wk'][i], p['rwkv_wv'][i],
                   vec(p['rwkv_w0'][i]), p['rwkv_w1'][i], p['rwkv_w2'][i],
                   vec(p['rwkv_a0'][i]), p['rwkv_a1'][i], p['rwkv_a2'][i],
                   p['rwkv_g1'][i], p['rwkv_g2'][i]]
            vmix = None
            if v_first is not None:
                vmix = (vec(p['rwkv_v0'][i - 1]), p['rwkv_v1'][i - 1], p['rwkv_v2'][i - 1], v_first)
            u, r, k, v, ld, a, g = _rwkv_proj(h, shift[i], vec(p['norm_mix'][l]), p['rwkv_mu'][i], wts, vmix,
                                              tm, B, T)
            if v_first is None:
                v_first = v
            tp = -(-T // CHUNK) * CHUNK
            seqs = (r, k, v, ld, a, g)
            if tp != T:
                seqs = tuple(jnp.pad(s.reshape(B, T, D), ((0, 0), (0, tp - T), (0, 0))).reshape(B * tp, D)
                             for s in seqs)
            z, s_new = _rwkv_scan(*seqs, vec(p['rwkv_kk'][i]), vec(p['rwkv_ka'][i]), vec(p['rwkv_rk'][i]),
                                  vec(p['rwkv_lnx_g'][i]), vec(p['rwkv_lnx_b'][i]), wkv[i], B, tp)
            if tp != T:
                z = z.reshape(B, tp, D)[:, :T].reshape(M, D)
            h = _out_proj([z], h, p['rwkv_wo'][i], tm)
            new_wkv.append(s_new)
            new_shift.append(u.reshape(B, T, D)[:, -1])
        last = l == depth - 1
        res = _ffn(h, vec(p['norm_ffn'][l]), p['w_gate'][l], p['w_up'][l], p['w_down'][l], tm,
                   vec(p['norm_final']) if last else None)
        if last:
            h, y = res
        else:
            h = res
    return (y.reshape(B, T, D), jnp.stack(new_k), jnp.stack(new_v), jnp.stack(new_conv), jnp.stack(new_wkv),
            jnp.stack(new_shift))


_MATMUL_WEIGHTS = ('w_in_ab', 'w_out_ab', 'rwkv_wr', 'rwkv_wk', 'rwkv_wv', 'rwkv_w1', 'rwkv_w2', 'rwkv_a1',
                   'rwkv_a2', 'rwkv_v1', 'rwkv_v2', 'rwkv_g1', 'rwkv_g2', 'rwkv_wo', 'w_gate', 'w_up', 'w_down')


def kernel(x_prompt, x_sample, cache_k, cache_v, state_conv, state_wkv, state_shift, page_table, norm_mix, norm_ffn, norm_final, w_in_ab, diff_lambda, subln, conv_w, conv_b, conv_ln_g, conv_ln_b, w_out_ab, rwkv_mu, rwkv_wr, rwkv_wk, rwkv_wv, rwkv_w0, rwkv_w1, rwkv_w2, rwkv_a0, rwkv_a1, rwkv_a2, rwkv_v0, rwkv_v1, rwkv_v2, rwkv_g1, rwkv_g2, rwkv_kk, rwkv_ka, rwkv_rk, rwkv_lnx_g, rwkv_lnx_b, rwkv_wo, w_gate, w_up, w_down):
    p = dict(norm_mix=norm_mix, norm_ffn=norm_ffn, norm_final=norm_final, w_in_ab=w_in_ab,
             diff_lambda=diff_lambda, subln=subln, conv_w=conv_w, conv_b=conv_b, conv_ln_g=conv_ln_g,
             conv_ln_b=conv_ln_b, w_out_ab=w_out_ab, rwkv_mu=rwkv_mu, rwkv_wr=rwkv_wr, rwkv_wk=rwkv_wk,
             rwkv_wv=rwkv_wv, rwkv_w0=rwkv_w0, rwkv_w1=rwkv_w1, rwkv_w2=rwkv_w2, rwkv_a0=rwkv_a0,
             rwkv_a1=rwkv_a1, rwkv_a2=rwkv_a2, rwkv_v0=rwkv_v0, rwkv_v1=rwkv_v1, rwkv_v2=rwkv_v2,
             rwkv_g1=rwkv_g1, rwkv_g2=rwkv_g2, rwkv_kk=rwkv_kk, rwkv_ka=rwkv_ka, rwkv_rk=rwkv_rk,
             rwkv_lnx_g=rwkv_lnx_g, rwkv_lnx_b=rwkv_lnx_b, rwkv_wo=rwkv_wo, w_gate=w_gate, w_up=w_up,
             w_down=w_down)
    for name in _MATMUL_WEIGHTS:
        p[name] = p[name].astype(BF16)

    Bp, Tp, D = x_prompt.shape
    n_a, n_c = state_conv.shape[0], state_wkv.shape[0]
    zero_conv = jnp.zeros((n_a, Bp, CONV_WIDTH - 1, B_WIDTH), F32)
    zero_wkv = jnp.zeros((n_c, Bp) + state_wkv.shape[2:], F32)
    zero_shift = jnp.zeros((n_c, Bp, D), F32)
    outs_p = _trunk(x_prompt, jnp.arange(Tp), zero_conv, zero_wkv, zero_shift, p, None)

    past = page_table.shape[1] * PAGE_SIZE
    outs_s = _trunk(x_sample, past + jnp.arange(x_sample.shape[1]), state_conv, state_wkv, state_shift, p,
                    (page_table, cache_k, cache_v))
    return (outs_p[0], outs_s[0]) + tuple(outs_p[1:]) + tuple(outs_s[1:])
```

```python
import functools
import math

import jax
import jax.numpy as jnp
from jax import lax
from jax.experimental import pallas as pl
from jax.experimental.pallas import tpu as pltpu

F32 = jnp.float32
BF16 = jnp.bfloat16

NORM_EPS = 1e-6
SUBLN_EPS = 1e-5
LN_EPS = 1e-5
GN_EPS = 64e-5
ROPE_THETA = 500000.0
ROPE_DIM = 16

A_HEADS = 4
A_QK_DIM = 64
A_V_DIM = 128
A_WIDTH = 512
B_WIDTH = 512
CONV_WIDTH = 31
C_HEAD = 64
PAGE_SIZE = 128

LANES = 128
SUBLANES = 8
QUAD = 4 * C_HEAD
CHUNK = 64
SCAN_SEQS = 2
SCAN_QUADS = 4
CONV_HALO = 32
VMEM_LIMIT = 56 * 1024 * 1024


def _cp(*sem):
    return pltpu.CompilerParams(dimension_semantics=sem, vmem_limit_bytes=VMEM_LIMIT)


def _mm(a, b):
    return jnp.dot(a.astype(BF16), b.astype(BF16), preferred_element_type=F32)


def _mm_nt(a, b):
    return lax.dot_general(a.astype(BF16), b.astype(BF16), (((1,), (1,)), ((), ())), preferred_element_type=F32)


def _mm_tn(a, b):
    return lax.dot_general(a.astype(BF16), b.astype(BF16), (((0,), (0,)), ((), ())), preferred_element_type=F32)


def _split3(x):
    hi = x.astype(BF16)
    r1 = x - hi.astype(F32)
    mid = r1.astype(BF16)
    lo = (r1 - mid.astype(F32)).astype(BF16)
    return hi, mid, lo


def _sigmoid(x):
    return 1.0 / (1.0 + jnp.exp(-x))


def _rms(x, g, eps):
    ms = jnp.mean(x * x, axis=-1, keepdims=True)
    return x * lax.rsqrt(ms + eps) * g


def _inproj_kernel(x_ref, g_ref, w_ref, c_ref, s1_ref, s2_ref, q_ref, k_ref, v_ref, glu_ref):
    u = _rms(x_ref[...], g_ref[...], NORM_EPS).astype(BF16)

    def proj(j):
        return jnp.dot(u, w_ref[:, j * A_WIDTH:(j + 1) * A_WIDTH], preferred_element_type=F32)

    ct = jnp.concatenate([c_ref[...]] * A_HEADS, axis=1)
    s1t = jnp.concatenate([s1_ref[...]] * A_HEADS, axis=1)
    s2t = jnp.concatenate([s2_ref[...]] * A_HEADS, axis=1)
    half = ROPE_DIM // 2

    def rope(z):
        return z * ct + pltpu.roll(z, A_WIDTH - half, 1) * s1t + pltpu.roll(z, half, 1) * s2t

    q_ref[...] = rope(proj(0))
    k_ref[...] = rope(proj(1))
    v_ref[...] = proj(2)
    glu_ref[...] = proj(3) * _sigmoid(proj(4))


def _in_proj_ab(h, g, w_bf, tabs, tm):
    M, D = h.shape
    ntab = tabs[0].shape[0] // tm
    tab_spec = pl.BlockSpec((tm, LANES), lambda i: (i % ntab, 0))
    row_out = pl.BlockSpec((tm, A_WIDTH), lambda i: (i, 0))
    return pl.pallas_call(
        _inproj_kernel,
        grid=(M // tm,),
        in_specs=[pl.BlockSpec((tm, D), lambda i: (i, 0)),
                  pl.BlockSpec((1, D), lambda i: (0, 0)),
                  pl.BlockSpec(w_bf.shape, lambda i: (0, 0)),
                  tab_spec, tab_spec, tab_spec],
        out_specs=[row_out] * 4,
        out_shape=[jax.ShapeDtypeStruct((M, A_WIDTH), F32)] * 4,
        compiler_params=_cp("parallel"),
        name="in_proj_ab",
    )(h, g, w_bf, *tabs)


def _rope_tables(pos):
    half = ROPE_DIM // 2
    inv = 1.0 / (ROPE_THETA ** (jnp.arange(0, ROPE_DIM, 2, dtype=F32) / ROPE_DIM))
    ang = pos.astype(F32)[:, None] * inv[None, :]
    cos, sin = jnp.cos(ang), jnp.sin(ang)
    n = pos.shape[0]
    rest = A_QK_DIM - ROPE_DIM
    c = jnp.concatenate([cos, cos, jnp.ones((n, rest), F32)], axis=1)
    s1 = jnp.concatenate([-sin, jnp.zeros((n, A_QK_DIM - half), F32)], axis=1)
    s2 = jnp.concatenate([jnp.zeros((n, half), F32), sin, jnp.zeros((n, rest), F32)], axis=1)
    return tuple(jnp.concatenate([t, t], axis=1) for t in (c, s1, s2))


def _diff_lambda(lam_ref, lam_init):
    lp = lam_ref[...]
    s1 = jnp.sum(lp[0:1] * lp[1:2], axis=-1, keepdims=True)
    s2 = jnp.sum(lp[2:3] * lp[3:4], axis=-1, keepdims=True)
    return jnp.exp(s1) - jnp.exp(s2) + lam_init


def _subln(o, g, lam_init):
    return _rms(o, g, SUBLN_EPS) * (1.0 - lam_init)


def _flash_kernel(qi_ref, ki_ref, lam_ref, sub_ref, q_ref, k_ref, v_ref, o_ref, m_ref, l_ref, acc_ref, *,
                  lam_init, tq, tk, rb):
    pair = pl.program_id(2)
    qi = qi_ref[pair]
    ki = ki_ref[pair]
    nrb = tq // rb
    reps = tk // LANES

    @pl.when(ki == 0)
    def _():
        m_ref[...] = jnp.full(m_ref.shape, -jnp.inf, F32)
        l_ref[...] = jnp.zeros(l_ref.shape, F32)
        acc_ref[...] = jnp.zeros(acc_ref.shape, F32)

    def step(masked):
        kb = k_ref[...].astype(BF16)
        vb = v_ref[...].astype(BF16)
        for r in range(nrb):
            q = q_ref[r * rb:(r + 1) * rb, :] * (A_QK_DIM ** -0.5 * math.log2(math.e))
            lane = lax.broadcasted_iota(jnp.int32, q.shape, 1)
            q12 = jnp.concatenate([jnp.where(lane < A_QK_DIM, q, 0.0), jnp.where(lane >= A_QK_DIM, q, 0.0)],
                                  axis=0)
            s = _mm_nt(q12, kb)
            if masked:
                row = lax.broadcasted_iota(jnp.int32, s.shape, 0)
                col = lax.broadcasted_iota(jnp.int32, s.shape, 1)
                row = jnp.where(row >= rb, row - rb, row) + (qi * tq + r * rb)
                s = jnp.where(col + ki * tk <= row, s, -jnp.inf)
            m_prev = m_ref[r]
            m_new = jnp.maximum(m_prev, jnp.max(s, axis=1, keepdims=True))
            alpha = jnp.exp2(m_prev - m_new)
            p = jnp.exp2(s - jnp.tile(m_new, (1, reps)))
            l_ref[r] = alpha * l_ref[r] + jnp.sum(p, axis=1, keepdims=True)
            acc_ref[r] = alpha * acc_ref[r] + jnp.dot(p.astype(BF16), vb, preferred_element_type=F32)
            m_ref[r] = m_new

    crosses = (ki + 1) * tk - 1 > qi * tq

    @pl.when(crosses)
    def _():
        step(True)

    @pl.when(jnp.logical_not(crosses))
    def _():
        step(False)

    @pl.when(ki == (qi + 1) * (tq // tk) - 1)
    def _():
        lam = _diff_lambda(lam_ref, lam_init)
        for r in range(nrb):
            o = acc_ref[r] / l_ref[r]
            o_ref[r * rb:(r + 1) * rb, :] = _subln(o[:rb] - lam * o[rb:], sub_ref[...], lam_init)


def _flash_prompt(q, k, v, lam_p, subln, lam_init, B, T, tq, tk, rb):
    nq, nk, ratio = T // tq, T // tk, tq // tk
    pairs = [(i, j) for i in range(nq) for j in range((i + 1) * ratio)]
    qi_tab = jnp.array([pq for pq, _ in pairs], jnp.int32)
    ki_tab = jnp.array([pk for _, pk in pairs], jnp.int32)
    kern = functools.partial(_flash_kernel, lam_init=lam_init, tq=tq, tk=tk, rb=rb)
    kv_spec = pl.BlockSpec((tk, A_V_DIM), lambda b, h, s, qt, kt: (b * nk + kt[s], h))
    q_spec = pl.BlockSpec((tq, A_V_DIM), lambda b, h, s, qt, kt: (b * nq + qt[s], h))
    return pl.pallas_call(
        kern,
        grid_spec=pltpu.PrefetchScalarGridSpec(
            num_scalar_prefetch=2,
            grid=(B, A_HEADS, len(pairs)),
            in_specs=[pl.BlockSpec(lam_p.shape, lambda b, h, s, qt, kt: (0, 0)),
                      pl.BlockSpec((1, A_V_DIM), lambda b, h, s, qt, kt: (0, 0)),
                      q_spec, kv_spec, kv_spec],
            out_specs=q_spec,
            scratch_shapes=[pltpu.VMEM((tq // rb, 2 * rb, LANES), F32), pltpu.VMEM((tq // rb, 2 * rb, LANES), F32),
                            pltpu.VMEM((tq // rb, 2 * rb, A_V_DIM), F32)]),
        out_shape=jax.ShapeDtypeStruct((B * T, A_WIDTH), F32),
        compiler_params=_cp("parallel", "parallel", "arbitrary"),
        name="diff_attn_prompt",
    )(qi_tab, ki_tab, lam_p, subln, q, k, v)


def _paged_kernel(pt_ref, lam_ref, sub_ref, q_ref, kn_ref, vn_ref, *rest, lam_init, pp, tn):
    k_refs = rest[:pp]
    v_refs = rest[pp:2 * pp]
    o_ref, m_ref, l_ref, acc_ref = rest[2 * pp:]
    j = pl.program_id(1)
    hrows = 2 * SUBLANES

    @pl.when(j == 0)
    def _():
        m_ref[...] = jnp.full(m_ref.shape, -jnp.inf, F32)
        l_ref[...] = jnp.zeros(l_ref.shape, F32)
        acc_ref[...] = jnp.zeros(acc_ref.shape, F32)

    qb = (q_ref[0] * (A_QK_DIM ** -0.5)).astype(BF16)

    def update(keys_of, vals_of, mask=None):
        s = jnp.concatenate([_mm_nt(qb[h * hrows:(h + 1) * hrows], keys_of(h)) for h in range(A_HEADS)], axis=0)
        if mask is not None:
            s = jnp.where(mask(s.shape), s, -jnp.inf)
        m_prev = m_ref[...]
        m_new = jnp.maximum(m_prev, jnp.max(s, axis=1, keepdims=True))
        alpha = jnp.exp(m_prev - m_new)
        p = jnp.exp(s - m_new)
        l_ref[...] = alpha * l_ref[...] + jnp.sum(p, axis=1, keepdims=True)
        pv = jnp.concatenate([_mm(p[h * hrows:(h + 1) * hrows], vals_of(h)) for h in range(A_HEADS)], axis=0)
        acc_ref[...] = alpha * acc_ref[...] + pv
        m_ref[...] = m_new

    rows_of = lambda ref, h: ref[0, 0, pl.ds(h, PAGE_SIZE, stride=A_HEADS), :]
    update(lambda h: jnp.concatenate([rows_of(kr, h) for kr in k_refs], axis=0),
           lambda h: jnp.concatenate([rows_of(vr, h) for vr in v_refs], axis=0))

    @pl.when(j == pl.num_programs(1) - 1)
    def _():
        def causal(shape):
            tok = lax.broadcasted_iota(jnp.int32, shape, 0) % SUBLANES
            col = lax.broadcasted_iota(jnp.int32, shape, 1)
            return (col <= tok) & (col < tn)

        head = lambda ref, h: ref[0][:, h * A_V_DIM:(h + 1) * A_V_DIM]
        update(lambda h: head(kn_ref, h), lambda h: head(vn_ref, h), causal)
        o = acc_ref[...] / l_ref[...]
        lam = _diff_lambda(lam_ref, lam_init)
        for h in range(A_HEADS):
            d = o[h * hrows:h * hrows + SUBLANES] - lam * o[h * hrows + SUBLANES:(h + 1) * hrows]
            o_ref[0, h] = _subln(d, sub_ref[...], lam_init)


def _paged_sample(page_table, q, k_new, v_new, cache_k, cache_v, layer, lam_p, subln, lam_init, Bd, tn):
    n_pages = page_table.shape[1]
    pp = math.gcd(n_pages, 8)
    tpad = SUBLANES - tn
    lane_map = jnp.arange(A_V_DIM) // A_QK_DIM
    keep = (lane_map[None, :] == jnp.arange(2)[:, None]).astype(F32)
    q4 = q.reshape(Bd, tn, A_HEADS, A_V_DIM).transpose(0, 2, 1, 3)
    qm = q4[:, :, None, :, :] * keep[None, None, :, None, :]
    qm = jnp.pad(qm, ((0, 0), (0, 0), (0, 0), (0, tpad), (0, 0))).reshape(Bd, A_HEADS * 2 * SUBLANES, A_V_DIM)
    pad = ((0, 0), (0, tpad), (0, 0))
    kn = jnp.pad(k_new.reshape(Bd, tn, A_WIDTH), pad)
    vn = jnp.pad(v_new.reshape(Bd, tn, A_WIDTH), pad)

    page_rows = PAGE_SIZE * A_HEADS
    cache_k = cache_k.reshape(cache_k.shape[:2] + (page_rows, A_V_DIM))
    cache_v = cache_v.reshape(cache_v.shape[:2] + (page_rows, A_V_DIM))

    def page_spec(i):
        return pl.BlockSpec((1, 1, page_rows, A_V_DIM), lambda b, j, pt: (layer, pt[b, j * pp + i], 0, 0))

    per_b = lambda b, j, pt: (b, 0, 0)
    nrow = A_HEADS * 2 * SUBLANES
    kern = functools.partial(_paged_kernel, lam_init=lam_init, pp=pp, tn=tn)
    out = pl.pallas_call(
        kern,
        grid_spec=pltpu.PrefetchScalarGridSpec(
            num_scalar_prefetch=1,
            grid=(Bd, n_pages // pp),
            in_specs=[pl.BlockSpec(lam_p.shape, lambda b, j, pt: (0, 0)),
                      pl.BlockSpec((1, A_V_DIM), lambda b, j, pt: (0, 0)),
                      pl.BlockSpec((1, nrow, A_V_DIM), per_b),
                      pl.BlockSpec((1, SUBLANES, A_WIDTH), per_b),
                      pl.BlockSpec((1, SUBLANES, A_WIDTH), per_b)]
                     + [page_spec(i) for i in range(pp)] * 2,
            out_specs=pl.BlockSpec((1, A_HEADS, SUBLANES, A_V_DIM), lambda b, j, pt: (b, 0, 0, 0)),
            scratch_shapes=[pltpu.VMEM((nrow, 1), F32), pltpu.VMEM((nrow, 1), F32),
                            pltpu.VMEM((nrow, A_V_DIM), F32)]),
        out_shape=jax.ShapeDtypeStruct((Bd, A_HEADS, SUBLANES, A_V_DIM), F32),
        compiler_params=_cp("parallel", "arbitrary"),
        name="diff_attn_paged",
    )(page_table, lam_p, subln, qm, kn, vn, *([cache_k] * pp), *([cache_v] * pp))
    return out[:, :, :tn].transpose(0, 2, 1, 3).reshape(Bd * tn, A_WIDTH)


def _conv_post(y, b_ref, lg_ref, lb_ref):
    y = y + b_ref[...]
    mu = jnp.mean(y, axis=-1, keepdims=True)
    d = y - mu
    var = jnp.mean(d * d, axis=-1, keepdims=True)
    z = d * lax.rsqrt(var + LN_EPS) * lg_ref[...] + lb_ref[...]
    return z * _sigmoid(z)


def _conv_kernel(glu_ref, halo_ref, buf_ref, w_ref, b_ref, lg_ref, lb_ref, o_ref, xp_ref, *, tt, sub):
    j = pl.program_id(1)
    xp_ref[0:CONV_HALO, :] = jnp.where(j == 0, buf_ref[0], halo_ref[...])
    xp_ref[CONV_HALO:, :] = glu_ref[...]
    lead = CONV_HALO - (CONV_WIDTH - 1)
    w = w_ref[...]
    for r0 in range(0, tt, sub):
        acc = xp_ref[r0 + lead:r0 + lead + sub, :] * w[0:1]
        for tap in range(1, CONV_WIDTH):
            acc = acc + xp_ref[r0 + lead + tap:r0 + lead + tap + sub, :] * w[tap:tap + 1]
        o_ref[r0:r0 + sub, :] = _conv_post(acc, b_ref, lg_ref, lb_ref)


def _conv_prompt(glu, buf, w, b, lg, lb, B, T, tt):
    nt = T // tt
    hpt = tt // CONV_HALO
    bufp = jnp.pad(buf, ((0, 0), (CONV_HALO - (CONV_WIDTH - 1), 0), (0, 0)))
    vec = pl.BlockSpec((1, B_WIDTH), lambda bb, j: (0, 0))
    kern = functools.partial(_conv_kernel, tt=tt, sub=min(tt, 64))
    return pl.pallas_call(
        kern,
        grid=(B, nt),
        in_specs=[pl.BlockSpec((tt, B_WIDTH), lambda bb, j: (bb * nt + j, 0)),
                  pl.BlockSpec((CONV_HALO, B_WIDTH), lambda bb, j: (jnp.maximum((bb * nt + j) * hpt - 1, 0), 0)),
                  pl.BlockSpec((1, CONV_HALO, B_WIDTH), lambda bb, j: (bb, 0, 0)),
                  pl.BlockSpec((CONV_WIDTH, B_WIDTH), lambda bb, j: (0, 0)),
                  vec, vec, vec],
        out_specs=pl.BlockSpec((tt, B_WIDTH), lambda bb, j: (bb * nt + j, 0)),
        out_shape=jax.ShapeDtypeStruct((B * T, B_WIDTH), F32),
        scratch_shapes=[pltpu.VMEM((CONV_HALO + tt, B_WIDTH), F32)],
        compiler_params=_cp("parallel", "arbitrary"),
        name="conv_prompt",
    )(glu, glu, bufp, w, b, lg, lb)


def _conv_step_kernel(xp_ref, w_ref, b_ref, lg_ref, lb_ref, o_ref, *, tn):
    w = w_ref[...]
    for t in range(tn):
        acc = xp_ref[t] * w[0:1]
        for tap in range(1, CONV_WIDTH):
            acc = acc + xp_ref[t + tap] * w[tap:tap + 1]
        o_ref[t] = _conv_post(acc, b_ref, lg_ref, lb_ref)


def _conv_sample(glu, buf, w, b, lg, lb, Bd, tn):
    xp = jnp.concatenate([buf, glu.reshape(Bd, tn, B_WIDTH)], axis=1)
    out = pl.pallas_call(
        functools.partial(_conv_step_kernel, tn=tn),
        out_shape=jax.ShapeDtypeStruct((tn, Bd, B_WIDTH), F32),
        compiler_params=pltpu.CompilerParams(vmem_limit_bytes=VMEM_LIMIT),
        name="conv_sample",
    )(xp.transpose(1, 0, 2), w, b, lg, lb)
    return out.transpose(1, 0, 2).reshape(Bd * tn, B_WIDTH), xp[:, -(CONV_WIDTH - 1):]


def _outproj_kernel(*refs, n_in):
    res_ref, w_ref, o_ref = refs[n_in], refs[n_in + 1], refs[n_in + 2]
    acc = res_ref[...]
    k0 = 0
    for x_ref in refs[:n_in]:
        kw = x_ref.shape[1]
        acc = acc + jnp.dot(x_ref[...].astype(BF16), w_ref[k0:k0 + kw, :], preferred_element_type=F32)
        k0 += kw
    o_ref[...] = acc


def _out_proj(xs, res, w_bf, tm):
    M, D = res.shape
    return pl.pallas_call(
        functools.partial(_outproj_kernel, n_in=len(xs)),
        grid=(M // tm,),
        in_specs=[pl.BlockSpec((tm, x.shape[1]), lambda i: (i, 0)) for x in xs]
                 + [pl.BlockSpec((tm, D), lambda i: (i, 0)), pl.BlockSpec(w_bf.shape, lambda i: (0, 0))],
        out_specs=pl.BlockSpec((tm, D), lambda i: (i, 0)),
        out_shape=jax.ShapeDtypeStruct((M, D), F32),
        compiler_params=_cp("parallel"),
        name="out_proj",
    )(*xs, res, w_bf)


def _ffn_kernel(h_ref, g_ref, wg_ref, wu_ref, wd_ref, *rest, final):
    h = h_ref[...]
    f = _rms(h, g_ref[...], NORM_EPS).astype(BF16)
    gate = jnp.dot(f, wg_ref[...], preferred_element_type=F32)
    up = jnp.dot(f, wu_ref[...], preferred_element_type=F32)
    act = (gate * _sigmoid(gate) * up).astype(BF16)
    out = h + jnp.dot(act, wd_ref[...], preferred_element_type=F32)
    if final:
        gf_ref, o_ref, y_ref = rest
        y_ref[...] = _rms(out, gf_ref[...], NORM_EPS)
    else:
        (o_ref,) = rest
    o_ref[...] = out


def _ffn(h, g, wg, wu, wd, tm, g_final=None):
    M, D = h.shape
    final = g_final is not None
    row = pl.BlockSpec((tm, D), lambda i: (i, 0))
    vec = pl.BlockSpec((1, D), lambda i: (0, 0))
    full = lambda a: pl.BlockSpec(a.shape, lambda i: (0, 0), pipeline_mode=pl.Buffered(1))
    ins = [h, g, wg, wu, wd] + ([g_final] if final else [])
    in_specs = [row, vec, full(wg), full(wu), full(wd)] + ([vec] if final else [])
    sds = jax.ShapeDtypeStruct((M, D), F32)
    return pl.pallas_call(
        functools.partial(_ffn_kernel, final=final),
        grid=(M // tm,),
        in_specs=in_specs,
        out_specs=[row, row] if final else row,
        out_shape=[sds, sds] if final else sds,
        compiler_params=_cp("parallel"),
        name="ffn",
    )(*ins)


def _rwkv_proj_kernel(*refs, has_vmix, seq_tiles, short_t):
    it = iter(refs)
    h_ref, hprev_ref, shift_ref, g_ref, mu_ref = next(it), next(it), next(it), next(it), next(it)
    wr_ref, wk_ref, wv_ref = next(it), next(it), next(it)
    w0_ref, w1_ref, w2_ref = next(it), next(it), next(it)
    a0_ref, a1_ref, a2_ref = next(it), next(it), next(it)
    g1_ref, g2_ref = next(it), next(it)
    if has_vmix:
        v0_ref, v1_ref, v2_ref, vfirst_ref = next(it), next(it), next(it), next(it)
    u_ref, r_ref, k_ref, v_ref, ld_ref, a_ref, gg_ref = (next(it) for _ in range(7))

    i = pl.program_id(0)
    u = _rms(h_ref[...], g_ref[...], NORM_EPS)
    u_ref[...] = u
    rolled = pltpu.roll(u, 1, 0)
    row = lax.broadcasted_iota(jnp.int32, u.shape, 0)
    if short_t:
        prev = jnp.where(row % short_t == 0, shift_ref[...], rolled)
    else:
        up8 = _rms(hprev_ref[...], g_ref[...], NORM_EPS)
        first = jnp.where(i % seq_tiles == 0, shift_ref[0], up8[SUBLANES - 1:SUBLANES])
        prev = jnp.where(row == 0, first, rolled)
    xx = prev - u
    mu = mu_ref[...]
    xr, xw, xk, xv, xa, xg = (u + xx * mu[j:j + 1] for j in range(6))

    r_ref[...] = _mm(xr, wr_ref[...])
    k_ref[...] = _mm(xk, wk_ref[...])
    v = _mm(xv, wv_ref[...])
    wl = w0_ref[...] + _mm(jnp.tanh(_mm(xw, w1_ref[...])), w2_ref[...])
    z = -wl
    w = -(jnp.maximum(z, 0.0) + jnp.log(1.0 + jnp.exp(-jnp.abs(z)))) - 0.5
    ld_ref[...] = -jnp.exp(w)
    if has_vmix:
        mix = _sigmoid(v0_ref[...] + _mm(_mm(xv, v1_ref[...]), v2_ref[...]))
        v = v + (vfirst_ref[...] - v) * mix
    v_ref[...] = v
    a_ref[...] = _sigmoid(a0_ref[...] + _mm(_mm(xa, a1_ref[...]), a2_ref[...]))
    gg_ref[...] = _mm(_sigmoid(_mm(xg, g1_ref[...])), g2_ref[...])


def _rwkv_proj(h, shift, g, mu, wts, vmix, tm, B, T):
    M, D = h.shape
    short = T < tm
    row = pl.BlockSpec((tm, D), lambda i: (i, 0))
    vec = pl.BlockSpec((1, D), lambda i: (0, 0))
    full = lambda a: pl.BlockSpec(a.shape, lambda i: (0, 0))
    if short:
        seq_tiles = 1
        shift_rows = jnp.zeros((B, T, D), F32).at[:, 0].set(shift).reshape(M, D)
        shift_in, shift_spec = shift_rows, row
        hprev_spec = pl.BlockSpec((SUBLANES, D), lambda i: (0, 0))
    else:
        seq_tiles = T // tm
        shift_in = shift.reshape(B, 1, D)
        shift_spec = pl.BlockSpec((1, 1, D), lambda i: (i // seq_tiles, 0, 0))
        hprev_spec = pl.BlockSpec((SUBLANES, D), lambda i: (jnp.maximum(i * (tm // SUBLANES) - 1, 0), 0))
    ins = [h, h, shift_in, g, mu] + list(wts)
    in_specs = [row, hprev_spec, shift_spec, vec, full(mu)] + [full(a) for a in wts]
    if vmix is not None:
        v0, v1, v2, v_first = vmix
        ins += [v0, v1, v2, v_first]
        in_specs += [vec, full(v1), full(v2), row]
    sds = jax.ShapeDtypeStruct((M, D), F32)
    kern = functools.partial(_rwkv_proj_kernel, has_vmix=vmix is not None, seq_tiles=seq_tiles,
                             short_t=T if short else 0)
    return pl.pallas_call(
        kern,
        grid=(M // tm,),
        in_specs=in_specs,
        out_specs=[row] * 7,
        out_shape=[sds] * 7,
        compiler_params=_cp("parallel"),
        name="rwkv_proj",
    )(*ins)


def _scan_kernel(r_ref, k_ref, v_ref, ld_ref, a_ref, g_ref, kkp_ref, ka_ref, rk_ref, lg_ref, lb_ref, h0_ref,
                 z_ref, hout_ref, bdh_ref, *, nb, nqd):
    c = pl.program_id(2)
    rq = lax.broadcasted_iota(jnp.int32, (QUAD, QUAD), 0)
    cq = lax.broadcasted_iota(jnp.int32, (QUAD, QUAD), 1)
    same_head = (rq // C_HEAD) == (cq // C_HEAD)
    units = [(i, qd) for i in range(nb) for qd in range(nqd)]

    @pl.when(c == 0)
    def _():
        for u, (i, qd) in enumerate(units):
            hc = h0_ref[i, qd]
            bdh_ref[u] = jnp.where(same_head, jnp.concatenate([hc] * 4, axis=1), 0.0)

    gens = [_scan_chunk(u, i, qd, same_head, r_ref, k_ref, v_ref, ld_ref, a_ref, g_ref, kkp_ref, ka_ref, rk_ref,
                        lg_ref, lb_ref, z_ref, bdh_ref) for u, (i, qd) in enumerate(units)]
    hnews = [None] * len(units)
    while any(h is None for h in hnews):
        for u in range(len(units)):
            if hnews[u] is None:
                try:
                    next(gens[u])
                except StopIteration as done:
                    hnews[u] = done.value

    @pl.when(c == pl.num_programs(2) - 1)
    def _():
        for u, (i, qd) in enumerate(units):
            hc = hnews[u][:, 0:C_HEAD]
            for hh in range(1, 4):
                hc = hc + hnews[u][:, hh * C_HEAD:(hh + 1) * C_HEAD]
            hout_ref[i, qd] = hc


def _scan_chunk(u, i, qd, same_head, r_ref, k_ref, v_ref, ld_ref, a_ref, g_ref, kkp_ref, ka_ref, rk_ref, lg_ref,
                lb_ref, z_ref, bdh_ref):
    C = CHUNK
    block_ones = jnp.where(same_head, 1.0, 0.0).astype(BF16)
    lanes = slice(qd * QUAD, (qd + 1) * QUAD)

    def bd(x):
        return jnp.where(same_head, jnp.concatenate([x] * 4, axis=0), 0.0)

    r, k, v, ld, a = (ref[i, :, lanes] for ref in (r_ref, k_ref, v_ref, ld_ref, a_ref))
    kkp, ka, rkp, lg, lb = (ref[:, lanes] for ref in (kkp_ref, ka_ref, rk_ref, lg_ref, lb_ref))

    tt = lax.broadcasted_iota(jnp.int32, (C, C), 0)
    ts = lax.broadcasted_iota(jnp.int32, (C, C), 1)
    tri = jnp.where(ts <= tt, 1.0, 0.0).astype(BF16)
    ones_cl = jnp.ones((C, LANES), BF16)
    parts = _split3(ld)
    lc = sum(jnp.dot(tri, p, preferred_element_type=F32) for p in parts)
    lcol = sum(lax.dot_general(p, ones_cl, (((0,), (0,)), ((), ())), preferred_element_type=F32)
               for p in parts)
    lend = lc[C - 1:C, :]
    gam = jnp.exp(lc)
    gam_prev = jnp.exp(lc - ld)
    gam_inv = jnp.exp(-lc)
    gam_end = jnp.exp(lend - lc)
    gam_rows = jnp.exp(jnp.concatenate([lcol, lcol], axis=1))
    yield

    kk = k * kkp
    ss = _mm(kk * kk, block_ones)
    yield
    kkn = kk / jnp.maximum(jnp.sqrt(ss), 1e-12)
    bv = kkn * a
    k2 = k * (1.0 + (a - 1.0) * ka)
    at = -kkn * gam_prev
    rt = r * gam
    lhs = jnp.concatenate([at, rt], axis=0)
    ab = _mm_nt(lhs, bd(bv * gam_inv))
    ak = _mm_nt(lhs, bd(k2 * gam_inv))
    yield
    t_i =lax.broadcasted_iota(jnp.int32, (C, QUAD), 0)
    s_i = lax.broadcasted_iota(jnp.int32, (C, QUAD), 1) % C_HEAD
    strict = s_i < t_i
    incl = s_i <= t_i
    a_ab = jnp.where(strict, ab[:C], 0.0)
    m_rb = jnp.where(incl, ab[C:], 0.0)
    a_ak = jnp.where(strict, ak[:C], 0.0)
    m_rk = jnp.where(incl, ak[C:], 0.0)

    p = a_ab
    tinv = jnp.where(s_i == t_i, 1.0, 0.0) + p
    n_sq = C.bit_length() - 2
    p = _mm(p, bd(p))
    yield
    for jj in range(n_sq):
        w = bd(p)
        if jj < n_sq - 1:
            pt = _mm(jnp.concatenate([p, tinv], axis=0), w)
            p = pt[:C]
            tinv = tinv + pt[C:]
        else:
            tinv = tinv + _mm(tinv, w)
        yield

    avrv = _mm(jnp.concatenate([a_ak, m_rk], axis=0), bd(v))
    x2, rv = avrv[:C], avrv[C:]
    yield
    w1 = _mm(tinv, bd(at))
    w2 = _mm(tinv, bd(x2))
    yield
    y1 = rt + _mm(m_rb, bd(w1))
    y2 = rv + _mm(m_rb, bd(w2))
    yield
    be = bv * gam_end
    ke = k2 * gam_end
    pm = jnp.where(same_head, _mm_tn(be, w1), 0.0)
    qm = jnp.where(same_head, _mm_tn(be, w2) + _mm_tn(ke, v), 0.0)
    yield

    hmat = bdh_ref[u]
    y = _mm(y1, hmat) + y2
    hnew = gam_rows * hmat + _mm(pm, hmat) + qm
    bdh_ref[u] = hnew
    yield

    inv_n = 1.0 / C_HEAD
    mean = _mm(y, block_ones) * inv_n
    yield
    d = y - mean
    var = _mm(d * d, block_ones) * inv_n
    yield
    yn = d * lax.rsqrt(var + GN_EPS) * lg + lb
    bonus = _mm(r * k2 * rkp, block_ones) * v
    z_ref[i, :, lanes] = (yn + bonus) * g_ref[i, :, lanes]
    return hnew


def _rwkv_scan(r, k, v, ld, a, g, kkp, ka, rk, lg, lb, state, B, T):
    D = r.shape[1]
    nq = D // QUAD
    nc = T // CHUNK
    nb = math.gcd(B, SCAN_SEQS)
    nqd = math.gcd(nq, SCAN_QUADS)
    h0 = jnp.swapaxes(state, -1, -2).reshape(B, nq, QUAD, C_HEAD)
    seq = pl.BlockSpec((nb, CHUNK, nqd * QUAD), lambda b, q, c: (b, c, q))
    vec = pl.BlockSpec((1, nqd * QUAD), lambda b, q, c: (0, q))
    st = pl.BlockSpec((nb, nqd, QUAD, C_HEAD), lambda b, q, c: (b, q, 0, 0))
    z, hout = pl.pallas_call(
        functools.partial(_scan_kernel, nb=nb, nqd=nqd),
        grid=(B // nb, nq // nqd, nc),
        in_specs=[seq] * 6 + [vec] * 5 + [st],
        out_specs=[seq, st],
        out_shape=[jax.ShapeDtypeStruct((B, T, D), F32), jax.ShapeDtypeStruct((B, nq, QUAD, C_HEAD), F32)],
        scratch_shapes=[pltpu.VMEM((nb * nqd, QUAD, QUAD), F32)],
        compiler_params=_cp("parallel", "parallel", "arbitrary"),
        name="rwkv_scan",
    )(*(s.reshape(B, T, D) for s in (r, k, v, ld, a, g)), kkp, ka, rk, lg, lb, h0)
    new_state = jnp.swapaxes(hout.reshape(B, D // C_HEAD, C_HEAD, C_HEAD), -1, -2)
    return z.reshape(B * T, D), new_state


def _row_tile(M, pref):
    return pref if M % pref == 0 else M


def _trunk(x, pos, conv_buf, wkv, shift, p, paged):
    B, T, D = x.shape
    M = B * T
    tm = _row_tile(M, 256)
    h = x.reshape(M, D)
    depth = p['norm_mix'].shape[0]
    tabs = _rope_tables(pos)
    if paged is not None:
        tabs = tuple(jnp.tile(t, (B, 1)) for t in tabs)
    vec = lambda a: a.reshape(1, -1)
    new_k, new_v, new_conv, new_wkv, new_shift = [], [], [], [], []
    v_first = None
    y = None
    for l in range(depth):
        i = l // 2
        if l % 2 == 0:
            q, k, v, glu = _in_proj_ab(h, vec(p['norm_mix'][l]), p['w_in_ab'][i], tabs, tm)
            lam_init = 0.8 - 0.6 * math.exp(-0.3 * l)
            lam_p, sub = p['diff_lambda'][i], vec(p['subln'][i])
            cw, cb = p['conv_w'][i], vec(p['conv_b'][i])
            clg, clb = vec(p['conv_ln_g'][i]), vec(p['conv_ln_b'][i])
            if paged is None:
                tq = _row_tile(T, 512)
                o = _flash_prompt(q, k, v, lam_p, sub, lam_init, B, T, tq, tq, min(tq, 256))
                cm = _conv_prompt(glu, conv_buf[i], cw, cb, clg, clb, B, T, _row_tile(T, 256))
                buf = glu.reshape(B, T, B_WIDTH)[:, -(CONV_WIDTH - 1):]
            else:
                page_table, cache_k, cache_v = paged
                o = _paged_sample(page_table, q, k, v, cache_k, cache_v, i, lam_p, sub, lam_init, B, T)
                cm, buf = _conv_sample(glu, conv_buf[i], cw, cb, clg, clb, B, T)
            h = _out_proj([o, cm], h, p['w_out_ab'][i], tm)
            new_k.append(k.reshape(B, T, A_HEADS, 2 * A_QK_DIM))
            new_v.append(v.reshape(B, T, A_HEADS, A_V_DIM))
            new_conv.append(buf)
        else:
            wts = [p['rwkv_wr'][i], p['rwkv_wk'][i], p['rwkv_wv'][i],
                   vec(p['rwkv_w0'][i]), p['rwkv_w1'][i], p['rwkv_w2'][i],
                   vec(p['rwkv_a0'][i]), p['rwkv_a1'][i], p['rwkv_a2'][i],
                   p['rwkv_g1'][i], p['rwkv_g2'][i]]
            vmix = None
            if v_first is not None:
                vmix = (vec(p['rwkv_v0'][i - 1]), p['rwkv_v1'][i - 1], p['rwkv_v2'][i - 1], v_first)
            u, r, k, v, ld, a, g = _rwkv_proj(h, shift[i], vec(p['norm_mix'][l]), p['rwkv_mu'][i], wts, vmix,
                                              tm, B, T)
            if v_first is None:
                v_first = v
            tp = -(-T // CHUNK) * CHUNK
            seqs = (r, k, v, ld, a, g)
            if tp != T:
                seqs = tuple(jnp.pad(s.reshape(B, T, D), ((0, 0), (0, tp - T), (0, 0))).reshape(B * tp, D)
                             for s in seqs)
            z, s_new = _rwkv_scan(*seqs, vec(p['rwkv_kk'][i]), vec(p['rwkv_ka'][i]), vec(p['rwkv_rk'][i]),
                                  vec(p['rwkv_lnx_g'][i]), vec(p['rwkv_lnx_b'][i]), wkv[i], B, tp)
            if tp != T:
                z = z.reshape(B, tp, D)[:, :T].reshape(M, D)
            h = _out_proj([z], h, p['rwkv_wo'][i], tm)
            new_wkv.append(s_new)
            new_shift.append(u.reshape(B, T, D)[:, -1])
        last = l == depth - 1
        res = _ffn(h, vec(p['norm_ffn'][l]), p['w_gate'][l], p['w_up'][l], p['w_down'][l], tm,
                   vec(p['norm_final']) if last else None)
        if last:
            h, y = res
        else:
            h = res
    return (y.reshape(B, T, D), jnp.stack(new_k), jnp.stack(new_v), jnp.stack(new_conv), jnp.stack(new_wkv),
            jnp.stack(new_shift))


_MATMUL_WEIGHTS = ('w_in_ab', 'w_out_ab', 'rwkv_wr', 'rwkv_wk', 'rwkv_wv', 'rwkv_w1', 'rwkv_w2', 'rwkv_a1',
                   'rwkv_a2', 'rwkv_v1', 'rwkv_v2', 'rwkv_g1', 'rwkv_g2', 'rwkv_wo', 'w_gate', 'w_up', 'w_down')


def kernel(x_prompt, x_sample, cache_k, cache_v, state_conv, state_wkv, state_shift, page_table, norm_mix, norm_ffn, norm_final, w_in_ab, diff_lambda, subln, conv_w, conv_b, conv_ln_g, conv_ln_b, w_out_ab, rwkv_mu, rwkv_wr, rwkv_wk, rwkv_wv, rwkv_w0, rwkv_w1, rwkv_w2, rwkv_a0, rwkv_a1, rwkv_a2, rwkv_v0, rwkv_v1, rwkv_v2, rwkv_g1, rwkv_g2, rwkv_kk, rwkv_ka, rwkv_rk, rwkv_lnx_g, rwkv_lnx_b, rwkv_wo, w_gate, w_up, w_down):
    p = dict(norm_mix=norm_mix, norm_ffn=norm_ffn, norm_final=norm_final, w_in_ab=w_in_ab,
             diff_lambda=diff_lambda, subln=subln, conv_w=conv_w, conv_b=conv_b, conv_ln_g=conv_ln_g,
             conv_ln_b=conv_ln_b, w_out_ab=w_out_ab, rwkv_mu=rwkv_mu, rwkv_wr=rwkv_wr, rwkv_wk=rwkv_wk,
             rwkv_wv=rwkv_wv, rwkv_w0=rwkv_w0, rwkv_w1=rwkv_w1, rwkv_w2=rwkv_w2, rwkv_a0=rwkv_a0,
             rwkv_a1=rwkv_a1, rwkv_a2=rwkv_a2, rwkv_v0=rwkv_v0, rwkv_v1=rwkv_v1, rwkv_v2=rwkv_v2,
             rwkv_g1=rwkv_g1, rwkv_g2=rwkv_g2, rwkv_kk=rwkv_kk, rwkv_ka=rwkv_ka, rwkv_rk=rwkv_rk,
             rwkv_lnx_g=rwkv_lnx_g, rwkv_lnx_b=rwkv_lnx_b, rwkv_wo=rwkv_wo, w_gate=w_gate, w_up=w_up,
             w_down=w_down)
    for name in _MATMUL_WEIGHTS:
        p[name] = p[name].astype(BF16)

    Bp, Tp, D = x_prompt.shape
    n_a, n_c = state_conv.shape[0], state_wkv.shape[0]
    zero_conv = jnp.zeros((n_a, Bp, CONV_WIDTH - 1, B_WIDTH), F32)
    zero_wkv = jnp.zeros((n_c, Bp) + state_wkv.shape[2:], F32)
    zero_shift = jnp.zeros((n_c, Bp, D), F32)
    outs_p = _trunk(x_prompt, jnp.arange(Tp), zero_conv, zero_wkv, zero_shift, p, None)

    past = page_table.shape[1] * PAGE_SIZE
    outs_s = _trunk(x_sample, past + jnp.arange(x_sample.shape[1]), state_conv, state_wkv, state_shift, p,
                    (page_table, cache_k, cache_v))
    return (outs_p[0], outs_s[0]) + tuple(outs_p[1:]) + tuple(outs_s[1:])
```

```python
import functools
import math

import jax
import jax.numpy as jnp
from jax import lax
from jax.experimental import pallas as pl
from jax.experimental.pallas import tpu as pltpu

F32 = jnp.float32
BF16 = jnp.bfloat16

NORM_EPS = 1e-6
SUBLN_EPS = 1e-5
LN_EPS = 1e-5
GN_EPS = 64e-5
ROPE_THETA = 500000.0
ROPE_DIM = 16

A_HEADS = 4
A_QK_DIM = 64
A_V_DIM = 128
A_WIDTH = 512
B_WIDTH = 512
CONV_WIDTH = 31
C_HEAD = 64
PAGE_SIZE = 128

LANES = 128
SUBLANES = 8
QUAD = 4 * C_HEAD
CHUNK = 64
PAGES_PER_STEP = 16
SCAN_SEQS = 2
SCAN_QUADS = 4
CONV_HALO = 32
VMEM_LIMIT = 56 * 1024 * 1024


def _cp(*sem):
    return pltpu.CompilerParams(dimension_semantics=sem, vmem_limit_bytes=VMEM_LIMIT)


def _mm(a, b):
    return jnp.dot(a.astype(BF16), b.astype(BF16), preferred_element_type=F32)


def _mm_nt(a, b):
    return lax.dot_general(a.astype(BF16), b.astype(BF16), (((1,), (1,)), ((), ())), preferred_element_type=F32)


def _mm_tn(a, b):
    return lax.dot_general(a.astype(BF16), b.astype(BF16), (((0,), (0,)), ((), ())), preferred_element_type=F32)


def _split2(x):
    hi = x.astype(BF16)
    lo = (x - hi.astype(F32)).astype(BF16)
    return hi, lo


def _sigmoid(x):
    return 1.0 / (1.0 + jnp.exp(-x))


def _rms(x, g, eps):
    ms = jnp.mean(x * x, axis=-1, keepdims=True)
    return x * lax.rsqrt(ms + eps) * g


def _inproj_kernel(x_ref, g_ref, w_ref, c_ref, s1_ref, s2_ref, q_ref, k_ref, v_ref, glu_ref):
    u = _rms(x_ref[...], g_ref[...], NORM_EPS).astype(BF16)

    def proj(j):
        return jnp.dot(u, w_ref[:, j * A_WIDTH:(j + 1) * A_WIDTH], preferred_element_type=F32)

    ct = jnp.concatenate([c_ref[...]] * A_HEADS, axis=1)
    s1t = jnp.concatenate([s1_ref[...]] * A_HEADS, axis=1)
    s2t = jnp.concatenate([s2_ref[...]] * A_HEADS, axis=1)
    half = ROPE_DIM // 2

    def rope(z):
        return z * ct + pltpu.roll(z, A_WIDTH - half, 1) * s1t + pltpu.roll(z, half, 1) * s2t

    q_ref[...] = rope(proj(0))
    tm = x_ref.shape[0]
    kr = rope(proj(1))
    vv = proj(2)
    for h in range(A_HEADS):
        k_ref[pl.ds(h, tm, stride=A_HEADS), :] = kr[:, h * A_V_DIM:(h + 1) * A_V_DIM]
        v_ref[pl.ds(h, tm, stride=A_HEADS), :] = vv[:, h * A_V_DIM:(h + 1) * A_V_DIM]
    glu_ref[...] = proj(3) * _sigmoid(proj(4))


def _in_proj_ab(h, g, w_bf, tabs, tm):
    M, D = h.shape
    ntab = tabs[0].shape[0] // tm
    tab_spec = pl.BlockSpec((tm, LANES), lambda i: (i % ntab, 0))
    row_out = pl.BlockSpec((tm, A_WIDTH), lambda i: (i, 0))
    kv_out = pl.BlockSpec((tm * A_HEADS, A_V_DIM), lambda i: (i, 0))
    kv_sds = jax.ShapeDtypeStruct((M * A_HEADS, A_V_DIM), F32)
    row_sds = jax.ShapeDtypeStruct((M, A_WIDTH), F32)
    return pl.pallas_call(
        _inproj_kernel,
        grid=(M // tm,),
        in_specs=[pl.BlockSpec((tm, D), lambda i: (i, 0)),
                  pl.BlockSpec((1, D), lambda i: (0, 0)),
                  pl.BlockSpec(w_bf.shape, lambda i: (0, 0)),
                  tab_spec, tab_spec, tab_spec],
        out_specs=[row_out, kv_out, kv_out, row_out],
        out_shape=[row_sds, kv_sds, kv_sds, row_sds],
        compiler_params=_cp("parallel"),
        name="in_proj_ab",
    )(h, g, w_bf, *tabs)


def _rope_tables(pos):
    half = ROPE_DIM // 2
    inv = 1.0 / (ROPE_THETA ** (jnp.arange(0, ROPE_DIM, 2, dtype=F32) / ROPE_DIM))
    ang = pos.astype(F32)[:, None] * inv[None, :]
    cos, sin = jnp.cos(ang), jnp.sin(ang)
    n = pos.shape[0]
    rest = A_QK_DIM - ROPE_DIM
    c = jnp.concatenate([cos, cos, jnp.ones((n, rest), F32)], axis=1)
    s1 = jnp.concatenate([-sin, jnp.zeros((n, A_QK_DIM - half), F32)], axis=1)
    s2 = jnp.concatenate([jnp.zeros((n, half), F32), sin, jnp.zeros((n, rest), F32)], axis=1)
    return tuple(jnp.concatenate([t, t], axis=1) for t in (c, s1, s2))


def _diff_lambda(lam_ref, lam_init):
    lp = lam_ref[...]
    s1 = jnp.sum(lp[0:1] * lp[1:2], axis=-1, keepdims=True)
    s2 = jnp.sum(lp[2:3] * lp[3:4], axis=-1, keepdims=True)
    return jnp.exp(s1) - jnp.exp(s2) + lam_init


def _subln(o, g, lam_init):
    return _rms(o, g, SUBLN_EPS) * (1.0 - lam_init)


def _flash_kernel(qi_ref, ki_ref, lam_ref, sub_ref, q_ref, k_ref, v_ref, o_ref, m_ref, l_ref, acc_ref, *,
                  lam_init, tq, tk, rb):
    pair = pl.program_id(1)
    qi = qi_ref[pair]
    ki = ki_ref[pair]
    nrb = tq // rb
    reps = tk // LANES

    @pl.when(ki == 0)
    def _():
        m_ref[...] = jnp.full(m_ref.shape, -jnp.inf, F32)
        l_ref[...] = jnp.zeros(l_ref.shape, F32)
        acc_ref[...] = jnp.zeros(acc_ref.shape, F32)

    def chain(h, r, masked, kv):
        slab = h * nrb + r
        kb, vb = kv[h]
        q = q_ref[r * rb:(r + 1) * rb, h * A_V_DIM:(h + 1) * A_V_DIM] * (A_QK_DIM ** -0.5 * math.log2(math.e))
        lane = lax.broadcasted_iota(jnp.int32, q.shape, 1)
        q12 = jnp.concatenate([jnp.where(lane < A_QK_DIM, q, 0.0), jnp.where(lane >= A_QK_DIM, q, 0.0)], axis=0)
        s = _mm_nt(q12, kb)
        yield
        if masked:
            row = lax.broadcasted_iota(jnp.int32, s.shape, 0)
            col = lax.broadcasted_iota(jnp.int32, s.shape, 1)
            row = jnp.where(row >= rb, row - rb, row) + (qi * tq + r * rb)
            s = jnp.where(col + ki * tk <= row, s, -jnp.inf)
        m_prev = m_ref[slab]
        m_new = jnp.maximum(m_prev, jnp.max(s, axis=1, keepdims=True))
        alpha = jnp.exp2(m_prev - m_new)
        p = jnp.exp2(s - jnp.tile(m_new, (1, reps)))
        l_ref[slab] = alpha * l_ref[slab] + jnp.sum(p, axis=1, keepdims=True)
        acc_ref[slab] = alpha * acc_ref[slab] + jnp.dot(p.astype(BF16), vb, preferred_element_type=F32)
        m_ref[slab] = m_new

    def step(masked):
        kv = [(k_ref[pl.ds(h, tk, stride=A_HEADS), :].astype(BF16),
               v_ref[pl.ds(h, tk, stride=A_HEADS), :].astype(BF16)) for h in range(A_HEADS)]
        chains = [chain(h, r, masked, kv) for h in range(A_HEADS) for r in range(nrb)]
        next(chains[0])
        for i, ch in enumerate(chains):
            if i + 1 < len(chains):
                next(chains[i + 1])
            for _ in ch:
                pass

    crosses = (ki + 1) * tk - 1 > qi * tq

    @pl.when(crosses)
    def _():
        step(True)

    @pl.when(jnp.logical_not(crosses))
    def _():
        step(False)

    @pl.when(ki == (qi + 1) * (tq // tk) - 1)
    def _():
        lam = _diff_lambda(lam_ref, lam_init)
        for h in range(A_HEADS):
            for r in range(nrb):
                o = acc_ref[h * nrb + r] / l_ref[h * nrb + r]
                o_ref[r * rb:(r + 1) * rb, h * A_V_DIM:(h + 1) * A_V_DIM] = _subln(
                    o[:rb] - lam * o[rb:], sub_ref[...], lam_init)


def _flash_prompt(q, k, v, lam_p, subln, lam_init, B, T, tq, tk, rb):
    nq, nk, ratio = T // tq, T // tk, tq // tk
    pairs = [(i, j) for i in range(nq) for j in range((i + 1) * ratio)]
    qi_tab = jnp.array([pq for pq, _ in pairs], jnp.int32)
    ki_tab = jnp.array([pk for _, pk in pairs], jnp.int32)
    kern = functools.partial(_flash_kernel, lam_init=lam_init, tq=tq, tk=tk, rb=rb)
    kv_spec = pl.BlockSpec((tk * A_HEADS, A_V_DIM), lambda b, s, qt, kt: (b * nk + kt[s], 0))
    q_spec = pl.BlockSpec((tq, A_WIDTH), lambda b, s, qt, kt: (b * nq + qt[s], 0))
    slabs = A_HEADS * (tq // rb)
    return pl.pallas_call(
        kern,
        grid_spec=pltpu.PrefetchScalarGridSpec(
            num_scalar_prefetch=2,
            grid=(B, len(pairs)),
            in_specs=[pl.BlockSpec(lam_p.shape, lambda b, s, qt, kt: (0, 0)),
                      pl.BlockSpec((1, A_V_DIM), lambda b, s, qt, kt: (0, 0)),
                      q_spec, kv_spec, kv_spec],
            out_specs=q_spec,
            scratch_shapes=[pltpu.VMEM((slabs, 2 * rb, LANES), F32), pltpu.VMEM((slabs, 2 * rb, LANES), F32),
                            pltpu.VMEM((slabs, 2 * rb, A_V_DIM), F32)]),
        out_shape=jax.ShapeDtypeStruct((B * T, A_WIDTH), F32),
        compiler_params=_cp("parallel", "arbitrary"),
        name="diff_attn_prompt",
    )(qi_tab, ki_tab, lam_p, subln, q, k, v)


def _paged_kernel(pt_ref, lam_ref, sub_ref, q_ref, kn_ref, vn_ref, *rest, lam_init, pp, tn):
    k_refs = rest[:pp]
    v_refs = rest[pp:2 * pp]
    o_ref, m_ref, l_ref, acc_ref = rest[2 * pp:]
    j = pl.program_id(1)
    hrows = 2 * SUBLANES

    @pl.when(j == 0)
    def _():
        m_ref[...] = jnp.full(m_ref.shape, -jnp.inf, F32)
        l_ref[...] = jnp.zeros(l_ref.shape, F32)
        acc_ref[...] = jnp.zeros(acc_ref.shape, F32)

    qb = (q_ref[0] * (A_QK_DIM ** -0.5)).astype(BF16)

    def update(keys_of, vals_of, mask=None):
        s = jnp.concatenate([_mm_nt(qb[h * hrows:(h + 1) * hrows], keys_of(h)) for h in range(A_HEADS)], axis=0)
        if mask is not None:
            s = jnp.where(mask(s.shape), s, -jnp.inf)
        m_prev = m_ref[...]
        m_new = jnp.maximum(m_prev, jnp.max(s, axis=1, keepdims=True))
        alpha = jnp.exp(m_prev - m_new)
        p = jnp.exp(s - m_new)
        l_ref[...] = alpha * l_ref[...] + jnp.sum(p, axis=1, keepdims=True)
        pv = jnp.concatenate([_mm(p[h * hrows:(h + 1) * hrows], vals_of(h)) for h in range(A_HEADS)], axis=0)
        acc_ref[...] = alpha * acc_ref[...] + pv
        m_ref[...] = m_new

    rows_of = lambda ref, h: ref[0, 0, pl.ds(h, PAGE_SIZE, stride=A_HEADS), :]
    update(lambda h: jnp.concatenate([rows_of(kr, h) for kr in k_refs], axis=0),
           lambda h: jnp.concatenate([rows_of(vr, h) for vr in v_refs], axis=0))

    @pl.when(j == pl.num_programs(1) - 1)
    def _():
        def causal(shape):
            tok = lax.broadcasted_iota(jnp.int32, shape, 0) % SUBLANES
            col = lax.broadcasted_iota(jnp.int32, shape, 1)
            return (col <= tok) & (col < tn)

        head = lambda ref, h: ref[0, pl.ds(h, SUBLANES, stride=A_HEADS), :]
        update(lambda h: head(kn_ref, h), lambda h: head(vn_ref, h), causal)
        o = acc_ref[...] / l_ref[...]
        lam = _diff_lambda(lam_ref, lam_init)
        for h in range(A_HEADS):
            d = o[h * hrows:h * hrows + SUBLANES] - lam * o[h * hrows + SUBLANES:(h + 1) * hrows]
            o_ref[0, h] = _subln(d, sub_ref[...], lam_init)


def _paged_sample(page_table, q, k_new, v_new, cache_k, cache_v, layer, lam_p, subln, lam_init, Bd, tn):
    n_pages = page_table.shape[1]
    pp = math.gcd(n_pages, PAGES_PER_STEP)
    tpad = SUBLANES - tn
    lane_map = jnp.arange(A_V_DIM) // A_QK_DIM
    keep = (lane_map[None, :] == jnp.arange(2)[:, None]).astype(F32)
    q4 = q.reshape(Bd, tn, A_HEADS, A_V_DIM).transpose(0, 2, 1, 3)
    qm = q4[:, :, None, :, :] * keep[None, None, :, None, :]
    qm = jnp.pad(qm, ((0, 0), (0, 0), (0, 0), (0, tpad), (0, 0))).reshape(Bd, A_HEADS * 2 * SUBLANES, A_V_DIM)
    pad = ((0, 0), (0, tpad * A_HEADS), (0, 0))
    kn = jnp.pad(k_new.reshape(Bd, tn * A_HEADS, A_V_DIM), pad)
    vn = jnp.pad(v_new.reshape(Bd, tn * A_HEADS, A_V_DIM), pad)

    page_rows = PAGE_SIZE * A_HEADS
    cache_k = cache_k.reshape(cache_k.shape[:2] + (page_rows, A_V_DIM))
    cache_v = cache_v.reshape(cache_v.shape[:2] + (page_rows, A_V_DIM))

    def page_spec(i):
        return pl.BlockSpec((1, 1, page_rows, A_V_DIM), lambda b, j, pt: (layer, pt[b, j * pp + i], 0, 0))

    per_b = lambda b, j, pt: (b, 0, 0)
    nrow = A_HEADS * 2 * SUBLANES
    kern = functools.partial(_paged_kernel, lam_init=lam_init, pp=pp, tn=tn)
    out = pl.pallas_call(
        kern,
        grid_spec=pltpu.PrefetchScalarGridSpec(
            num_scalar_prefetch=1,
            grid=(Bd, n_pages // pp),
            in_specs=[pl.BlockSpec(lam_p.shape, lambda b, j, pt: (0, 0)),
                      pl.BlockSpec((1, A_V_DIM), lambda b, j, pt: (0, 0)),
                      pl.BlockSpec((1, nrow, A_V_DIM), per_b),
                      pl.BlockSpec((1, SUBLANES * A_HEADS, A_V_DIM), per_b),
                      pl.BlockSpec((1, SUBLANES * A_HEADS, A_V_DIM), per_b)]
                     + [page_spec(i) for i in range(pp)] * 2,
            out_specs=pl.BlockSpec((1, A_HEADS, SUBLANES, A_V_DIM), lambda b, j, pt: (b, 0, 0, 0)),
            scratch_shapes=[pltpu.VMEM((nrow, 1), F32), pltpu.VMEM((nrow, 1), F32),
                            pltpu.VMEM((nrow, A_V_DIM), F32)]),
        out_shape=jax.ShapeDtypeStruct((Bd, A_HEADS, SUBLANES, A_V_DIM), F32),
        compiler_params=_cp("parallel", "arbitrary"),
        name="diff_attn_paged",
    )(page_table, lam_p, subln, qm, kn, vn, *([cache_k] * pp), *([cache_v] * pp))
    return out[:, :, :tn].transpose(0, 2, 1, 3).reshape(Bd * tn, A_WIDTH)


def _conv_post(y, b_ref, lg_ref, lb_ref):
    y = y + b_ref[...]
    mu = jnp.mean(y, axis=-1, keepdims=True)
    d = y - mu
    var = jnp.mean(d * d, axis=-1, keepdims=True)
    z = d * lax.rsqrt(var + LN_EPS) * lg_ref[...] + lb_ref[...]
    return z * _sigmoid(z)


def _conv_kernel(glu_ref, halo_ref, buf_ref, w_ref, b_ref, lg_ref, lb_ref, o_ref, xp_ref, *, tt, sub):
    j = pl.program_id(1)
    xp_ref[0:CONV_HALO, :] = jnp.where(j == 0, buf_ref[0], halo_ref[...])
    xp_ref[CONV_HALO:, :] = glu_ref[...]
    lead = CONV_HALO - (CONV_WIDTH - 1)
    w = w_ref[...]
    for r0 in range(0, tt, sub):
        acc = xp_ref[r0 + lead:r0 + lead + sub, :] * w[0:1]
        for tap in range(1, CONV_WIDTH):
            acc = acc + xp_ref[r0 + lead + tap:r0 + lead + tap + sub, :] * w[tap:tap + 1]
        o_ref[r0:r0 + sub, :] = _conv_post(acc, b_ref, lg_ref, lb_ref)


def _conv_prompt(glu, buf, w, b, lg, lb, B, T, tt):
    nt = T // tt
    hpt = tt // CONV_HALO
    bufp = jnp.pad(buf, ((0, 0), (CONV_HALO - (CONV_WIDTH - 1), 0), (0, 0)))
    vec = pl.BlockSpec((1, B_WIDTH), lambda bb, j: (0, 0))
    kern = functools.partial(_conv_kernel, tt=tt, sub=min(tt, 64))
    return pl.pallas_call(
        kern,
        grid=(B, nt),
        in_specs=[pl.BlockSpec((tt, B_WIDTH), lambda bb, j: (bb * nt + j, 0)),
                  pl.BlockSpec((CONV_HALO, B_WIDTH), lambda bb, j: (jnp.maximum((bb * nt + j) * hpt - 1, 0), 0)),
                  pl.BlockSpec((1, CONV_HALO, B_WIDTH), lambda bb, j: (bb, 0, 0)),
                  pl.BlockSpec((CONV_WIDTH, B_WIDTH), lambda bb, j: (0, 0)),
                  vec, vec, vec],
        out_specs=pl.BlockSpec((tt, B_WIDTH), lambda bb, j: (bb * nt + j, 0)),
        out_shape=jax.ShapeDtypeStruct((B * T, B_WIDTH), F32),
        scratch_shapes=[pltpu.VMEM((CONV_HALO + tt, B_WIDTH), F32)],
        compiler_params=_cp("parallel", "arbitrary"),
        name="conv_prompt",
    )(glu, glu, bufp, w, b, lg, lb)


def _conv_step_kernel(xp_ref, w_ref, b_ref, lg_ref, lb_ref, o_ref, *, tn):
    w = w_ref[...]
    for t in range(tn):
        acc = xp_ref[t] * w[0:1]
        for tap in range(1, CONV_WIDTH):
            acc = acc + xp_ref[t + tap] * w[tap:tap + 1]
        o_ref[t] = _conv_post(acc, b_ref, lg_ref, lb_ref)


def _conv_sample(glu, buf, w, b, lg, lb, Bd, tn):
    xp = jnp.concatenate([buf, glu.reshape(Bd, tn, B_WIDTH)], axis=1)
    out = pl.pallas_call(
        functools.partial(_conv_step_kernel, tn=tn),
        out_shape=jax.ShapeDtypeStruct((tn, Bd, B_WIDTH), F32),
        compiler_params=pltpu.CompilerParams(vmem_limit_bytes=VMEM_LIMIT),
        name="conv_sample",
    )(xp.transpose(1, 0, 2), w, b, lg, lb)
    return out.transpose(1, 0, 2).reshape(Bd * tn, B_WIDTH), xp[:, -(CONV_WIDTH - 1):]


def _outproj_kernel(*refs, n_in):
    res_ref, w_ref, o_ref = refs[n_in], refs[n_in + 1], refs[n_in + 2]
    acc = res_ref[...]
    k0 = 0
    for x_ref in refs[:n_in]:
        kw = x_ref.shape[1]
        acc = acc + jnp.dot(x_ref[...].astype(BF16), w_ref[k0:k0 + kw, :], preferred_element_type=F32)
        k0 += kw
    o_ref[...] = acc


def _out_proj(xs, res, w_bf, tm):
    M, D = res.shape
    return pl.pallas_call(
        functools.partial(_outproj_kernel, n_in=len(xs)),
        grid=(M // tm,),
        in_specs=[pl.BlockSpec((tm, x.shape[1]), lambda i: (i, 0)) for x in xs]
                 + [pl.BlockSpec((tm, D), lambda i: (i, 0)), pl.BlockSpec(w_bf.shape, lambda i: (0, 0))],
        out_specs=pl.BlockSpec((tm, D), lambda i: (i, 0)),
        out_shape=jax.ShapeDtypeStruct((M, D), F32),
        compiler_params=_cp("parallel"),
        name="out_proj",
    )(*xs, res, w_bf)


def _ffn_kernel(h_ref, g_ref, wg_ref, wu_ref, wd_ref, *rest, final):
    h = h_ref[...]
    f = _rms(h, g_ref[...], NORM_EPS).astype(BF16)
    gate = jnp.dot(f, wg_ref[...], preferred_element_type=F32)
    up = jnp.dot(f, wu_ref[...], preferred_element_type=F32)
    act = (gate * _sigmoid(gate) * up).astype(BF16)
    out = h + jnp.dot(act, wd_ref[...], preferred_element_type=F32)
    if final:
        gf_ref, o_ref, y_ref = rest
        y_ref[...] = _rms(out, gf_ref[...], NORM_EPS)
    else:
        (o_ref,) = rest
    o_ref[...] = out


def _ffn(h, g, wg, wu, wd, tm, g_final=None):
    M, D = h.shape
    final = g_final is not None
    row = pl.BlockSpec((tm, D), lambda i: (i, 0))
    vec = pl.BlockSpec((1, D), lambda i: (0, 0))
    full = lambda a: pl.BlockSpec(a.shape, lambda i: (0, 0), pipeline_mode=pl.Buffered(1))
    ins = [h, g, wg, wu, wd] + ([g_final] if final else [])
    in_specs = [row, vec, full(wg), full(wu), full(wd)] + ([vec] if final else [])
    sds = jax.ShapeDtypeStruct((M, D), F32)
    return pl.pallas_call(
        functools.partial(_ffn_kernel, final=final),
        grid=(M // tm,),
        in_specs=in_specs,
        out_specs=[row, row] if final else row,
        out_shape=[sds, sds] if final else sds,
        compiler_params=_cp("parallel"),
        name="ffn",
    )(*ins)


def _rwkv_proj_kernel(*refs, has_vmix, seq_tiles, short_t):
    it = iter(refs)
    h_ref, hprev_ref, shift_ref, g_ref, mu_ref = next(it), next(it), next(it), next(it), next(it)
    wr_ref, wk_ref, wv_ref = next(it), next(it), next(it)
    w0_ref, w1_ref, w2_ref = next(it), next(it), next(it)
    a0_ref, a1_ref, a2_ref = next(it), next(it), next(it)
    g1_ref, g2_ref = next(it), next(it)
    if has_vmix:
        v0_ref, v1_ref, v2_ref, vfirst_ref = next(it), next(it), next(it), next(it)
    u_ref, r_ref, k_ref, v_ref, ld_ref, a_ref, gg_ref = (next(it) for _ in range(7))

    i = pl.program_id(0)
    u = _rms(h_ref[...], g_ref[...], NORM_EPS)
    u_ref[...] = u
    rolled = pltpu.roll(u, 1, 0)
    row = lax.broadcasted_iota(jnp.int32, u.shape, 0)
    if short_t:
        prev = jnp.where(row % short_t == 0, shift_ref[...], rolled)
    else:
        up8 = _rms(hprev_ref[...], g_ref[...], NORM_EPS)
        first = jnp.where(i % seq_tiles == 0, shift_ref[0], up8[SUBLANES - 1:SUBLANES])
        prev = jnp.where(row == 0, first, rolled)
    xx = prev - u
    mu = mu_ref[...]
    xr, xw, xk, xv, xa, xg = (u + xx * mu[j:j + 1] for j in range(6))

    r_ref[...] = _mm(xr, wr_ref[...])
    k_ref[...] = _mm(xk, wk_ref[...])
    v = _mm(xv, wv_ref[...])
    wl = w0_ref[...] + _mm(jnp.tanh(_mm(xw, w1_ref[...])), w2_ref[...])
    z = -wl
    w = -(jnp.maximum(z, 0.0) + jnp.log(1.0 + jnp.exp(-jnp.abs(z)))) - 0.5
    ld_ref[...] = -jnp.exp(w)
    if has_vmix:
        mix = _sigmoid(v0_ref[...] + _mm(_mm(xv, v1_ref[...]), v2_ref[...]))
        v = v + (vfirst_ref[...] - v) * mix
    v_ref[...] = v
    a_ref[...] = _sigmoid(a0_ref[...] + _mm(_mm(xa, a1_ref[...]), a2_ref[...]))
    gg_ref[...] = _mm(_sigmoid(_mm(xg, g1_ref[...])), g2_ref[...])


def _rwkv_proj(h, shift, g, mu, wts, vmix, tm, B, T):
    M, D = h.shape
    short = T < tm
    row = pl.BlockSpec((tm, D), lambda i: (i, 0))
    vec = pl.BlockSpec((1, D), lambda i: (0, 0))
    full = lambda a: pl.BlockSpec(a.shape, lambda i: (0, 0))
    if short:
        seq_tiles = 1
        shift_rows = jnp.zeros((B, T, D), F32).at[:, 0].set(shift).reshape(M, D)
        shift_in, shift_spec = shift_rows, row
        hprev_spec = pl.BlockSpec((SUBLANES, D), lambda i: (0, 0))
    else:
        seq_tiles = T // tm
        shift_in = shift.reshape(B, 1, D)
        shift_spec = pl.BlockSpec((1, 1, D), lambda i: (i // seq_tiles, 0, 0))
        hprev_spec = pl.BlockSpec((SUBLANES, D), lambda i: (jnp.maximum(i * (tm // SUBLANES) - 1, 0), 0))
    ins = [h, h, shift_in, g, mu] + list(wts)
    in_specs = [row, hprev_spec, shift_spec, vec, full(mu)] + [full(a) for a in wts]
    if vmix is not None:
        v0, v1, v2, v_first = vmix
        ins += [v0, v1, v2, v_first]
        in_specs += [vec, full(v1), full(v2), row]
    sds = jax.ShapeDtypeStruct((M, D), F32)
    kern = functools.partial(_rwkv_proj_kernel, has_vmix=vmix is not None, seq_tiles=seq_tiles,
                             short_t=T if short else 0)
    return pl.pallas_call(
        kern,
        grid=(M // tm,),
        in_specs=in_specs,
        out_specs=[row] * 7,
        out_shape=[sds] * 7,
        compiler_params=_cp("parallel"),
        name="rwkv_proj",
    )(*ins)


def _scan_kernel(r_ref, k_ref, v_ref, ld_ref, a_ref, g_ref, kkp_ref, ka_ref, rk_ref, lg_ref, lb_ref, h0_ref,
                 z_ref, hout_ref, bdh_ref, *, nb, nqd):
    c = pl.program_id(2)
    rq = lax.broadcasted_iota(jnp.int32, (QUAD, QUAD), 0)
    cq = lax.broadcasted_iota(jnp.int32, (QUAD, QUAD), 1)
    same_head = (rq // C_HEAD) == (cq // C_HEAD)
    units = [(i, qd) for i in range(nb) for qd in range(nqd)]

    @pl.when(c == 0)
    def _():
        for u, (i, qd) in enumerate(units):
            hc = h0_ref[i, qd]
            bdh_ref[u] = jnp.where(same_head, jnp.concatenate([hc] * 4, axis=1), 0.0)

    gens = [_scan_chunk(u, i, qd, same_head, r_ref, k_ref, v_ref, ld_ref, a_ref, g_ref, kkp_ref, ka_ref, rk_ref,
                        lg_ref, lb_ref, z_ref, bdh_ref) for u, (i, qd) in enumerate(units)]
    hnews = [None] * len(units)
    while any(h is None for h in hnews):
        for u in range(len(units)):
            if hnews[u] is None:
                try:
                    next(gens[u])
                except StopIteration as done:
                    hnews[u] = done.value

    @pl.when(c == pl.num_programs(2) - 1)
    def _():
        for u, (i, qd) in enumerate(units):
            hc = hnews[u][:, 0:C_HEAD]
            for hh in range(1, 4):
                hc = hc + hnews[u][:, hh * C_HEAD:(hh + 1) * C_HEAD]
            hout_ref[i, qd] = hc


def _scan_chunk(u, i, qd, same_head, r_ref, k_ref, v_ref, ld_ref, a_ref, g_ref, kkp_ref, ka_ref, rk_ref, lg_ref,
                lb_ref, z_ref, bdh_ref):
    C = CHUNK
    block_ones = jnp.where(same_head, 1.0, 0.0).astype(BF16)
    lanes = slice(qd * QUAD, (qd + 1) * QUAD)

    def bd(x):
        return jnp.where(same_head, jnp.concatenate([x] * 4, axis=0), 0.0)

    r, k, v, ld, a = (ref[i, :, lanes] for ref in (r_ref, k_ref, v_ref, ld_ref, a_ref))
    kkp, ka, rkp, lg, lb = (ref[:, lanes] for ref in (kkp_ref, ka_ref, rk_ref, lg_ref, lb_ref))

    tt = lax.broadcasted_iota(jnp.int32, (C, C), 0)
    ts = lax.broadcasted_iota(jnp.int32, (C, C), 1)
    tri = jnp.where(ts <= tt, 1.0, 0.0).astype(BF16)
    ones_cl = jnp.ones((C, LANES), BF16)
    parts = _split2(ld)
    lc = sum(jnp.dot(tri, p, preferred_element_type=F32) for p in parts)
    lcol = sum(lax.dot_general(p, ones_cl, (((0,), (0,)), ((), ())), preferred_element_type=F32)
               for p in parts)
    lend = lc[C - 1:C, :]
    gam = jnp.exp(lc)
    gam_prev = jnp.exp(lc - ld)
    gam_inv = jnp.exp(-lc)
    gam_end = jnp.exp(lend - lc)
    gam_rows = jnp.exp(jnp.concatenate([lcol, lcol], axis=1))
    yield

    kk = k * kkp
    ss = _mm(kk * kk, block_ones)
    yield
    kkn = kk / jnp.maximum(jnp.sqrt(ss), 1e-12)
    bv = kkn * a
    k2 = k * (1.0 + (a - 1.0) * ka)
    at = -kkn * gam_prev
    rt = r * gam
    lhs = jnp.concatenate([at, rt], axis=0)
    ab = _mm_nt(lhs, bd(bv * gam_inv))
    ak = _mm_nt(lhs, bd(k2 * gam_inv))
    yield
    t_i =lax.broadcasted_iota(jnp.int32, (C, QUAD), 0)
    s_i = lax.broadcasted_iota(jnp.int32, (C, QUAD), 1) % C_HEAD
    strict = s_i < t_i
    incl = s_i <= t_i
    a_ab = jnp.where(strict, ab[:C], 0.0)
    m_rb = jnp.where(incl, ab[C:], 0.0)
    a_ak = jnp.where(strict, ak[:C], 0.0)
    m_rk = jnp.where(incl, ak[C:], 0.0)

    p = a_ab
    tinv = jnp.where(s_i == t_i, 1.0, 0.0) + p
    n_sq = C.bit_length() - 2
    p = _mm(p, bd(p))
    yield
    for jj in range(n_sq):
        w = bd(p)
        if jj < n_sq - 1:
            pt = _mm(jnp.concatenate([p, tinv], axis=0), w)
            p = pt[:C]
            tinv = tinv + pt[C:]
        else:
            tinv = tinv + _mm(tinv, w)
        yield

    avrv = _mm(jnp.concatenate([a_ak, m_rk], axis=0), bd(v))
    x2, rv = avrv[:C], avrv[C:]
    yield
    w1 = _mm(tinv, bd(at))
    w2 = _mm(tinv, bd(x2))
    yield
    y1 = rt + _mm(m_rb, bd(w1))
    y2 = rv + _mm(m_rb, bd(w2))
    yield
    be = bv * gam_end
    ke = k2 * gam_end
    pm = jnp.where(same_head, _mm_tn(be, w1), 0.0)
    qm = jnp.where(same_head, _mm_tn(be, w2) + _mm_tn(ke, v), 0.0)
    yield

    hmat = bdh_ref[u]
    y = _mm(y1, hmat) + y2
    hnew = gam_rows * hmat + _mm(pm, hmat) + qm
    bdh_ref[u] = hnew
    yield

    inv_n = 1.0 / C_HEAD
    mean = _mm(y, block_ones) * inv_n
    yield
    d = y - mean
    var = _mm(d * d, block_ones) * inv_n
    yield
    yn = d * lax.rsqrt(var + GN_EPS) * lg + lb
    bonus = _mm(r * k2 * rkp, block_ones) * v
    z_ref[i, :, lanes] = (yn + bonus) * g_ref[i, :, lanes]
    return hnew


def _rwkv_scan(r, k, v, ld, a, g, kkp, ka, rk, lg, lb, state, B, T):
    D = r.shape[1]
    nq = D // QUAD
    nc = T // CHUNK
    nb = math.gcd(B, SCAN_SEQS)
    nqd = math.gcd(nq, SCAN_QUADS)
    h0 = jnp.swapaxes(state, -1, -2).reshape(B, nq, QUAD, C_HEAD)
    seq = pl.BlockSpec((nb, CHUNK, nqd * QUAD), lambda b, q, c: (b, c, q))
    vec = pl.BlockSpec((1, nqd * QUAD), lambda b, q, c: (0, q))
    st = pl.BlockSpec((nb, nqd, QUAD, C_HEAD), lambda b, q, c: (b, q, 0, 0))
    z, hout = pl.pallas_call(
        functools.partial(_scan_kernel, nb=nb, nqd=nqd),
        grid=(B // nb, nq // nqd, nc),
        in_specs=[seq] * 6 + [vec] * 5 + [st],
        out_specs=[seq, st],
        out_shape=[jax.ShapeDtypeStruct((B, T, D), F32), jax.ShapeDtypeStruct((B, nq, QUAD, C_HEAD), F32)],
        scratch_shapes=[pltpu.VMEM((nb * nqd, QUAD, QUAD), F32)],
        compiler_params=_cp("parallel", "parallel", "arbitrary"),
        name="rwkv_scan",
    )(*(s.reshape(B, T, D) for s in (r, k, v, ld, a, g)), kkp, ka, rk, lg, lb, h0)
    new_state = jnp.swapaxes(hout.reshape(B, D // C_HEAD, C_HEAD, C_HEAD), -1, -2)
    return z.reshape(B * T, D), new_state


def _row_tile(M, pref):
    return pref if M % pref == 0 else M


def _trunk(x, pos, conv_buf, wkv, shift, p, paged):
    B, T, D = x.shape
    M = B * T
    tm = _row_tile(M, 256)
    h = x.reshape(M, D)
    depth = p['norm_mix'].shape[0]
    tabs = _rope_tables(pos)
    if paged is not None:
        tabs = tuple(jnp.tile(t, (B, 1)) for t in tabs)
    vec = lambda a: a.reshape(1, -1)
    new_k, new_v, new_conv, new_wkv, new_shift = [], [], [], [], []
    v_first = None
    y = None
    for l in range(depth):
        i = l // 2
        if l % 2 == 0:
            q, k, v, glu = _in_proj_ab(h, vec(p['norm_mix'][l]), p['w_in_ab'][i], tabs, tm)
            lam_init = 0.8 - 0.6 * math.exp(-0.3 * l)
            lam_p, sub = p['diff_lambda'][i], vec(p['subln'][i])
            cw, cb = p['conv_w'][i], vec(p['conv_b'][i])
            clg, clb = vec(p['conv_ln_g'][i]), vec(p['conv_ln_b'][i])
            if paged is None:
                tq = _row_tile(T, 512)
                o = _flash_prompt(q, k, v, lam_p, sub, lam_init, B, T, tq, tq, min(tq, 256))
                cm = _conv_prompt(glu, conv_buf[i], cw, cb, clg, clb, B, T, _row_tile(T, 256))
                buf = glu.reshape(B, T, B_WIDTH)[:, -(CONV_WIDTH - 1):]
            else:
                page_table, cache_k, cache_v = paged
                o = _paged_sample(page_table, q, k, v, cache_k, cache_v, i, lam_p, sub, lam_init, B, T)
                cm, buf = _conv_sample(glu, conv_buf[i], cw, cb, clg, clb, B, T)
            h = _out_proj([o, cm], h, p['w_out_ab'][i], tm)
            new_k.append(k.reshape(B, T, A_HEADS, 2 * A_QK_DIM))
            new_v.append(v.reshape(B, T, A_HEADS, A_V_DIM))
            new_conv.append(buf)
        else:
            wts = [p['rwkv_wr'][i], p['rwkv_wk'][i], p['rwkv_wv'][i],
                   vec(p['rwkv_w0'][i]), p['rwkv_w1'][i], p['rwkv_w2'][i],
                   vec(p['rwkv_a0'][i]), p['rwkv_a1'][i], p['rwkv_a2'][i],
                   p['rwkv_g1'][i], p['rwkv_g2'][i]]
            vmix = None
            if v_first is not None:
                vmix = (vec(p['rwkv_v0'][i - 1]), p['rwkv_v1'][i - 1], p['rwkv_v2'][i - 1], v_first)
            u, r, k, v, ld, a, g = _rwkv_proj(h, shift[i], vec(p['norm_mix'][l]), p['rwkv_mu'][i], wts, vmix,
                                              tm, B, T)
            if v_first is None:
                v_first = v
            tp = -(-T // CHUNK) * CHUNK
            seqs = (r, k, v, ld, a, g)
            if tp != T:
                seqs = tuple(jnp.pad(s.reshape(B, T, D), ((0, 0), (0, tp - T), (0, 0))).reshape(B * tp, D)
                             for s in seqs)
            z, s_new = _rwkv_scan(*seqs, vec(p['rwkv_kk'][i]), vec(p['rwkv_ka'][i]), vec(p['rwkv_rk'][i]),
                                  vec(p['rwkv_lnx_g'][i]), vec(p['rwkv_lnx_b'][i]), wkv[i], B, tp)
            if tp != T:
                z = z.reshape(B, tp, D)[:, :T].reshape(M, D)
            h = _out_proj([z], h, p['rwkv_wo'][i], tm)
            new_wkv.append(s_new)
            new_shift.append(u.reshape(B, T, D)[:, -1])
        last = l == depth - 1
        res = _ffn(h, vec(p['norm_ffn'][l]), p['w_gate'][l], p['w_up'][l], p['w_down'][l], tm,
                   vec(p['norm_final']) if last else None)
        if last:
            h, y = res
        else:
            h = res
    return (y.reshape(B, T, D), jnp.stack(new_k), jnp.stack(new_v), jnp.stack(new_conv), jnp.stack(new_wkv),
            jnp.stack(new_shift))


_MATMUL_WEIGHTS = ('w_in_ab', 'w_out_ab', 'rwkv_wr', 'rwkv_wk', 'rwkv_wv', 'rwkv_w1', 'rwkv_w2', 'rwkv_a1',
                   'rwkv_a2', 'rwkv_v1', 'rwkv_v2', 'rwkv_g1', 'rwkv_g2', 'rwkv_wo', 'w_gate', 'w_up', 'w_down')


def kernel(x_prompt, x_sample, cache_k, cache_v, state_conv, state_wkv, state_shift, page_table, norm_mix, norm_ffn, norm_final, w_in_ab, diff_lambda, subln, conv_w, conv_b, conv_ln_g, conv_ln_b, w_out_ab, rwkv_mu, rwkv_wr, rwkv_wk, rwkv_wv, rwkv_w0, rwkv_w1, rwkv_w2, rwkv_a0, rwkv_a1, rwkv_a2, rwkv_v0, rwkv_v1, rwkv_v2, rwkv_g1, rwkv_g2, rwkv_kk, rwkv_ka, rwkv_rk, rwkv_lnx_g, rwkv_lnx_b, rwkv_wo, w_gate, w_up, w_down):
    p = dict(norm_mix=norm_mix, norm_ffn=norm_ffn, norm_final=norm_final, w_in_ab=w_in_ab,
             diff_lambda=diff_lambda, subln=subln, conv_w=conv_w, conv_b=conv_b, conv_ln_g=conv_ln_g,
             conv_ln_b=conv_ln_b, w_out_ab=w_out_ab, rwkv_mu=rwkv_mu, rwkv_wr=rwkv_wr, rwkv_wk=rwkv_wk,
             rwkv_wv=rwkv_wv, rwkv_w0=rwkv_w0, rwkv_w1=rwkv_w1, rwkv_w2=rwkv_w2, rwkv_a0=rwkv_a0,
             rwkv_a1=rwkv_a1, rwkv_a2=rwkv_a2, rwkv_v0=rwkv_v0, rwkv_v1=rwkv_v1, rwkv_v2=rwkv_v2,
             rwkv_g1=rwkv_g1, rwkv_g2=rwkv_g2, rwkv_kk=rwkv_kk, rwkv_ka=rwkv_ka, rwkv_rk=rwkv_rk,
             rwkv_lnx_g=rwkv_lnx_g, rwkv_lnx_b=rwkv_lnx_b, rwkv_wo=rwkv_wo, w_gate=w_gate, w_up=w_up,
             w_down=w_down)
    for name in _MATMUL_WEIGHTS:
        p[name] = p[name].astype(BF16)

    Bp, Tp, D = x_prompt.shape
    n_a, n_c = state_conv.shape[0], state_wkv.shape[0]
    zero_conv = jnp.zeros((n_a, Bp, CONV_WIDTH - 1, B_WIDTH), F32)
    zero_wkv = jnp.zeros((n_c, Bp) + state_wkv.shape[2:], F32)
    zero_shift = jnp.zeros((n_c, Bp, D), F32)
    outs_p = _trunk(x_prompt, jnp.arange(Tp), zero_conv, zero_wkv, zero_shift, p, None)

    past = page_table.shape[1] * PAGE_SIZE
    outs_s = _trunk(x_sample, past + jnp.arange(x_sample.shape[1]), state_conv, state_wkv, state_shift, p,
                    (page_table, cache_k, cache_v))
    return (outs_p[0], outs_s[0]) + tuple(outs_p[1:]) + tuple(outs_s[1:])
```

```python
import functools
import math

import jax
import jax.numpy as jnp
from jax import lax
from jax.experimental import pallas as pl
from jax.experimental.pallas import tpu as pltpu

F32 = jnp.float32
BF16 = jnp.bfloat16

NORM_EPS = 1e-6
SUBLN_EPS = 1e-5
LN_EPS = 1e-5
GN_EPS = 64e-5
ROPE_THETA = 500000.0
ROPE_DIM = 16

A_HEADS = 4
A_QK_DIM = 64
A_V_DIM = 128
A_WIDTH = 512
B_WIDTH = 512
CONV_WIDTH = 31
C_HEAD = 64
PAGE_SIZE = 128

LANES = 128
SUBLANES = 8
QUAD = 4 * C_HEAD
CHUNK = 64
FLASH_TQ = 1024
FLASH_TK = 512
FLASH_ROWS = 256
PAGES_PER_STEP = 16
SCAN_SEQS = 2
SCAN_QUADS = 4
CONV_HALO = 32
VMEM_LIMIT = 56 * 1024 * 1024


def _cp(*sem):
    return pltpu.CompilerParams(dimension_semantics=sem, vmem_limit_bytes=VMEM_LIMIT)


def _mm(a, b):
    return jnp.dot(a.astype(BF16), b.astype(BF16), preferred_element_type=F32)


def _mm_nt(a, b):
    return lax.dot_general(a.astype(BF16), b.astype(BF16), (((1,), (1,)), ((), ())), preferred_element_type=F32)


def _mm_tn(a, b):
    return lax.dot_general(a.astype(BF16), b.astype(BF16), (((0,), (0,)), ((), ())), preferred_element_type=F32)


def _split2(x):
    hi = x.astype(BF16)
    lo = (x - hi.astype(F32)).astype(BF16)
    return hi, lo


def _sigmoid(x):
    return 1.0 / (1.0 + jnp.exp(-x))


def _rms(x, g, eps):
    ms = jnp.mean(x * x, axis=-1, keepdims=True)
    return x * lax.rsqrt(ms + eps) * g


def _inproj_kernel(x_ref, g_ref, w_ref, c_ref, s1_ref, s2_ref, q_ref, k_ref, v_ref, glu_ref):
    u = _rms(x_ref[...], g_ref[...], NORM_EPS).astype(BF16)

    def proj(j):
        return jnp.dot(u, w_ref[:, j * A_WIDTH:(j + 1) * A_WIDTH], preferred_element_type=F32)

    ct = jnp.concatenate([c_ref[...]] * A_HEADS, axis=1)
    s1t = jnp.concatenate([s1_ref[...]] * A_HEADS, axis=1)
    s2t = jnp.concatenate([s2_ref[...]] * A_HEADS, axis=1)
    half = ROPE_DIM // 2

    def rope(z):
        return z * ct + pltpu.roll(z, A_WIDTH - half, 1) * s1t + pltpu.roll(z, half, 1) * s2t

    q_ref[...] = rope(proj(0))
    tm = x_ref.shape[0]
    kr = rope(proj(1))
    vv = proj(2)
    for h in range(A_HEADS):
        k_ref[pl.ds(h, tm, stride=A_HEADS), :] = kr[:, h * A_V_DIM:(h + 1) * A_V_DIM]
        v_ref[pl.ds(h, tm, stride=A_HEADS), :] = vv[:, h * A_V_DIM:(h + 1) * A_V_DIM]
    glu_ref[...] = proj(3) * _sigmoid(proj(4))


def _in_proj_ab(h, g, w_bf, tabs, tm):
    M, D = h.shape
    ntab = tabs[0].shape[0] // tm
    tab_spec = pl.BlockSpec((tm, LANES), lambda i: (i % ntab, 0))
    row_out = pl.BlockSpec((tm, A_WIDTH), lambda i: (i, 0))
    kv_out = pl.BlockSpec((tm * A_HEADS, A_V_DIM), lambda i: (i, 0))
    kv_sds = jax.ShapeDtypeStruct((M * A_HEADS, A_V_DIM), F32)
    row_sds = jax.ShapeDtypeStruct((M, A_WIDTH), F32)
    return pl.pallas_call(
        _inproj_kernel,
        grid=(M // tm,),
        in_specs=[pl.BlockSpec((tm, D), lambda i: (i, 0)),
                  pl.BlockSpec((1, D), lambda i: (0, 0)),
                  pl.BlockSpec(w_bf.shape, lambda i: (0, 0)),
                  tab_spec, tab_spec, tab_spec],
        out_specs=[row_out, kv_out, kv_out, row_out],
        out_shape=[row_sds, kv_sds, kv_sds, row_sds],
        compiler_params=_cp("parallel"),
        name="in_proj_ab",
    )(h, g, w_bf, *tabs)


def _rope_tables(pos):
    half = ROPE_DIM // 2
    inv = 1.0 / (ROPE_THETA ** (jnp.arange(0, ROPE_DIM, 2, dtype=F32) / ROPE_DIM))
    ang = pos.astype(F32)[:, None] * inv[None, :]
    cos, sin = jnp.cos(ang), jnp.sin(ang)
    n = pos.shape[0]
    rest = A_QK_DIM - ROPE_DIM
    c = jnp.concatenate([cos, cos, jnp.ones((n, rest), F32)], axis=1)
    s1 = jnp.concatenate([-sin, jnp.zeros((n, A_QK_DIM - half), F32)], axis=1)
    s2 = jnp.concatenate([jnp.zeros((n, half), F32), sin, jnp.zeros((n, rest), F32)], axis=1)
    return tuple(jnp.concatenate([t, t], axis=1) for t in (c, s1, s2))


def _diff_lambda(lam_ref, lam_init):
    lp = lam_ref[...]
    s1 = jnp.sum(lp[0:1] * lp[1:2], axis=-1, keepdims=True)
    s2 = jnp.sum(lp[2:3] * lp[3:4], axis=-1, keepdims=True)
    return jnp.exp(s1) - jnp.exp(s2) + lam_init


def _subln(o, g, lam_init):
    return _rms(o, g, SUBLN_EPS) * (1.0 - lam_init)


def _flash_kernel(qi_ref, ki_ref, lam_ref, sub_ref, q_ref, k_ref, v_ref, o_ref, m_ref, l_ref, acc_ref, *,
                  lam_init, tq, tk, rb):
    pair = pl.program_id(1)
    qi = qi_ref[pair]
    ki = ki_ref[pair]
    nrb = tq // rb
    reps = tk // LANES

    @pl.when(ki == 0)
    def _():
        m_ref[...] = jnp.full(m_ref.shape, -jnp.inf, F32)
        l_ref[...] = jnp.zeros(l_ref.shape, F32)
        acc_ref[...] = jnp.zeros(acc_ref.shape, F32)

    def chain(h, r, masked, kv):
        slab = h * nrb + r
        kb, vb = kv[h]
        q = q_ref[r * rb:(r + 1) * rb, h * A_V_DIM:(h + 1) * A_V_DIM] * (A_QK_DIM ** -0.5 * math.log2(math.e))
        lane = lax.broadcasted_iota(jnp.int32, q.shape, 1)
        q12 = jnp.concatenate([jnp.where(lane < A_QK_DIM, q, 0.0), jnp.where(lane >= A_QK_DIM, q, 0.0)], axis=0)
        s = _mm_nt(q12, kb)
        yield
        if masked is not None:
            row = lax.broadcasted_iota(jnp.int32, s.shape, 0)
            col = lax.broadcasted_iota(jnp.int32, s.shape, 1)
            row = jnp.where(row >= rb, row - rb, row)
            s = jnp.where(col + masked <= row, s, -jnp.inf)
        m_prev = m_ref[slab]
        m_new = jnp.maximum(m_prev, jnp.max(s, axis=1, keepdims=True))
        alpha = jnp.exp2(m_prev - m_new)
        p = jnp.exp2(s - jnp.tile(m_new, (1, reps)))
        pv = jnp.dot(p.astype(BF16), vb, preferred_element_type=F32)
        l_ref[slab] = alpha * l_ref[slab] + pv[:, A_V_DIM:]
        acc_ref[slab] = alpha * acc_ref[slab] + pv[:, :A_V_DIM]
        m_ref[slab] = m_new

    def step(rel):
        ones = jnp.ones((tk, LANES), BF16)
        kv = [(k_ref[pl.ds(h, tk, stride=A_HEADS), :].astype(BF16),
               jnp.concatenate([v_ref[pl.ds(h, tk, stride=A_HEADS), :].astype(BF16), ones], axis=1))
              for h in range(A_HEADS)]
        chains = []
        for h in range(A_HEADS):
            for r in range(nrb):
                off = None if rel is None else rel * tk - r * rb
                if off is not None and off > rb - 1:
                    continue
                if off is not None and off + tk - 1 <= 0:
                    off = None
                chains.append(chain(h, r, off, kv))
        next(chains[0])
        for i, ch in enumerate(chains):
            if i + 1 < len(chains):
                next(chains[i + 1])
            for _ in ch:
                pass

    ratio = tq // tk
    for rel in range(ratio):
        @pl.when(ki == qi * ratio + rel)
        def _(rel=rel):
            step(rel)

    @pl.when(ki < qi * ratio)
    def _():
        step(None)

    @pl.when(ki == (qi + 1) * ratio - 1)
    def _():
        lam = _diff_lambda(lam_ref, lam_init)
        for h in range(A_HEADS):
            for r in range(nrb):
                o = acc_ref[h * nrb + r] / l_ref[h * nrb + r]
                o_ref[r * rb:(r + 1) * rb, h * A_V_DIM:(h + 1) * A_V_DIM] = _subln(
                    o[:rb] - lam * o[rb:], sub_ref[...], lam_init)


def _flash_prompt(q, k, v, lam_p, subln, lam_init, B, T, tq, tk, rb):
    nq, nk, ratio = T // tq, T // tk, tq // tk
    pairs = [(i, j) for i in range(nq) for j in range((i + 1) * ratio)]
    qi_tab = jnp.array([pq for pq, _ in pairs], jnp.int32)
    ki_tab = jnp.array([pk for _, pk in pairs], jnp.int32)
    kern = functools.partial(_flash_kernel, lam_init=lam_init, tq=tq, tk=tk, rb=rb)
    kv_spec = pl.BlockSpec((tk * A_HEADS, A_V_DIM), lambda b, s, qt, kt: (b * nk + kt[s], 0))
    q_spec = pl.BlockSpec((tq, A_WIDTH), lambda b, s, qt, kt: (b * nq + qt[s], 0))
    slabs = A_HEADS * (tq // rb)
    return pl.pallas_call(
        kern,
        grid_spec=pltpu.PrefetchScalarGridSpec(
            num_scalar_prefetch=2,
            grid=(B, len(pairs)),
            in_specs=[pl.BlockSpec(lam_p.shape, lambda b, s, qt, kt: (0, 0)),
                      pl.BlockSpec((1, A_V_DIM), lambda b, s, qt, kt: (0, 0)),
                      q_spec, kv_spec, kv_spec],
            out_specs=q_spec,
            scratch_shapes=[pltpu.VMEM((slabs, 2 * rb, LANES), F32), pltpu.VMEM((slabs, 2 * rb, LANES), F32),
                            pltpu.VMEM((slabs, 2 * rb, A_V_DIM), F32)]),
        out_shape=jax.ShapeDtypeStruct((B * T, A_WIDTH), F32),
        compiler_params=_cp("parallel", "arbitrary"),
        name="diff_attn_prompt",
    )(qi_tab, ki_tab, lam_p, subln, q, k, v)


def _paged_kernel(pt_ref, lam_ref, sub_ref, q_ref, kn_ref, vn_ref, *rest, lam_init, pp, tn):
    k_refs = rest[:pp]
    v_refs = rest[pp:2 * pp]
    o_ref, m_ref, l_ref, acc_ref = rest[2 * pp:]
    j = pl.program_id(1)
    hrows = 2 * SUBLANES

    @pl.when(j == 0)
    def _():
        m_ref[...] = jnp.full(m_ref.shape, -jnp.inf, F32)
        l_ref[...] = jnp.zeros(l_ref.shape, F32)
        acc_ref[...] = jnp.zeros(acc_ref.shape, F32)

    qb = (q_ref[0] * (A_QK_DIM ** -0.5)).astype(BF16)

    def update(keys_of, vals_of, mask=None):
        s = jnp.concatenate([_mm_nt(qb[h * hrows:(h + 1) * hrows], keys_of(h)) for h in range(A_HEADS)], axis=0)
        if mask is not None:
            s = jnp.where(mask(s.shape), s, -jnp.inf)
        m_prev = m_ref[...]
        m_new = jnp.maximum(m_prev, jnp.max(s, axis=1, keepdims=True))
        alpha = jnp.exp(m_prev - m_new)
        p = jnp.exp(s - m_new)
        l_ref[...] = alpha * l_ref[...] + jnp.sum(p, axis=1, keepdims=True)
        pv = jnp.concatenate([_mm(p[h * hrows:(h + 1) * hrows], vals_of(h)) for h in range(A_HEADS)], axis=0)
        acc_ref[...] = alpha * acc_ref[...] + pv
        m_ref[...] = m_new

    rows_of = lambda ref, h: ref[0, 0, pl.ds(h, PAGE_SIZE, stride=A_HEADS), :]
    update(lambda h: jnp.concatenate([rows_of(kr, h) for kr in k_refs], axis=0),
           lambda h: jnp.concatenate([rows_of(vr, h) for vr in v_refs], axis=0))

    @pl.when(j == pl.num_programs(1) - 1)
    def _():
        def causal(shape):
            tok = lax.broadcasted_iota(jnp.int32, shape, 0) % SUBLANES
            col = lax.broadcasted_iota(jnp.int32, shape, 1)
            return (col <= tok) & (col < tn)

        head = lambda ref, h: ref[0, pl.ds(h, SUBLANES, stride=A_HEADS), :]
        update(lambda h: head(kn_ref, h), lambda h: head(vn_ref, h), causal)
        o = acc_ref[...] / l_ref[...]
        lam = _diff_lambda(lam_ref, lam_init)
        for h in range(A_HEADS):
            d = o[h * hrows:h * hrows + SUBLANES] - lam * o[h * hrows + SUBLANES:(h + 1) * hrows]
            o_ref[0, h] = _subln(d, sub_ref[...], lam_init)


def _paged_sample(page_table, q, k_new, v_new, cache_k, cache_v, layer, lam_p, subln, lam_init, Bd, tn):
    n_pages = page_table.shape[1]
    pp = math.gcd(n_pages, PAGES_PER_STEP)
    tpad = SUBLANES - tn
    lane_map = jnp.arange(A_V_DIM) // A_QK_DIM
    keep = (lane_map[None, :] == jnp.arange(2)[:, None]).astype(F32)
    q4 = q.reshape(Bd, tn, A_HEADS, A_V_DIM).transpose(0, 2, 1, 3)
    qm = q4[:, :, None, :, :] * keep[None, None, :, None, :]
    qm = jnp.pad(qm, ((0, 0), (0, 0), (0, 0), (0, tpad), (0, 0))).reshape(Bd, A_HEADS * 2 * SUBLANES, A_V_DIM)
    pad = ((0, 0), (0, tpad * A_HEADS), (0, 0))
    kn = jnp.pad(k_new.reshape(Bd, tn * A_HEADS, A_V_DIM), pad)
    vn = jnp.pad(v_new.reshape(Bd, tn * A_HEADS, A_V_DIM), pad)

    page_rows = PAGE_SIZE * A_HEADS
    cache_k = cache_k.reshape(cache_k.shape[:2] + (page_rows, A_V_DIM))
    cache_v = cache_v.reshape(cache_v.shape[:2] + (page_rows, A_V_DIM))

    def page_spec(i):
        return pl.BlockSpec((1, 1, page_rows, A_V_DIM), lambda b, j, pt: (layer, pt[b, j * pp + i], 0, 0))

    per_b = lambda b, j, pt: (b, 0, 0)
    nrow = A_HEADS * 2 * SUBLANES
    kern = functools.partial(_paged_kernel, lam_init=lam_init, pp=pp, tn=tn)
    out = pl.pallas_call(
        kern,
        grid_spec=pltpu.PrefetchScalarGridSpec(
            num_scalar_prefetch=1,
            grid=(Bd, n_pages // pp),
            in_specs=[pl.BlockSpec(lam_p.shape, lambda b, j, pt: (0, 0)),
                      pl.BlockSpec((1, A_V_DIM), lambda b, j, pt: (0, 0)),
                      pl.BlockSpec((1, nrow, A_V_DIM), per_b),
                      pl.BlockSpec((1, SUBLANES * A_HEADS, A_V_DIM), per_b),
                      pl.BlockSpec((1, SUBLANES * A_HEADS, A_V_DIM), per_b)]
                     + [page_spec(i) for i in range(pp)] * 2,
            out_specs=pl.BlockSpec((1, A_HEADS, SUBLANES, A_V_DIM), lambda b, j, pt: (b, 0, 0, 0)),
            scratch_shapes=[pltpu.VMEM((nrow, 1), F32), pltpu.VMEM((nrow, 1), F32),
                            pltpu.VMEM((nrow, A_V_DIM), F32)]),
        out_shape=jax.ShapeDtypeStruct((Bd, A_HEADS, SUBLANES, A_V_DIM), F32),
        compiler_params=_cp("parallel", "arbitrary"),
        name="diff_attn_paged",
    )(page_table, lam_p, subln, qm, kn, vn, *([cache_k] * pp), *([cache_v] * pp))
    return out[:, :, :tn].transpose(0, 2, 1, 3).reshape(Bd * tn, A_WIDTH)


def _conv_post(y, b_ref, lg_ref, lb_ref):
    y = y + b_ref[...]
    mu = jnp.mean(y, axis=-1, keepdims=True)
    d = y - mu
    var = jnp.mean(d * d, axis=-1, keepdims=True)
    z = d * lax.rsqrt(var + LN_EPS) * lg_ref[...] + lb_ref[...]
    return z * _sigmoid(z)


def _conv_kernel(glu_ref, halo_ref, buf_ref, w_ref, b_ref, lg_ref, lb_ref, o_ref, xp_ref, xs_ref, *, tt, sub):
    j = pl.program_id(1)
    rows = CONV_HALO + tt
    xp_ref[0:CONV_HALO, :] = jnp.where(j == 0, buf_ref[0], halo_ref[...])
    xp_ref[CONV_HALO:, :] = glu_ref[...]
    for ph in range(SUBLANES):
        xs_ref[ph, 0:rows - ph, :] = xp_ref[ph:rows, :]
    lead = CONV_HALO - (CONV_WIDTH - 1)
    w = w_ref[...]
    for r0 in range(0, tt, sub):
        acc = None
        for tap in range(CONV_WIDTH):
            ph, base = (lead + tap) % SUBLANES, (lead + tap) // SUBLANES * SUBLANES
            term = xs_ref[ph, r0 + base:r0 + base + sub, :] * w[tap:tap + 1]
            acc = term if acc is None else acc + term
        o_ref[r0:r0 + sub, :] = _conv_post(acc, b_ref, lg_ref, lb_ref)


def _conv_prompt(glu, buf, w, b, lg, lb, B, T, tt):
    nt = T // tt
    hpt = tt // CONV_HALO
    bufp = jnp.pad(buf, ((0, 0), (CONV_HALO - (CONV_WIDTH - 1), 0), (0, 0)))
    vec = pl.BlockSpec((1, B_WIDTH), lambda bb, j: (0, 0))
    kern = functools.partial(_conv_kernel, tt=tt, sub=min(tt, 64))
    return pl.pallas_call(
        kern,
        grid=(B, nt),
        in_specs=[pl.BlockSpec((tt, B_WIDTH), lambda bb, j: (bb * nt + j, 0)),
                  pl.BlockSpec((CONV_HALO, B_WIDTH), lambda bb, j: (jnp.maximum((bb * nt + j) * hpt - 1, 0), 0)),
                  pl.BlockSpec((1, CONV_HALO, B_WIDTH), lambda bb, j: (bb, 0, 0)),
                  pl.BlockSpec((CONV_WIDTH, B_WIDTH), lambda bb, j: (0, 0)),
                  vec, vec, vec],
        out_specs=pl.BlockSpec((tt, B_WIDTH), lambda bb, j: (bb * nt + j, 0)),
        out_shape=jax.ShapeDtypeStruct((B * T, B_WIDTH), F32),
        scratch_shapes=[pltpu.VMEM((CONV_HALO + tt, B_WIDTH), F32),
                        pltpu.VMEM((SUBLANES, CONV_HALO + tt, B_WIDTH), F32)],
        compiler_params=_cp("parallel", "arbitrary"),
        name="conv_prompt",
    )(glu, glu, bufp, w, b, lg, lb)


def _conv_step_kernel(xp_ref, w_ref, b_ref, lg_ref, lb_ref, o_ref, *, tn):
    w = w_ref[...]
    for t in range(tn):
        acc = xp_ref[t] * w[0:1]
        for tap in range(1, CONV_WIDTH):
            acc = acc + xp_ref[t + tap] * w[tap:tap + 1]
        o_ref[t] = _conv_post(acc, b_ref, lg_ref, lb_ref)


def _conv_sample(glu, buf, w, b, lg, lb, Bd, tn):
    xp = jnp.concatenate([buf, glu.reshape(Bd, tn, B_WIDTH)], axis=1)
    out = pl.pallas_call(
        functools.partial(_conv_step_kernel, tn=tn),
        out_shape=jax.ShapeDtypeStruct((tn, Bd, B_WIDTH), F32),
        compiler_params=pltpu.CompilerParams(vmem_limit_bytes=VMEM_LIMIT),
        name="conv_sample",
    )(xp.transpose(1, 0, 2), w, b, lg, lb)
    return out.transpose(1, 0, 2).reshape(Bd * tn, B_WIDTH), xp[:, -(CONV_WIDTH - 1):]


def _mix_ffn_kernel(*refs, n_in, final):
    x_refs = refs[:n_in]
    h_ref, wo_ref, g_ref, wg_ref, wu_ref, wd_ref = refs[n_in:n_in + 6]
    rest = refs[n_in + 6:]
    h = h_ref[...]
    k0 = 0
    for x_ref in x_refs:
        kw = x_ref.shape[1]
        h = h + jnp.dot(x_ref[...].astype(BF16), wo_ref[k0:k0 + kw, :], preferred_element_type=F32)
        k0 += kw
    f = _rms(h, g_ref[...], NORM_EPS).astype(BF16)
    gate = jnp.dot(f, wg_ref[...], preferred_element_type=F32)
    up = jnp.dot(f, wu_ref[...], preferred_element_type=F32)
    act = (gate * _sigmoid(gate) * up).astype(BF16)
    out = h + jnp.dot(act, wd_ref[...], preferred_element_type=F32)
    if final:
        gf_ref, o_ref, y_ref = rest
        y_ref[...] = _rms(out, gf_ref[...], NORM_EPS)
    else:
        (o_ref,) = rest
    o_ref[...] = out


def _mix_ffn(xs, h, wo, g, wg, wu, wd, tm, g_final=None):
    M, D = h.shape
    final = g_final is not None
    row = pl.BlockSpec((tm, D), lambda i: (i, 0))
    vec = pl.BlockSpec((1, D), lambda i: (0, 0))
    full = lambda a: pl.BlockSpec(a.shape, lambda i: (0, 0), pipeline_mode=pl.Buffered(1))
    ins = list(xs) + [h, wo, g, wg, wu, wd] + ([g_final] if final else [])
    in_specs = ([pl.BlockSpec((tm, x.shape[1]), lambda i: (i, 0)) for x in xs]
                + [row, full(wo), vec, full(wg), full(wu), full(wd)] + ([vec] if final else []))
    sds = jax.ShapeDtypeStruct((M, D), F32)
    return pl.pallas_call(
        functools.partial(_mix_ffn_kernel, n_in=len(xs), final=final),
        grid=(M // tm,),
        in_specs=in_specs,
        out_specs=[row, row] if final else row,
        out_shape=[sds, sds] if final else sds,
        compiler_params=_cp("parallel"),
        name="mix_ffn",
    )(*ins)


def _rwkv_proj_kernel(*refs, has_vmix, seq_tiles, short_t):
    it = iter(refs)
    h_ref, hprev_ref, shift_ref, g_ref, mu_ref = next(it), next(it), next(it), next(it), next(it)
    wr_ref, wk_ref, wv_ref = next(it), next(it), next(it)
    w0_ref, w1_ref, w2_ref = next(it), next(it), next(it)
    a0_ref, a1_ref, a2_ref = next(it), next(it), next(it)
    g1_ref, g2_ref = next(it), next(it)
    if has_vmix:
        v0_ref, v1_ref, v2_ref, vfirst_ref = next(it), next(it), next(it), next(it)
    u_ref, r_ref, k_ref, v_ref, ld_ref, a_ref, gg_ref = (next(it) for _ in range(7))

    i = pl.program_id(0)
    u = _rms(h_ref[...], g_ref[...], NORM_EPS)
    keep = u_ref.shape[1]
    u_ref[0] = u[u.shape[0] - keep:]
    rolled = pltpu.roll(u, 1, 0)
    row = lax.broadcasted_iota(jnp.int32, u.shape, 0)
    if short_t:
        prev = jnp.where(row % short_t == 0, shift_ref[...], rolled)
    else:
        up8 = _rms(hprev_ref[...], g_ref[...], NORM_EPS)
        first = jnp.where(i % seq_tiles == 0, shift_ref[0], up8[SUBLANES - 1:SUBLANES])
        prev = jnp.where(row == 0, first, rolled)
    xx = prev - u
    mu = mu_ref[...]
    xr, xw, xk, xv, xa, xg = (u + xx * mu[j:j + 1] for j in range(6))

    r_ref[...] = _mm(xr, wr_ref[...])
    k_ref[...] = _mm(xk, wk_ref[...])
    v = _mm(xv, wv_ref[...])
    wl = w0_ref[...] + _mm(jnp.tanh(_mm(xw, w1_ref[...])), w2_ref[...])
    z = -wl
    w = -(jnp.maximum(z, 0.0) + jnp.log(1.0 + jnp.exp(-jnp.abs(z)))) - 0.5
    ld_ref[...] = -jnp.exp(w)
    if has_vmix:
        mix = _sigmoid(v0_ref[...] + _mm(_mm(xv, v1_ref[...]), v2_ref[...]))
        v = v + (vfirst_ref[...] - v) * mix
    v_ref[...] = v
    a_ref[...] = _sigmoid(a0_ref[...] + _mm(_mm(xa, a1_ref[...]), a2_ref[...]))
    gg_ref[...] = _mm(_sigmoid(_mm(xg, g1_ref[...])), g2_ref[...])


def _rwkv_proj(h, shift, g, mu, wts, vmix, tm, B, T):
    M, D = h.shape
    short = T < tm
    row = pl.BlockSpec((tm, D), lambda i: (i, 0))
    vec = pl.BlockSpec((1, D), lambda i: (0, 0))
    full = lambda a: pl.BlockSpec(a.shape, lambda i: (0, 0))
    if short:
        seq_tiles = 1
        shift_rows = jnp.zeros((B, T, D), F32).at[:, 0].set(shift).reshape(M, D)
        shift_in, shift_spec = shift_rows, row
        hprev_spec = pl.BlockSpec((SUBLANES, D), lambda i: (0, 0))
    else:
        seq_tiles = T // tm
        shift_in = shift.reshape(B, 1, D)
        shift_spec = pl.BlockSpec((1, 1, D), lambda i: (i // seq_tiles, 0, 0))
        hprev_spec = pl.BlockSpec((SUBLANES, D), lambda i: (jnp.maximum(i * (tm // SUBLANES) - 1, 0), 0))
    ins = [h, h, shift_in, g, mu] + list(wts)
    in_specs = [row, hprev_spec, shift_spec, vec, full(mu)] + [full(a) for a in wts]
    if vmix is not None:
        v0, v1, v2, v_first = vmix
        ins += [v0, v1, v2, v_first]
        in_specs += [vec, full(v1), full(v2), row]
    sds = jax.ShapeDtypeStruct((M, D), F32)
    keep = tm if short else SUBLANES
    kern = functools.partial(_rwkv_proj_kernel, has_vmix=vmix is not None, seq_tiles=seq_tiles,
                             short_t=T if short else 0)
    tail, *seqs = pl.pallas_call(
        kern,
        grid=(M // tm,),
        in_specs=in_specs,
        out_specs=[pl.BlockSpec((1, keep, D), lambda i: (i, 0, 0))] + [row] * 6,
        out_shape=[jax.ShapeDtypeStruct((M // tm, keep, D), F32)] + [sds] * 6,
        compiler_params=_cp("parallel"),
        name="rwkv_proj",
    )(*ins)
    if short:
        last_u = tail.reshape(B, T, D)[:, -1]
    else:
        last_u = tail.reshape(B, seq_tiles, keep, D)[:, -1, -1]
    return last_u, seqs


def _scan_kernel(r_ref, k_ref, v_ref, ld_ref, a_ref, g_ref, kkp_ref, ka_ref, rk_ref, lg_ref, lb_ref, h0_ref,
                 z_ref, hout_ref, bdh_ref, pad_ref, *, nb, nqd):
    c = pl.program_id(2)
    rq = lax.broadcasted_iota(jnp.int32, (QUAD, QUAD), 0)
    cq = lax.broadcasted_iota(jnp.int32, (QUAD, QUAD), 1)
    same_head = (rq // C_HEAD) == (cq // C_HEAD)
    units = [(i, qd) for i in range(nb) for qd in range(nqd)]

    @pl.when(c == 0)
    def _():
        for u, (i, qd) in enumerate(units):
            hc = h0_ref[i, qd]
            bdh_ref[u] = jnp.where(same_head, jnp.concatenate([hc] * 4, axis=1), 0.0)

    gens = [_scan_chunk(u, i, qd, same_head, r_ref, k_ref, v_ref, ld_ref, a_ref, g_ref, kkp_ref, ka_ref, rk_ref,
                        lg_ref, lb_ref, z_ref, bdh_ref, pad_ref) for u, (i, qd) in enumerate(units)]
    hnews = [None] * len(units)
    while any(h is None for h in hnews):
        for u in range(len(units)):
            if hnews[u] is None:
                try:
                    next(gens[u])
                except StopIteration as done:
                    hnews[u] = done.value

    @pl.when(c == pl.num_programs(2) - 1)
    def _():
        for u, (i, qd) in enumerate(units):
            hc = hnews[u][:, 0:C_HEAD]
            for hh in range(1, 4):
                hc = hc + hnews[u][:, hh * C_HEAD:(hh + 1) * C_HEAD]
            hout_ref[i, qd] = hc


def _scan_chunk(u, i, qd, same_head, r_ref, k_ref, v_ref, ld_ref, a_ref, g_ref, kkp_ref, ka_ref, rk_ref, lg_ref,
                lb_ref, z_ref, bdh_ref, pad_ref):
    C = CHUNK
    block_ones = jnp.where(same_head, 1.0, 0.0).astype(BF16)
    lanes = slice(qd * QUAD, (qd + 1) * QUAD)
    t_valid = r_ref.shape[1]

    def bd(x):
        return jnp.where(same_head, jnp.concatenate([x] * 4, axis=0), 0.0)

    def load(j, ref):
        if t_valid == C:
            return ref[i, :, lanes]
        pad_ref[j, u] = jnp.zeros((C, QUAD), F32)
        pad_ref[j, u, 0:t_valid, :] = ref[i, :, lanes]
        return pad_ref[j, u]

    r, k, v, ld, a = (load(j, ref) for j, ref in enumerate((r_ref, k_ref, v_ref, ld_ref, a_ref)))
    kkp, ka, rkp, lg, lb = (ref[:, lanes] for ref in (kkp_ref, ka_ref, rk_ref, lg_ref, lb_ref))

    tt = lax.broadcasted_iota(jnp.int32, (C, C), 0)
    ts = lax.broadcasted_iota(jnp.int32, (C, C), 1)
    tri = jnp.where(ts <= tt, 1.0, 0.0).astype(BF16)
    ones_cl = jnp.ones((C, LANES), BF16)
    parts = _split2(ld)
    lc = sum(jnp.dot(tri, p, preferred_element_type=F32) for p in parts)
    lcol = sum(lax.dot_general(p, ones_cl, (((0,), (0,)), ((), ())), preferred_element_type=F32)
               for p in parts)
    lend = lc[C - 1:C, :]
    gam = jnp.exp(lc)
    gam_prev = jnp.exp(lc - ld)
    gam_inv = jnp.exp(-lc)
    gam_end = jnp.exp(lend - lc)
    gam_rows = jnp.exp(jnp.concatenate([lcol, lcol], axis=1))
    yield

    kk = k * kkp
    ss = _mm(kk * kk, block_ones)
    yield
    kkn = kk / jnp.maximum(jnp.sqrt(ss), 1e-12)
    bv = kkn * a
    k2 = k * (1.0 + (a - 1.0) * ka)
    at = -kkn * gam_prev
    rt = r * gam
    lhs = jnp.concatenate([at, rt], axis=0)
    ab = _mm_nt(lhs, bd(bv * gam_inv))
    ak = _mm_nt(lhs, bd(k2 * gam_inv))
    yield
    t_i =lax.broadcasted_iota(jnp.int32, (C, QUAD), 0)
    s_i = lax.broadcasted_iota(jnp.int32, (C, QUAD), 1) % C_HEAD
    strict = s_i < t_i
    incl = s_i <= t_i
    a_ab = jnp.where(strict, ab[:C], 0.0)
    m_rb = jnp.where(incl, ab[C:], 0.0)
    a_ak = jnp.where(strict, ak[:C], 0.0)
    m_rk = jnp.where(incl, ak[C:], 0.0)

    p = a_ab
    tinv = jnp.where(s_i == t_i, 1.0, 0.0) + p
    n_sq = C.bit_length() - 2
    p = _mm(p, bd(p))
    yield
    for jj in range(n_sq):
        w = bd(p)
        if jj < n_sq - 1:
            pt = _mm(jnp.concatenate([p, tinv], axis=0), w)
            p = pt[:C]
            tinv = tinv + pt[C:]
        else:
            tinv = tinv + _mm(tinv, w)
        yield

    avrv = _mm(jnp.concatenate([a_ak, m_rk], axis=0), bd(v))
    x2, rv = avrv[:C], avrv[C:]
    yield
    w1 = _mm(tinv, bd(at))
    w2 = _mm(tinv, bd(x2))
    yield
    y1 = rt + _mm(m_rb, bd(w1))
    y2 = rv + _mm(m_rb, bd(w2))
    yield
    be = bv * gam_end
    ke = k2 * gam_end
    pm = jnp.where(same_head, _mm_tn(be, w1), 0.0)
    qm = jnp.where(same_head, _mm_tn(be, w2) + _mm_tn(ke, v), 0.0)
    yield

    hmat = bdh_ref[u]
    y = _mm(y1, hmat) + y2
    hnew = gam_rows * hmat + _mm(pm, hmat) + qm
    bdh_ref[u] = hnew
    yield

    inv_n = 1.0 / C_HEAD
    mean = _mm(y, block_ones) * inv_n
    yield
    d = y - mean
    var = _mm(d * d, block_ones) * inv_n
    yield
    yn = d * lax.rsqrt(var + GN_EPS) * lg + lb
    bonus = _mm(r * k2 * rkp, block_ones) * v
    z_ref[i, :, lanes] = ((yn + bonus)[0:t_valid]) * g_ref[i, :, lanes]
    return hnew


def _rwkv_scan(r, k, v, ld, a, g, kkp, ka, rk, lg, lb, state, B, T):
    D = r.shape[1]
    nq = D // QUAD
    tc = min(T, CHUNK)
    nc = T // tc
    nb = math.gcd(B, SCAN_SEQS)
    nqd = math.gcd(nq, SCAN_QUADS)
    h0 = jnp.swapaxes(state, -1, -2).reshape(B, nq, QUAD, C_HEAD)
    seq = pl.BlockSpec((nb, tc, nqd * QUAD), lambda b, q, c: (b, c, q))
    vec = pl.BlockSpec((1, nqd * QUAD), lambda b, q, c: (0, q))
    st = pl.BlockSpec((nb, nqd, QUAD, C_HEAD), lambda b, q, c: (b, q, 0, 0))
    z, hout = pl.pallas_call(
        functools.partial(_scan_kernel, nb=nb, nqd=nqd),
        grid=(B // nb, nq // nqd, nc),
        in_specs=[seq] * 6 + [vec] * 5 + [st],
        out_specs=[seq, st],
        out_shape=[jax.ShapeDtypeStruct((B, T, D), F32), jax.ShapeDtypeStruct((B, nq, QUAD, C_HEAD), F32)],
        scratch_shapes=[pltpu.VMEM((nb * nqd, QUAD, QUAD), F32),
                        pltpu.VMEM((5, nb * nqd, CHUNK, QUAD) if tc < CHUNK else (1, 1, SUBLANES, LANES), F32)],
        compiler_params=_cp("parallel", "parallel", "arbitrary"),
        name="rwkv_scan",
    )(*(s.reshape(B, T, D) for s in (r, k, v, ld, a, g)), kkp, ka, rk, lg, lb, h0)
    new_state = jnp.swapaxes(hout.reshape(B, D // C_HEAD, C_HEAD, C_HEAD), -1, -2)
    return z.reshape(B * T, D), new_state


def _row_tile(M, pref):
    return pref if M % pref == 0 else M


def _trunk(x, pos, conv_buf, wkv, shift, p, paged):
    B, T, D = x.shape
    M = B * T
    tm = _row_tile(M, 256)
    h = x.reshape(M, D)
    depth = p['norm_mix'].shape[0]
    tabs = _rope_tables(pos)
    if paged is not None:
        tabs = tuple(jnp.tile(t, (B, 1)) for t in tabs)
    vec = lambda a: a.reshape(1, -1)
    new_k, new_v, new_conv, new_wkv, new_shift = [], [], [], [], []
    v_first = None
    y = None
    for l in range(depth):
        i = l // 2
        if l % 2 == 0:
            q, k, v, glu = _in_proj_ab(h, vec(p['norm_mix'][l]), p['w_in_ab'][i], tabs, tm)
            lam_init = 0.8 - 0.6 * math.exp(-0.3 * l)
            lam_p, sub = p['diff_lambda'][i], vec(p['subln'][i])
            cw, cb = p['conv_w'][i], vec(p['conv_b'][i])
            clg, clb = vec(p['conv_ln_g'][i]), vec(p['conv_ln_b'][i])
            if paged is None:
                tk = _row_tile(T, FLASH_TK)
                tq = _row_tile(T, FLASH_TQ)
                o = _flash_prompt(q, k, v, lam_p, sub, lam_init, B, T, tq, tk, min(tq, FLASH_ROWS))
                cm = _conv_prompt(glu, conv_buf[i], cw, cb, clg, clb, B, T, _row_tile(T, 256))
                buf = glu.reshape(B, T, B_WIDTH)[:, -(CONV_WIDTH - 1):]
            else:
                page_table, cache_k, cache_v = paged
                o = _paged_sample(page_table, q, k, v, cache_k, cache_v, i, lam_p, sub, lam_init, B, T)
                cm, buf = _conv_sample(glu, conv_buf[i], cw, cb, clg, clb, B, T)
            mix_in, w_mix = [o, cm], p['w_out_ab'][i]
            new_k.append(k.reshape(B, T, A_HEADS, 2 * A_QK_DIM))
            new_v.append(v.reshape(B, T, A_HEADS, A_V_DIM))
            new_conv.append(buf)
        else:
            wts = [p['rwkv_wr'][i], p['rwkv_wk'][i], p['rwkv_wv'][i],
                   vec(p['rwkv_w0'][i]), p['rwkv_w1'][i], p['rwkv_w2'][i],
                   vec(p['rwkv_a0'][i]), p['rwkv_a1'][i], p['rwkv_a2'][i],
                   p['rwkv_g1'][i], p['rwkv_g2'][i]]
            vmix = None
            if v_first is not None:
                vmix = (vec(p['rwkv_v0'][i - 1]), p['rwkv_v1'][i - 1], p['rwkv_v2'][i - 1], v_first)
            last_u, seqs = _rwkv_proj(h, shift[i], vec(p['norm_mix'][l]), p['rwkv_mu'][i], wts, vmix, tm, B, T)
            if v_first is None:
                v_first = seqs[2]
            z, s_new = _rwkv_scan(*seqs, vec(p['rwkv_kk'][i]), vec(p['rwkv_ka'][i]), vec(p['rwkv_rk'][i]),
                                  vec(p['rwkv_lnx_g'][i]), vec(p['rwkv_lnx_b'][i]), wkv[i], B, T)
            mix_in, w_mix = [z], p['rwkv_wo'][i]
            new_wkv.append(s_new)
            new_shift.append(last_u)
        last = l == depth - 1
        res = _mix_ffn(mix_in, h, w_mix, vec(p['norm_ffn'][l]), p['w_gate'][l], p['w_up'][l], p['w_down'][l], tm,
                       vec(p['norm_final']) if last else None)
        if last:
            h, y = res
        else:
            h = res
    return (y.reshape(B, T, D), jnp.stack(new_k), jnp.stack(new_v), jnp.stack(new_conv), jnp.stack(new_wkv),
            jnp.stack(new_shift))


_MATMUL_WEIGHTS = ('w_in_ab', 'w_out_ab', 'rwkv_wr', 'rwkv_wk', 'rwkv_wv', 'rwkv_w1', 'rwkv_w2', 'rwkv_a1',
                   'rwkv_a2', 'rwkv_v1', 'rwkv_v2', 'rwkv_g1', 'rwkv_g2', 'rwkv_wo', 'w_gate', 'w_up', 'w_down')


def kernel(x_prompt, x_sample, cache_k, cache_v, state_conv, state_wkv, state_shift, page_table, norm_mix, norm_ffn, norm_final, w_in_ab, diff_lambda, subln, conv_w, conv_b, conv_ln_g, conv_ln_b, w_out_ab, rwkv_mu, rwkv_wr, rwkv_wk, rwkv_wv, rwkv_w0, rwkv_w1, rwkv_w2, rwkv_a0, rwkv_a1, rwkv_a2, rwkv_v0, rwkv_v1, rwkv_v2, rwkv_g1, rwkv_g2, rwkv_kk, rwkv_ka, rwkv_rk, rwkv_lnx_g, rwkv_lnx_b, rwkv_wo, w_gate, w_up, w_down):
    p = dict(norm_mix=norm_mix, norm_ffn=norm_ffn, norm_final=norm_final, w_in_ab=w_in_ab,
             diff_lambda=diff_lambda, subln=subln, conv_w=conv_w, conv_b=conv_b, conv_ln_g=conv_ln_g,
             conv_ln_b=conv_ln_b, w_out_ab=w_out_ab, rwkv_mu=rwkv_mu, rwkv_wr=rwkv_wr, rwkv_wk=rwkv_wk,
             rwkv_wv=rwkv_wv, rwkv_w0=rwkv_w0, rwkv_w1=rwkv_w1, rwkv_w2=rwkv_w2, rwkv_a0=rwkv_a0,
             rwkv_a1=rwkv_a1, rwkv_a2=rwkv_a2, rwkv_v0=rwkv_v0, rwkv_v1=rwkv_v1, rwkv_v2=rwkv_v2,
             rwkv_g1=rwkv_g1, rwkv_g2=rwkv_g2, rwkv_kk=rwkv_kk, rwkv_ka=rwkv_ka, rwkv_rk=rwkv_rk,
             rwkv_lnx_g=rwkv_lnx_g, rwkv_lnx_b=rwkv_lnx_b, rwkv_wo=rwkv_wo, w_gate=w_gate, w_up=w_up,
             w_down=w_down)
    for name in _MATMUL_WEIGHTS:
        p[name] = p[name].astype(BF16)

    Bp, Tp, D = x_prompt.shape
    n_a, n_c = state_conv.shape[0], state_wkv.shape[0]
    zero_conv = jnp.zeros((n_a, Bp, CONV_WIDTH - 1, B_WIDTH), F32)
    zero_wkv = jnp.zeros((n_c, Bp) + state_wkv.shape[2:], F32)
    zero_shift = jnp.zeros((n_c, Bp, D), F32)
    outs_p = _trunk(x_prompt, jnp.arange(Tp), zero_conv, zero_wkv, zero_shift, p, None)

    past = page_table.shape[1] * PAGE_SIZE
    outs_s = _trunk(x_sample, past + jnp.arange(x_sample.shape[1]), state_conv, state_wkv, state_shift, p,
                    (page_table, cache_k, cache_v))
    return (outs_p[0], outs_s[0]) + tuple(outs_p[1:]) + tuple(outs_s[1:])
```

```python
import functools
import math

import jax
import jax.numpy as jnp
from jax import lax
from jax.experimental import pallas as pl
from jax.experimental.pallas import tpu as pltpu

F32 = jnp.float32
BF16 = jnp.bfloat16

NORM_EPS = 1e-6
SUBLN_EPS = 1e-5
LN_EPS = 1e-5
GN_EPS = 64e-5
ROPE_THETA = 500000.0
ROPE_DIM = 16

A_HEADS = 4
A_QK_DIM = 64
A_V_DIM = 128
A_WIDTH = 512
B_WIDTH = 512
CONV_WIDTH = 31
C_HEAD = 64
PAGE_SIZE = 128

LANES = 128
SUBLANES = 8
QUAD = 4 * C_HEAD
CHUNK = 64
FLASH_TQ = 1024
FLASH_TK = 512
FLASH_ROWS = 256
PAGES_PER_STEP = 16
SCAN_SEQS = 2
SCAN_QUADS = 4
CONV_HALO = 32
VMEM_LIMIT = 56 * 1024 * 1024


def _cp(*sem):
    return pltpu.CompilerParams(dimension_semantics=sem, vmem_limit_bytes=VMEM_LIMIT)


def _mm(a, b):
    return jnp.dot(a.astype(BF16), b.astype(BF16), preferred_element_type=F32)


def _mm_nt(a, b):
    return lax.dot_general(a.astype(BF16), b.astype(BF16), (((1,), (1,)), ((), ())), preferred_element_type=F32)


def _mm_tn(a, b):
    return lax.dot_general(a.astype(BF16), b.astype(BF16), (((0,), (0,)), ((), ())), preferred_element_type=F32)


def _split2(x):
    hi = x.astype(BF16)
    lo = (x - hi.astype(F32)).astype(BF16)
    return hi, lo


def _sigmoid(x):
    return 1.0 / (1.0 + jnp.exp(-x))


def _rms(x, g, eps):
    ms = jnp.mean(x * x, axis=-1, keepdims=True)
    return x * lax.rsqrt(ms + eps) * g


def _inproj_kernel(x_ref, g_ref, w_ref, c_ref, s1_ref, s2_ref, q_ref, k_ref, v_ref, glu_ref):
    u = _rms(x_ref[...], g_ref[...], NORM_EPS).astype(BF16)

    def proj(j):
        return jnp.dot(u, w_ref[:, j * A_WIDTH:(j + 1) * A_WIDTH], preferred_element_type=F32)

    ct = jnp.concatenate([c_ref[...]] * A_HEADS, axis=1)
    s1t = jnp.concatenate([s1_ref[...]] * A_HEADS, axis=1)
    s2t = jnp.concatenate([s2_ref[...]] * A_HEADS, axis=1)
    half = ROPE_DIM // 2

    def rope(z):
        return z * ct + pltpu.roll(z, A_WIDTH - half, 1) * s1t + pltpu.roll(z, half, 1) * s2t

    q_ref[...] = rope(proj(0))
    tm = x_ref.shape[0]
    kr = rope(proj(1))
    vv = proj(2)
    for h in range(A_HEADS):
        k_ref[pl.ds(h, tm, stride=A_HEADS), :] = kr[:, h * A_V_DIM:(h + 1) * A_V_DIM]
        v_ref[pl.ds(h, tm, stride=A_HEADS), :] = vv[:, h * A_V_DIM:(h + 1) * A_V_DIM]
    glu_ref[...] = proj(3) * _sigmoid(proj(4))


def _in_proj_ab(h, g, w_bf, tabs, tm):
    M, D = h.shape
    ntab = tabs[0].shape[0] // tm
    tab_spec = pl.BlockSpec((tm, LANES), lambda i: (i % ntab, 0))
    row_out = pl.BlockSpec((tm, A_WIDTH), lambda i: (i, 0))
    kv_out = pl.BlockSpec((tm * A_HEADS, A_V_DIM), lambda i: (i, 0))
    kv_sds = jax.ShapeDtypeStruct((M * A_HEADS, A_V_DIM), F32)
    row_sds = jax.ShapeDtypeStruct((M, A_WIDTH), F32)
    return pl.pallas_call(
        _inproj_kernel,
        grid=(M // tm,),
        in_specs=[pl.BlockSpec((tm, D), lambda i: (i, 0)),
                  pl.BlockSpec((1, D), lambda i: (0, 0)),
                  pl.BlockSpec(w_bf.shape, lambda i: (0, 0)),
                  tab_spec, tab_spec, tab_spec],
        out_specs=[row_out, kv_out, kv_out, row_out],
        out_shape=[row_sds, kv_sds, kv_sds, row_sds],
        compiler_params=_cp("parallel"),
        name="in_proj_ab",
    )(h, g, w_bf, *tabs)


def _rope_tables(pos):
    half = ROPE_DIM // 2
    inv = 1.0 / (ROPE_THETA ** (jnp.arange(0, ROPE_DIM, 2, dtype=F32) / ROPE_DIM))
    ang = pos.astype(F32)[:, None] * inv[None, :]
    cos, sin = jnp.cos(ang), jnp.sin(ang)
    n = pos.shape[0]
    rest = A_QK_DIM - ROPE_DIM
    c = jnp.concatenate([cos, cos, jnp.ones((n, rest), F32)], axis=1)
    s1 = jnp.concatenate([-sin, jnp.zeros((n, A_QK_DIM - half), F32)], axis=1)
    s2 = jnp.concatenate([jnp.zeros((n, half), F32), sin, jnp.zeros((n, rest), F32)], axis=1)
    return tuple(jnp.concatenate([t, t], axis=1) for t in (c, s1, s2))


def _diff_lambda(lam_ref, lam_init):
    lp = lam_ref[...]
    s1 = jnp.sum(lp[0:1] * lp[1:2], axis=-1, keepdims=True)
    s2 = jnp.sum(lp[2:3] * lp[3:4], axis=-1, keepdims=True)
    return jnp.exp(s1) - jnp.exp(s2) + lam_init


def _subln(o, g, lam_init):
    return _rms(o, g, SUBLN_EPS) * (1.0 - lam_init)


def _flash_kernel(qi_ref, ki_ref, lam_ref, sub_ref, q_ref, k_ref, v_ref, o_ref, m_ref, l_ref, acc_ref, *,
                  lam_init, tq, tk, rb):
    pair = pl.program_id(1)
    qi = qi_ref[pair]
    ki = ki_ref[pair]
    nrb = tq // rb
    reps = tk // LANES

    @pl.when(ki == 0)
    def _():
        m_ref[...] = jnp.full(m_ref.shape, -jnp.inf, F32)
        l_ref[...] = jnp.zeros(l_ref.shape, F32)
        acc_ref[...] = jnp.zeros(acc_ref.shape, F32)

    def chain(h, r, masked, kv):
        slab = h * nrb + r
        kb, vb = kv[h]
        q = q_ref[r * rb:(r + 1) * rb, h * A_V_DIM:(h + 1) * A_V_DIM] * (A_QK_DIM ** -0.5 * math.log2(math.e))
        lane = lax.broadcasted_iota(jnp.int32, q.shape, 1)
        q12 = jnp.concatenate([jnp.where(lane < A_QK_DIM, q, 0.0), jnp.where(lane >= A_QK_DIM, q, 0.0)], axis=0)
        s = _mm_nt(q12, kb)
        yield
        if masked is not None:
            row = lax.broadcasted_iota(jnp.int32, s.shape, 0)
            col = lax.broadcasted_iota(jnp.int32, s.shape, 1)
            row = jnp.where(row >= rb, row - rb, row)
            s = jnp.where(col + masked <= row, s, -jnp.inf)
        m_prev = m_ref[slab]
        m_new = jnp.maximum(m_prev, jnp.max(s, axis=1, keepdims=True))
        alpha = jnp.exp2(m_prev - m_new)
        p = jnp.exp2(s - jnp.tile(m_new, (1, reps)))
        pv = jnp.dot(p.astype(BF16), vb, preferred_element_type=F32)
        l_ref[slab] = alpha * l_ref[slab] + pv[:, A_V_DIM:]
        acc_ref[slab] = alpha * acc_ref[slab] + pv[:, :A_V_DIM]
        m_ref[slab] = m_new

    def step(rel):
        ones = jnp.ones((tk, LANES), BF16)
        kv = [(k_ref[pl.ds(h, tk, stride=A_HEADS), :].astype(BF16),
               jnp.concatenate([v_ref[pl.ds(h, tk, stride=A_HEADS), :].astype(BF16), ones], axis=1))
              for h in range(A_HEADS)]
        chains = []
        for h in range(A_HEADS):
            for r in range(nrb):
                off = None if rel is None else rel * tk - r * rb
                if off is not None and off > rb - 1:
                    continue
                if off is not None and off + tk - 1 <= 0:
                    off = None
                chains.append(chain(h, r, off, kv))
        next(chains[0])
        for i, ch in enumerate(chains):
            if i + 1 < len(chains):
                next(chains[i + 1])
            for _ in ch:
                pass

    ratio = tq // tk
    for rel in range(ratio):
        @pl.when(ki == qi * ratio + rel)
        def _(rel=rel):
            step(rel)

    @pl.when(ki < qi * ratio)
    def _():
        step(None)

    @pl.when(ki == (qi + 1) * ratio - 1)
    def _():
        lam = _diff_lambda(lam_ref, lam_init)
        for h in range(A_HEADS):
            for r in range(nrb):
                o = acc_ref[h * nrb + r] / l_ref[h * nrb + r]
                o_ref[r * rb:(r + 1) * rb, h * A_V_DIM:(h + 1) * A_V_DIM] = _subln(
                    o[:rb] - lam * o[rb:], sub_ref[...], lam_init)


def _flash_prompt(q, k, v, lam_p, subln, lam_init, B, T, tq, tk, rb):
    nq, nk, ratio = T // tq, T // tk, tq // tk
    pairs = [(i, j) for i in range(nq) for j in range((i + 1) * ratio)]
    qi_tab = jnp.array([pq for pq, _ in pairs], jnp.int32)
    ki_tab = jnp.array([pk for _, pk in pairs], jnp.int32)
    kern = functools.partial(_flash_kernel, lam_init=lam_init, tq=tq, tk=tk, rb=rb)
    kv_spec = pl.BlockSpec((tk * A_HEADS, A_V_DIM), lambda b, s, qt, kt: (b * nk + kt[s], 0))
    q_spec = pl.BlockSpec((tq, A_WIDTH), lambda b, s, qt, kt: (b * nq + qt[s], 0))
    slabs = A_HEADS * (tq // rb)
    return pl.pallas_call(
        kern,
        grid_spec=pltpu.PrefetchScalarGridSpec(
            num_scalar_prefetch=2,
            grid=(B, len(pairs)),
            in_specs=[pl.BlockSpec(lam_p.shape, lambda b, s, qt, kt: (0, 0)),
                      pl.BlockSpec((1, A_V_DIM), lambda b, s, qt, kt: (0, 0)),
                      q_spec, kv_spec, kv_spec],
            out_specs=q_spec,
            scratch_shapes=[pltpu.VMEM((slabs, 2 * rb, LANES), F32), pltpu.VMEM((slabs, 2 * rb, LANES), F32),
                            pltpu.VMEM((slabs, 2 * rb, A_V_DIM), F32)]),
        out_shape=jax.ShapeDtypeStruct((B * T, A_WIDTH), F32),
        compiler_params=_cp("parallel", "arbitrary"),
        name="diff_attn_prompt",
    )(qi_tab, ki_tab, lam_p, subln, q, k, v)


def _paged_kernel(pt_ref, lam_ref, sub_ref, q_ref, kn_ref, vn_ref, *rest, lam_init, pp, tn):
    k_refs = rest[:pp]
    v_refs = rest[pp:2 * pp]
    o_ref, m_ref, l_ref, acc_ref = rest[2 * pp:]
    j = pl.program_id(1)
    hrows = 2 * SUBLANES

    @pl.when(j == 0)
    def _():
        m_ref[...] = jnp.full(m_ref.shape, -jnp.inf, F32)
        l_ref[...] = jnp.zeros(l_ref.shape, F32)
        acc_ref[...] = jnp.zeros(acc_ref.shape, F32)

    qb = (q_ref[0] * (A_QK_DIM ** -0.5)).astype(BF16)

    def update(keys_of, vals_of, mask=None):
        s = jnp.concatenate([_mm_nt(qb[h * hrows:(h + 1) * hrows], keys_of(h)) for h in range(A_HEADS)], axis=0)
        if mask is not None:
            s = jnp.where(mask(s.shape), s, -jnp.inf)
        m_prev = m_ref[...]
        m_new = jnp.maximum(m_prev, jnp.max(s, axis=1, keepdims=True))
        alpha = jnp.exp(m_prev - m_new)
        p = jnp.exp(s - m_new)
        l_ref[...] = alpha * l_ref[...] + jnp.sum(p, axis=1, keepdims=True)
        pv = jnp.concatenate([_mm(p[h * hrows:(h + 1) * hrows], vals_of(h)) for h in range(A_HEADS)], axis=0)
        acc_ref[...] = alpha * acc_ref[...] + pv
        m_ref[...] = m_new

    rows_of = lambda ref, h: ref[0, 0, pl.ds(h, PAGE_SIZE, stride=A_HEADS), :]
    update(lambda h: jnp.concatenate([rows_of(kr, h) for kr in k_refs], axis=0),
           lambda h: jnp.concatenate([rows_of(vr, h) for vr in v_refs], axis=0))

    @pl.when(j == pl.num_programs(1) - 1)
    def _():
        def causal(shape):
            tok = lax.broadcasted_iota(jnp.int32, shape, 0) % SUBLANES
            col = lax.broadcasted_iota(jnp.int32, shape, 1)
            return (col <= tok) & (col < tn)

        head = lambda ref, h: ref[0, pl.ds(h, SUBLANES, stride=A_HEADS), :]
        update(lambda h: head(kn_ref, h), lambda h: head(vn_ref, h), causal)
        o = acc_ref[...] / l_ref[...]
        lam = _diff_lambda(lam_ref, lam_init)
        for h in range(A_HEADS):
            d = o[h * hrows:h * hrows + SUBLANES] - lam * o[h * hrows + SUBLANES:(h + 1) * hrows]
            o_ref[0, h] = _subln(d, sub_ref[...], lam_init)


def _paged_sample(page_table, q, k_new, v_new, cache_k, cache_v, layer, lam_p, subln, lam_init, Bd, tn):
    n_pages = page_table.shape[1]
    pp = math.gcd(n_pages, PAGES_PER_STEP)
    tpad = SUBLANES - tn
    lane_map = jnp.arange(A_V_DIM) // A_QK_DIM
    keep = (lane_map[None, :] == jnp.arange(2)[:, None]).astype(F32)
    q4 = q.reshape(Bd, tn, A_HEADS, A_V_DIM).transpose(0, 2, 1, 3)
    qm = q4[:, :, None, :, :] * keep[None, None, :, None, :]
    qm = jnp.pad(qm, ((0, 0), (0, 0), (0, 0), (0, tpad), (0, 0))).reshape(Bd, A_HEADS * 2 * SUBLANES, A_V_DIM)
    pad = ((0, 0), (0, tpad * A_HEADS), (0, 0))
    kn = jnp.pad(k_new.reshape(Bd, tn * A_HEADS, A_V_DIM), pad)
    vn = jnp.pad(v_new.reshape(Bd, tn * A_HEADS, A_V_DIM), pad)

    page_rows = PAGE_SIZE * A_HEADS
    cache_k = cache_k.reshape(cache_k.shape[:2] + (page_rows, A_V_DIM))
    cache_v = cache_v.reshape(cache_v.shape[:2] + (page_rows, A_V_DIM))

    def page_spec(i):
        return pl.BlockSpec((1, 1, page_rows, A_V_DIM), lambda b, j, pt: (layer, pt[b, j * pp + i], 0, 0))

    per_b = lambda b, j, pt: (b, 0, 0)
    nrow = A_HEADS * 2 * SUBLANES
    kern = functools.partial(_paged_kernel, lam_init=lam_init, pp=pp, tn=tn)
    out = pl.pallas_call(
        kern,
        grid_spec=pltpu.PrefetchScalarGridSpec(
            num_scalar_prefetch=1,
            grid=(Bd, n_pages // pp),
            in_specs=[pl.BlockSpec(lam_p.shape, lambda b, j, pt: (0, 0)),
                      pl.BlockSpec((1, A_V_DIM), lambda b, j, pt: (0, 0)),
                      pl.BlockSpec((1, nrow, A_V_DIM), per_b),
                      pl.BlockSpec((1, SUBLANES * A_HEADS, A_V_DIM), per_b),
                      pl.BlockSpec((1, SUBLANES * A_HEADS, A_V_DIM), per_b)]
                     + [page_spec(i) for i in range(pp)] * 2,
            out_specs=pl.BlockSpec((1, A_HEADS, SUBLANES, A_V_DIM), lambda b, j, pt: (b, 0, 0, 0)),
            scratch_shapes=[pltpu.VMEM((nrow, 1), F32), pltpu.VMEM((nrow, 1), F32),
                            pltpu.VMEM((nrow, A_V_DIM), F32)]),
        out_shape=jax.ShapeDtypeStruct((Bd, A_HEADS, SUBLANES, A_V_DIM), F32),
        compiler_params=_cp("parallel", "arbitrary"),
        name="diff_attn_paged",
    )(page_table, lam_p, subln, qm, kn, vn, *([cache_k] * pp), *([cache_v] * pp))
    return out[:, :, :tn].transpose(0, 2, 1, 3).reshape(Bd * tn, A_WIDTH)


def _conv_post(y, b_ref, lg_ref, lb_ref):
    y = y + b_ref[...]
    mu = jnp.mean(y, axis=-1, keepdims=True)
    d = y - mu
    var = jnp.mean(d * d, axis=-1, keepdims=True)
    z = d * lax.rsqrt(var + LN_EPS) * lg_ref[...] + lb_ref[...]
    return z * _sigmoid(z)


def _conv_kernel(glu_ref, halo_ref, buf_ref, w_ref, b_ref, lg_ref, lb_ref, o_ref, xp_ref, xs_ref, *, tt, sub):
    j = pl.program_id(1)
    rows = CONV_HALO + tt
    xp_ref[0:CONV_HALO, :] = jnp.where(j == 0, buf_ref[0], halo_ref[...])
    xp_ref[CONV_HALO:, :] = glu_ref[...]
    for ph in range(SUBLANES):
        xs_ref[ph, 0:rows - ph, :] = xp_ref[ph:rows, :]
    lead = CONV_HALO - (CONV_WIDTH - 1)
    w = w_ref[...]
    for r0 in range(0, tt, sub):
        acc = None
        for tap in range(CONV_WIDTH):
            ph, base = (lead + tap) % SUBLANES, (lead + tap) // SUBLANES * SUBLANES
            term = xs_ref[ph, r0 + base:r0 + base + sub, :] * w[tap:tap + 1]
            acc = term if acc is None else acc + term
        o_ref[r0:r0 + sub, :] = _conv_post(acc, b_ref, lg_ref, lb_ref)


def _conv_prompt(glu, buf, w, b, lg, lb, B, T, tt):
    nt = T // tt
    hpt = tt // CONV_HALO
    bufp = jnp.pad(buf, ((0, 0), (CONV_HALO - (CONV_WIDTH - 1), 0), (0, 0)))
    vec = pl.BlockSpec((1, B_WIDTH), lambda bb, j: (0, 0))
    kern = functools.partial(_conv_kernel, tt=tt, sub=min(tt, 64))
    return pl.pallas_call(
        kern,
        grid=(B, nt),
        in_specs=[pl.BlockSpec((tt, B_WIDTH), lambda bb, j: (bb * nt + j, 0)),
                  pl.BlockSpec((CONV_HALO, B_WIDTH), lambda bb, j: (jnp.maximum((bb * nt + j) * hpt - 1, 0), 0)),
                  pl.BlockSpec((1, CONV_HALO, B_WIDTH), lambda bb, j: (bb, 0, 0)),
                  pl.BlockSpec((CONV_WIDTH, B_WIDTH), lambda bb, j: (0, 0)),
                  vec, vec, vec],
        out_specs=pl.BlockSpec((tt, B_WIDTH), lambda bb, j: (bb * nt + j, 0)),
        out_shape=jax.ShapeDtypeStruct((B * T, B_WIDTH), F32),
        scratch_shapes=[pltpu.VMEM((CONV_HALO + tt, B_WIDTH), F32),
                        pltpu.VMEM((SUBLANES, CONV_HALO + tt, B_WIDTH), F32)],
        compiler_params=_cp("parallel", "arbitrary"),
        name="conv_prompt",
    )(glu, glu, bufp, w, b, lg, lb)


def _conv_step_kernel(xp_ref, w_ref, b_ref, lg_ref, lb_ref, o_ref, *, tn):
    w = w_ref[...]
    for t in range(tn):
        acc = xp_ref[t] * w[0:1]
        for tap in range(1, CONV_WIDTH):
            acc = acc + xp_ref[t + tap] * w[tap:tap + 1]
        o_ref[t] = _conv_post(acc, b_ref, lg_ref, lb_ref)


def _conv_sample(glu, buf, w, b, lg, lb, Bd, tn):
    xp = jnp.concatenate([buf, glu.reshape(Bd, tn, B_WIDTH)], axis=1)
    out = pl.pallas_call(
        functools.partial(_conv_step_kernel, tn=tn),
        out_shape=jax.ShapeDtypeStruct((tn, Bd, B_WIDTH), F32),
        compiler_params=pltpu.CompilerParams(vmem_limit_bytes=VMEM_LIMIT),
        name="conv_sample",
    )(xp.transpose(1, 0, 2), w, b, lg, lb)
    return out.transpose(1, 0, 2).reshape(Bd * tn, B_WIDTH), xp[:, -(CONV_WIDTH - 1):]


def _mix_ffn_kernel(*refs, n_in, final):
    x_refs = refs[:n_in]
    h_ref, wo_ref, g_ref, wg_ref, wu_ref, wd_ref = refs[n_in:n_in + 6]
    rest = refs[n_in + 6:]
    h = h_ref[...]
    k0 = 0
    for x_ref in x_refs:
        kw = x_ref.shape[1]
        h = h + jnp.dot(x_ref[...].astype(BF16), wo_ref[k0:k0 + kw, :], preferred_element_type=F32)
        k0 += kw
    f = _rms(h, g_ref[...], NORM_EPS).astype(BF16)
    gate = jnp.dot(f, wg_ref[...], preferred_element_type=F32)
    up = jnp.dot(f, wu_ref[...], preferred_element_type=F32)
    act = (gate * _sigmoid(gate) * up).astype(BF16)
    out = h + jnp.dot(act, wd_ref[...], preferred_element_type=F32)
    if final:
        gf_ref, o_ref, y_ref = rest
        y_ref[...] = _rms(out, gf_ref[...], NORM_EPS)
    else:
        (o_ref,) = rest
    o_ref[...] = out


def _mix_ffn(xs, h, wo, g, wg, wu, wd, tm, g_final=None):
    M, D = h.shape
    final = g_final is not None
    row = pl.BlockSpec((tm, D), lambda i: (i, 0))
    vec = pl.BlockSpec((1, D), lambda i: (0, 0))
    full = lambda a: pl.BlockSpec(a.shape, lambda i: (0, 0), pipeline_mode=pl.Buffered(1))
    ins = list(xs) + [h, wo, g, wg, wu, wd] + ([g_final] if final else [])
    in_specs = ([pl.BlockSpec((tm, x.shape[1]), lambda i: (i, 0)) for x in xs]
                + [row, full(wo), vec, full(wg), full(wu), full(wd)] + ([vec] if final else []))
    sds = jax.ShapeDtypeStruct((M, D), F32)
    return pl.pallas_call(
        functools.partial(_mix_ffn_kernel, n_in=len(xs), final=final),
        grid=(M // tm,),
        in_specs=in_specs,
        out_specs=[row, row] if final else row,
        out_shape=[sds, sds] if final else sds,
        compiler_params=_cp("parallel"),
        name="mix_ffn",
    )(*ins)


def _rwkv_proj_kernel(*refs, has_vmix, seq_tiles, short_t):
    it = iter(refs)
    h_ref, hprev_ref, shift_ref, g_ref, mu_ref = next(it), next(it), next(it), next(it), next(it)
    wr_ref, wk_ref, wv_ref = next(it), next(it), next(it)
    w0_ref, w1_ref, w2_ref = next(it), next(it), next(it)
    a0_ref, a1_ref, a2_ref = next(it), next(it), next(it)
    g1_ref, g2_ref = next(it), next(it)
    if has_vmix:
        v0_ref, v1_ref, v2_ref, vfirst_ref = next(it), next(it), next(it), next(it)
    u_ref, r_ref, k_ref, v_ref, ld_ref, a_ref, gg_ref = (next(it) for _ in range(7))

    i = pl.program_id(0)
    u = _rms(h_ref[...], g_ref[...], NORM_EPS)
    keep = u_ref.shape[1]
    u_ref[0] = u[u.shape[0] - keep:]
    rolled = pltpu.roll(u, 1, 0)
    row = lax.broadcasted_iota(jnp.int32, u.shape, 0)
    if short_t:
        prev = jnp.where(row % short_t == 0, shift_ref[...], rolled)
    else:
        up8 = _rms(hprev_ref[...], g_ref[...], NORM_EPS)
        first = jnp.where(i % seq_tiles == 0, shift_ref[0], up8[SUBLANES - 1:SUBLANES])
        prev = jnp.where(row == 0, first, rolled)
    xx = prev - u
    mu = mu_ref[...]
    xr, xw, xk, xv, xa, xg = (u + xx * mu[j:j + 1] for j in range(6))

    r_ref[...] = _mm(xr, wr_ref[...])
    k_ref[...] = _mm(xk, wk_ref[...])
    v = _mm(xv, wv_ref[...])
    wl = w0_ref[...] + _mm(jnp.tanh(_mm(xw, w1_ref[...])), w2_ref[...])
    z = -wl
    w = -(jnp.maximum(z, 0.0) + jnp.log(1.0 + jnp.exp(-jnp.abs(z)))) - 0.5
    ld_ref[...] = -jnp.exp(w)
    if has_vmix:
        mix = _sigmoid(v0_ref[...] + _mm(_mm(xv, v1_ref[...]), v2_ref[...]))
        v = v + (vfirst_ref[...] - v) * mix
    v_ref[...] = v
    a_ref[...] = _sigmoid(a0_ref[...] + _mm(_mm(xa, a1_ref[...]), a2_ref[...]))
    gg_ref[...] = _mm(_sigmoid(_mm(xg, g1_ref[...])), g2_ref[...])


def _rwkv_proj(h, shift, g, mu, wts, vmix, tm, B, T):
    M, D = h.shape
    short = T < tm
    row = pl.BlockSpec((tm, D), lambda i: (i, 0))
    vec = pl.BlockSpec((1, D), lambda i: (0, 0))
    full = lambda a: pl.BlockSpec(a.shape, lambda i: (0, 0))
    if short:
        seq_tiles = 1
        shift_rows = jnp.zeros((B, T, D), F32).at[:, 0].set(shift).reshape(M, D)
        shift_in, shift_spec = shift_rows, row
        hprev_spec = pl.BlockSpec((SUBLANES, D), lambda i: (0, 0))
    else:
        seq_tiles = T // tm
        shift_in = shift.reshape(B, 1, D)
        shift_spec = pl.BlockSpec((1, 1, D), lambda i: (i // seq_tiles, 0, 0))
        hprev_spec = pl.BlockSpec((SUBLANES, D), lambda i: (jnp.maximum(i * (tm // SUBLANES) - 1, 0), 0))
    ins = [h, h, shift_in, g, mu] + list(wts)
    in_specs = [row, hprev_spec, shift_spec, vec, full(mu)] + [full(a) for a in wts]
    if vmix is not None:
        v0, v1, v2, v_first = vmix
        ins += [v0, v1, v2, v_first]
        in_specs += [vec, full(v1), full(v2), row]
    sds = jax.ShapeDtypeStruct((M, D), F32)
    keep = tm if short else SUBLANES
    kern = functools.partial(_rwkv_proj_kernel, has_vmix=vmix is not None, seq_tiles=seq_tiles,
                             short_t=T if short else 0)
    tail, *seqs = pl.pallas_call(
        kern,
        grid=(M // tm,),
        in_specs=in_specs,
        out_specs=[pl.BlockSpec((1, keep, D), lambda i: (i, 0, 0))] + [row] * 6,
        out_shape=[jax.ShapeDtypeStruct((M // tm, keep, D), F32)] + [sds] * 6,
        compiler_params=_cp("parallel"),
        name="rwkv_proj",
    )(*ins)
    if short:
        last_u = tail.reshape(B, T, D)[:, -1]
    else:
        last_u = tail.reshape(B, seq_tiles, keep, D)[:, -1, -1]
    return last_u, seqs


def _scan_kernel(r_ref, k_ref, v_ref, ld_ref, a_ref, g_ref, kkp_ref, ka_ref, rk_ref, lg_ref, lb_ref, h0_ref,
                 z_ref, hout_ref, bdh_ref, pad_ref, *, nb, nqd):
    c = pl.program_id(2)
    rq = lax.broadcasted_iota(jnp.int32, (QUAD, QUAD), 0)
    cq = lax.broadcasted_iota(jnp.int32, (QUAD, QUAD), 1)
    same_head = (rq // C_HEAD) == (cq // C_HEAD)
    units = [(i, qd) for i in range(nb) for qd in range(nqd)]

    @pl.when(c == 0)
    def _():
        for u, (i, qd) in enumerate(units):
            hc = h0_ref[i, qd]
            bdh_ref[u] = jnp.where(same_head, jnp.concatenate([hc] * 4, axis=1), 0.0)

    block_ones = jnp.where(same_head, 1.0, 0.0).astype(BF16)
    gens = [_scan_chunk(u, i, qd, same_head, r_ref, k_ref, v_ref, ld_ref, a_ref, g_ref, kkp_ref, ka_ref, rk_ref,
                        lg_ref, lb_ref, z_ref, bdh_ref, pad_ref) for u, (i, qd) in enumerate(units)]
    hnews = [None] * len(units)
    answers = [None] * len(units)
    while any(h is None for h in hnews):
        asks = {}
        for u in range(len(units)):
            if hnews[u] is None:
                try:
                    ask = gens[u].send(answers[u])
                    if ask is not None:
                        asks[u] = ask
                except StopIteration as done:
                    hnews[u] = done.value
        answers = [None] * len(units)
        if asks:
            sums = _mm(jnp.concatenate(list(asks.values()), axis=0), block_ones)
            row = 0
            for u, ask in asks.items():
                answers[u] = sums[row:row + ask.shape[0]]
                row += ask.shape[0]

    @pl.when(c == pl.num_programs(2) - 1)
    def _():
        for u, (i, qd) in enumerate(units):
            hc = hnews[u][:, 0:C_HEAD]
            for hh in range(1, 4):
                hc = hc + hnews[u][:, hh * C_HEAD:(hh + 1) * C_HEAD]
            hout_ref[i, qd] = hc


def _scan_chunk(u, i, qd, same_head, r_ref, k_ref, v_ref, ld_ref, a_ref, g_ref, kkp_ref, ka_ref, rk_ref, lg_ref,
                lb_ref, z_ref, bdh_ref, pad_ref):
    C = CHUNK
    lanes = slice(qd * QUAD, (qd + 1) * QUAD)
    t_valid = r_ref.shape[1]

    def bd(x):
        return jnp.where(same_head, jnp.concatenate([x] * 4, axis=0), 0.0)

    def load(j, ref):
        if t_valid == C:
            return ref[i, :, lanes]
        pad_ref[j, u] = jnp.zeros((C, QUAD), F32)
        pad_ref[j, u, 0:t_valid, :] = ref[i, :, lanes]
        return pad_ref[j, u]

    r, k, v, ld, a = (load(j, ref) for j, ref in enumerate((r_ref, k_ref, v_ref, ld_ref, a_ref)))
    kkp, ka, rkp, lg, lb = (ref[:, lanes] for ref in (kkp_ref, ka_ref, rk_ref, lg_ref, lb_ref))

    tt = lax.broadcasted_iota(jnp.int32, (C, C), 0)
    ts = lax.broadcasted_iota(jnp.int32, (C, C), 1)
    tri = jnp.where(ts <= tt, 1.0, 0.0).astype(BF16)
    lc = sum(jnp.dot(tri, p, preferred_element_type=F32) for p in _split2(ld))
    lend = lc[C - 1:C, :]
    gam = jnp.exp(lc)
    gam_prev = jnp.exp(lc - ld)
    gam_inv = jnp.exp(-lc)
    gam_end = jnp.exp(lend - lc)
    yield

    kk = k * kkp
    k2 = k * (1.0 + (a - 1.0) * ka)
    head_sums = yield jnp.concatenate([kk * kk, r * k2 * rkp], axis=0)
    ss, rk_sum = head_sums[:C], head_sums[C:]
    kkn = kk / jnp.maximum(jnp.sqrt(ss), 1e-12)
    bv = kkn * a
    at = -kkn * gam_prev
    rt = r * gam
    lhs = jnp.concatenate([at, rt], axis=0)
    ab = _mm_nt(lhs, bd(bv * gam_inv))
    ak = _mm_nt(lhs, bd(k2 * gam_inv))
    yield
    t_i = lax.broadcasted_iota(jnp.int32, (C, QUAD), 0)
    s_i = lax.broadcasted_iota(jnp.int32, (C, QUAD), 1) % C_HEAD
    strict = s_i < t_i
    incl = s_i <= t_i
    a_ab = jnp.where(strict, ab[:C], 0.0)
    m_rb = jnp.where(incl, ab[C:], 0.0)
    a_ak = jnp.where(strict, ak[:C], 0.0)
    m_rk = jnp.where(incl, ak[C:], 0.0)

    p = a_ab
    tinv = jnp.where(s_i == t_i, 1.0, 0.0) + p
    n_sq = C.bit_length() - 2
    p = _mm(p, bd(p))
    yield
    for jj in range(n_sq):
        w = bd(p)
        if jj < n_sq - 1:
            pt = _mm(jnp.concatenate([p, tinv], axis=0), w)
            p = pt[:C]
            tinv = tinv + pt[C:]
        else:
            tinv = tinv + _mm(tinv, w)
        yield

    avrv = _mm(jnp.concatenate([a_ak, m_rk], axis=0), bd(v))
    x2, rv = avrv[:C], avrv[C:]
    yield
    w1 = _mm(tinv, bd(at))
    w2 = _mm(tinv, bd(x2))
    yield
    y1 = rt + _mm(m_rb, bd(w1))
    y2 = rv + _mm(m_rb, bd(w2))
    yield
    be = bv * gam_end
    ke = k2 * gam_end
    fresh = _mm_tn(jnp.concatenate([w2, v], axis=0), jnp.concatenate([be, ke], axis=0))
    yield
    gmat = bdh_ref[u]
    y = _mm_nt(y1, gmat) + y2
    ut = _mm_nt(gmat, w1)
    yield
    carried = _mm(ut, be)
    hnew = gmat * jnp.exp(lend) + jnp.where(same_head, carried + fresh, 0.0)
    bdh_ref[u] = hnew
    yield

    inv_n = 1.0 / C_HEAD
    mean = (yield y) * inv_n
    d = y - mean
    var = (yield d * d) * inv_n
    yn = d * lax.rsqrt(var + GN_EPS) * lg + lb
    bonus = rk_sum * v
    z_ref[i, :, lanes] = ((yn + bonus)[0:t_valid]) * g_ref[i, :, lanes]
    return hnew


def _rwkv_scan(r, k, v, ld, a, g, kkp, ka, rk, lg, lb, state, B, T):
    D = r.shape[1]
    nq = D // QUAD
    tc = min(T, CHUNK)
    nc = T // tc
    nb = math.gcd(B, SCAN_SEQS)
    nqd = math.gcd(nq, SCAN_QUADS)
    h0 = state.reshape(B, nq, QUAD, C_HEAD)
    seq = pl.BlockSpec((nb, tc, nqd * QUAD), lambda b, q, c: (b, c, q))
    vec = pl.BlockSpec((1, nqd * QUAD), lambda b, q, c: (0, q))
    st = pl.BlockSpec((nb, nqd, QUAD, C_HEAD), lambda b, q, c: (b, q, 0, 0))
    z, hout = pl.pallas_call(
        functools.partial(_scan_kernel, nb=nb, nqd=nqd),
        grid=(B // nb, nq // nqd, nc),
        in_specs=[seq] * 6 + [vec] * 5 + [st],
        out_specs=[seq, st],
        out_shape=[jax.ShapeDtypeStruct((B, T, D), F32), jax.ShapeDtypeStruct((B, nq, QUAD, C_HEAD), F32)],
        scratch_shapes=[pltpu.VMEM((nb * nqd, QUAD, QUAD), F32),
                        pltpu.VMEM((5, nb * nqd, CHUNK, QUAD) if tc < CHUNK else (1, 1, SUBLANES, LANES), F32)],
        compiler_params=_cp("parallel", "parallel", "arbitrary"),
        name="rwkv_scan",
    )(*(s.reshape(B, T, D) for s in (r, k, v, ld, a, g)), kkp, ka, rk, lg, lb, h0)
    new_state = hout.reshape(B, D // C_HEAD, C_HEAD, C_HEAD)
    return z.reshape(B * T, D), new_state


def _row_tile(M, pref):
    return pref if M % pref == 0 else M


def _trunk(x, pos, conv_buf, wkv, shift, p, paged):
    B, T, D = x.shape
    M = B * T
    tm = _row_tile(M, 256)
    h = x.reshape(M, D)
    depth = p['norm_mix'].shape[0]
    tabs = _rope_tables(pos)
    if paged is not None:
        tabs = tuple(jnp.tile(t, (B, 1)) for t in tabs)
    vec = lambda a: a.reshape(1, -1)
    new_k, new_v, new_conv, new_wkv, new_shift = [], [], [], [], []
    v_first = None
    y = None
    for l in range(depth):
        i = l // 2
        if l % 2 == 0:
            q, k, v, glu = _in_proj_ab(h, vec(p['norm_mix'][l]), p['w_in_ab'][i], tabs, tm)
            lam_init = 0.8 - 0.6 * math.exp(-0.3 * l)
            lam_p, sub = p['diff_lambda'][i], vec(p['subln'][i])
            cw, cb = p['conv_w'][i], vec(p['conv_b'][i])
            clg, clb = vec(p['conv_ln_g'][i]), vec(p['conv_ln_b'][i])
            if paged is None:
                tk = _row_tile(T, FLASH_TK)
                tq = _row_tile(T, FLASH_TQ)
                o = _flash_prompt(q, k, v, lam_p, sub, lam_init, B, T, tq, tk, min(tq, FLASH_ROWS))
                cm = _conv_prompt(glu, conv_buf[i], cw, cb, clg, clb, B, T, _row_tile(T, 256))
                buf = glu.reshape(B, T, B_WIDTH)[:, -(CONV_WIDTH - 1):]
            else:
                page_table, cache_k, cache_v = paged
                o = _paged_sample(page_table, q, k, v, cache_k, cache_v, i, lam_p, sub, lam_init, B, T)
                cm, buf = _conv_sample(glu, conv_buf[i], cw, cb, clg, clb, B, T)
            mix_in, w_mix = [o, cm], p['w_out_ab'][i]
            new_k.append(k.reshape(B, T, A_HEADS, 2 * A_QK_DIM))
            new_v.append(v.reshape(B, T, A_HEADS, A_V_DIM))
            new_conv.append(buf)
        else:
            wts = [p['rwkv_wr'][i], p['rwkv_wk'][i], p['rwkv_wv'][i],
                   vec(p['rwkv_w0'][i]), p['rwkv_w1'][i], p['rwkv_w2'][i],
                   vec(p['rwkv_a0'][i]), p['rwkv_a1'][i], p['rwkv_a2'][i],
                   p['rwkv_g1'][i], p['rwkv_g2'][i]]
            vmix = None
            if v_first is not None:
                vmix = (vec(p['rwkv_v0'][i - 1]), p['rwkv_v1'][i - 1], p['rwkv_v2'][i - 1], v_first)
            last_u, seqs = _rwkv_proj(h, shift[i], vec(p['norm_mix'][l]), p['rwkv_mu'][i], wts, vmix, tm, B, T)
            if v_first is None:
                v_first = seqs[2]
            z, s_new = _rwkv_scan(*seqs, vec(p['rwkv_kk'][i]), vec(p['rwkv_ka'][i]), vec(p['rwkv_rk'][i]),
                                  vec(p['rwkv_lnx_g'][i]), vec(p['rwkv_lnx_b'][i]), wkv[i], B, T)
            mix_in, w_mix = [z], p['rwkv_wo'][i]
            new_wkv.append(s_new)
            new_shift.append(last_u)
        last = l == depth - 1
        res = _mix_ffn(mix_in, h, w_mix, vec(p['norm_ffn'][l]), p['w_gate'][l], p['w_up'][l], p['w_down'][l], tm,
                       vec(p['norm_final']) if last else None)
        if last:
            h, y = res
        else:
            h = res
    return (y.reshape(B, T, D), jnp.stack(new_k), jnp.stack(new_v), jnp.stack(new_conv), jnp.stack(new_wkv),
            jnp.stack(new_shift))


_MATMUL_WEIGHTS = ('w_in_ab', 'w_out_ab', 'rwkv_wr', 'rwkv_wk', 'rwkv_wv', 'rwkv_w1', 'rwkv_w2', 'rwkv_a1',
                   'rwkv_a2', 'rwkv_v1', 'rwkv_v2', 'rwkv_g1', 'rwkv_g2', 'rwkv_wo', 'w_gate', 'w_up', 'w_down')


def kernel(x_prompt, x_sample, cache_k, cache_v, state_conv, state_wkv, state_shift, page_table, norm_mix, norm_ffn, norm_final, w_in_ab, diff_lambda, subln, conv_w, conv_b, conv_ln_g, conv_ln_b, w_out_ab, rwkv_mu, rwkv_wr, rwkv_wk, rwkv_wv, rwkv_w0, rwkv_w1, rwkv_w2, rwkv_a0, rwkv_a1, rwkv_a2, rwkv_v0, rwkv_v1, rwkv_v2, rwkv_g1, rwkv_g2, rwkv_kk, rwkv_ka, rwkv_rk, rwkv_lnx_g, rwkv_lnx_b, rwkv_wo, w_gate, w_up, w_down):
    p = dict(norm_mix=norm_mix, norm_ffn=norm_ffn, norm_final=norm_final, w_in_ab=w_in_ab,
             diff_lambda=diff_lambda, subln=subln, conv_w=conv_w, conv_b=conv_b, conv_ln_g=conv_ln_g,
             conv_ln_b=conv_ln_b, w_out_ab=w_out_ab, rwkv_mu=rwkv_mu, rwkv_wr=rwkv_wr, rwkv_wk=rwkv_wk,
             rwkv_wv=rwkv_wv, rwkv_w0=rwkv_w0, rwkv_w1=rwkv_w1, rwkv_w2=rwkv_w2, rwkv_a0=rwkv_a0,
             rwkv_a1=rwkv_a1, rwkv_a2=rwkv_a2, rwkv_v0=rwkv_v0, rwkv_v1=rwkv_v1, rwkv_v2=rwkv_v2,
             rwkv_g1=rwkv_g1, rwkv_g2=rwkv_g2, rwkv_kk=rwkv_kk, rwkv_ka=rwkv_ka, rwkv_rk=rwkv_rk,
             rwkv_lnx_g=rwkv_lnx_g, rwkv_lnx_b=rwkv_lnx_b, rwkv_wo=rwkv_wo, w_gate=w_gate, w_up=w_up,
             w_down=w_down)
    for name in _MATMUL_WEIGHTS:
        p[name] = p[name].astype(BF16)

    Bp, Tp, D = x_prompt.shape
    n_a, n_c = state_conv.shape[0], state_wkv.shape[0]
    zero_conv = jnp.zeros((n_a, Bp, CONV_WIDTH - 1, B_WIDTH), F32)
    zero_wkv = jnp.zeros((n_c, Bp) + state_wkv.shape[2:], F32)
    zero_shift = jnp.zeros((n_c, Bp, D), F32)
    outs_p = _trunk(x_prompt, jnp.arange(Tp), zero_conv, zero_wkv, zero_shift, p, None)

    past = page_table.shape[1] * PAGE_SIZE
    outs_s = _trunk(x_sample, past + jnp.arange(x_sample.shape[1]), state_conv, state_wkv, state_shift, p,
                    (page_table, cache_k, cache_v))
    return (outs_p[0], outs_s[0]) + tuple(outs_p[1:]) + tuple(outs_s[1:])
```

```python
import functools
import math

import jax
import jax.numpy as jnp
from jax import lax
from jax.experimental import pallas as pl
from jax.experimental.pallas import tpu as pltpu

F32 = jnp.float32
BF16 = jnp.bfloat16

NORM_EPS = 1e-6
SUBLN_EPS = 1e-5
LN_EPS = 1e-5
GN_EPS = 64e-5
ROPE_THETA = 500000.0
ROPE_DIM = 16

A_HEADS = 4
A_QK_DIM = 64
A_V_DIM = 128
A_WIDTH = 512
B_WIDTH = 512
CONV_WIDTH = 31
C_HEAD = 64
PAGE_SIZE = 128

LANES = 128
SUBLANES = 8
QUAD = 4 * C_HEAD
CHUNK = 64
ROW_TILE = 256
WIDE_ROW_TILE = 512
FLASH_TQ = 1024
FLASH_TK = 1024
FLASH_ROWS = 512
PAGES_PER_STEP = 16
SCAN_SEQS = 2
SCAN_QUADS = 4
CONV_HALO = 32
VMEM_LIMIT = 56 * 1024 * 1024


def _cp(*sem):
    return pltpu.CompilerParams(dimension_semantics=sem, vmem_limit_bytes=VMEM_LIMIT)


def _mm(a, b):
    return jnp.dot(a.astype(BF16), b.astype(BF16), preferred_element_type=F32)


def _mm_nt(a, b):
    return lax.dot_general(a.astype(BF16), b.astype(BF16), (((1,), (1,)), ((), ())), preferred_element_type=F32)


def _mm_tn(a, b):
    return lax.dot_general(a.astype(BF16), b.astype(BF16), (((0,), (0,)), ((), ())), preferred_element_type=F32)


def _split2(x):
    hi = x.astype(BF16)
    lo = (x - hi.astype(F32)).astype(BF16)
    return hi, lo


def _sigmoid(x):
    return 1.0 / (1.0 + jnp.exp(-x))


def _rms(x, g, eps):
    ms = jnp.mean(x * x, axis=-1, keepdims=True)
    return x * lax.rsqrt(ms + eps) * g


def _inproj_kernel(x_ref, g_ref, w_ref, c_ref, s1_ref, s2_ref, q_ref, k_ref, v_ref, glu_ref):
    u =_rms(x_ref[...], g_ref[...], NORM_EPS).astype(BF16)

    def proj(j):
        return jnp.dot(u, w_ref[:, j * A_WIDTH:(j + 1) * A_WIDTH], preferred_element_type=F32)

    ct = jnp.concatenate([c_ref[...]] * A_HEADS, axis=1)
    s1t = jnp.concatenate([s1_ref[...]] * A_HEADS, axis=1)
    s2t = jnp.concatenate([s2_ref[...]] * A_HEADS, axis=1)
    half = ROPE_DIM // 2

    def rope(z):
        return z * ct + pltpu.roll(z, A_WIDTH - half, 1) * s1t + pltpu.roll(z, half, 1) * s2t

    q_ref[...] = rope(proj(0))
    tm = x_ref.shape[0]
    kr = rope(proj(1))
    vv = proj(2)
    for h in range(A_HEADS):
        k_ref[pl.ds(h, tm, stride=A_HEADS), :] = kr[:, h * A_V_DIM:(h + 1) * A_V_DIM]
        v_ref[pl.ds(h, tm, stride=A_HEADS), :] = vv[:, h * A_V_DIM:(h + 1) * A_V_DIM]
    glu_ref[...] = proj(3) * _sigmoid(proj(4))


def _in_proj_ab(h, g, w_bf, tabs, tm):
    M, D = h.shape
    ntab = tabs[0].shape[0] // tm
    tab_spec = pl.BlockSpec((tm, LANES), lambda i: (i % ntab, 0))
    row_out = pl.BlockSpec((tm, A_WIDTH), lambda i: (i, 0))
    kv_out = pl.BlockSpec((tm * A_HEADS, A_V_DIM), lambda i: (i, 0))
    kv_sds = jax.ShapeDtypeStruct((M * A_HEADS, A_V_DIM), F32)
    row_sds = jax.ShapeDtypeStruct((M, A_WIDTH), F32)
    return pl.pallas_call(
        _inproj_kernel,
        grid=(M // tm,),
        in_specs=[pl.BlockSpec((tm, D), lambda i: (i, 0)),
                  pl.BlockSpec((1, D), lambda i: (0, 0)),
                  pl.BlockSpec(w_bf.shape, lambda i: (0, 0)),
                  tab_spec, tab_spec, tab_spec],
        out_specs=[row_out, kv_out, kv_out, row_out],
        out_shape=[row_sds, kv_sds, kv_sds, row_sds],
        compiler_params=_cp("parallel"),
        name="in_proj_ab",
    )(h, g, w_bf, *tabs)


def _rope_tables(pos):
    half = ROPE_DIM // 2
    inv = 1.0 / (ROPE_THETA ** (jnp.arange(0, ROPE_DIM, 2, dtype=F32) / ROPE_DIM))
    ang = pos.astype(F32)[:, None] * inv[None, :]
    cos, sin = jnp.cos(ang), jnp.sin(ang)
    n = pos.shape[0]
    rest = A_QK_DIM - ROPE_DIM
    c = jnp.concatenate([cos, cos, jnp.ones((n, rest), F32)], axis=1)
    s1 = jnp.concatenate([-sin, jnp.zeros((n, A_QK_DIM - half), F32)], axis=1)
    s2 = jnp.concatenate([jnp.zeros((n, half), F32), sin, jnp.zeros((n, rest), F32)], axis=1)
    return tuple(jnp.concatenate([t, t], axis=1) for t in (c, s1, s2))


def _diff_lambda(lam_ref, lam_init):
    lp = lam_ref[...]
    s1 = jnp.sum(lp[0:1] * lp[1:2], axis=-1, keepdims=True)
    s2 = jnp.sum(lp[2:3] * lp[3:4], axis=-1, keepdims=True)
    return jnp.exp(s1) - jnp.exp(s2) + lam_init


def _subln(o, g, lam_init):
    return _rms(o, g, SUBLN_EPS) * (1.0 - lam_init)


def _flash_kernel(qi_ref, ki_ref, lam_ref, sub_ref, q_ref, k_ref, v_ref, o_ref, m_ref, l_ref, acc_ref, *,
                  lam_init, tq, tk, rb):
    pair = pl.program_id(1)
    qi = qi_ref[pair]
    ki = ki_ref[pair]
    nrb = tq // rb
    reps = tk // LANES

    @pl.when(ki == 0)
    def _():
        m_ref[...] = jnp.full(m_ref.shape, -jnp.inf, F32)
        l_ref[...] = jnp.zeros(l_ref.shape, F32)
        acc_ref[...] = jnp.zeros(acc_ref.shape, F32)

    def chain(h, r, masked, kv):
        slab = h * nrb + r
        kb, vb = kv[h]
        q = q_ref[r * rb:(r + 1) * rb, h * A_V_DIM:(h + 1) * A_V_DIM] * (A_QK_DIM ** -0.5 * math.log2(math.e))
        lane = lax.broadcasted_iota(jnp.int32, q.shape, 1)
        q12 = jnp.concatenate([jnp.where(lane < A_QK_DIM, q, 0.0), jnp.where(lane >= A_QK_DIM, q, 0.0)], axis=0)
        s = _mm_nt(q12, kb)
        yield
        if masked is not None:
            row = lax.broadcasted_iota(jnp.int32, s.shape, 0)
            col = lax.broadcasted_iota(jnp.int32, s.shape, 1)
            row = jnp.where(row >= rb, row - rb, row)
            s = jnp.where(col + masked <= row, s, -jnp.inf)
        m_prev = m_ref[slab]
        m_new = jnp.maximum(m_prev, jnp.max(s, axis=1, keepdims=True))
        alpha = jnp.exp2(m_prev - m_new)
        p = jnp.exp2(s - jnp.tile(m_new, (1, reps)))
        pv = jnp.dot(p.astype(BF16), vb, preferred_element_type=F32)
        l_ref[slab] = alpha * l_ref[slab] + pv[:, A_V_DIM:]
        acc_ref[slab] = alpha * acc_ref[slab] + pv[:, :A_V_DIM]
        m_ref[slab] = m_new

    def step(rel):
        ones = jnp.ones((tk, LANES), BF16)
        kv = [(k_ref[pl.ds(h, tk, stride=A_HEADS), :].astype(BF16),
               jnp.concatenate([v_ref[pl.ds(h, tk, stride=A_HEADS), :].astype(BF16), ones], axis=1))
              for h in range(A_HEADS)]
        chains = []
        for h in range(A_HEADS):
            for r in range(nrb):
                off = None if rel is None else rel * tk - r * rb
                if off is not None and off > rb - 1:
                    continue
                if off is not None and off + tk - 1 <= 0:
                    off = None
                chains.append(chain(h, r, off, kv))
        next(chains[0])
        for i, ch in enumerate(chains):
            if i + 1 < len(chains):
                next(chains[i + 1])
            for _ in ch:
                pass

    ratio = tq // tk
    for rel in range(ratio):
        @pl.when(ki == qi * ratio + rel)
        def _(rel=rel):
            step(rel)

    @pl.when(ki < qi * ratio)
    def _():
        step(None)

    @pl.when(ki == (qi + 1) * ratio - 1)
    def _():
        lam = _diff_lambda(lam_ref, lam_init)
        for h in range(A_HEADS):
            for r in range(nrb):
                o = acc_ref[h * nrb + r] / l_ref[h * nrb + r]
                o_ref[r * rb:(r + 1) * rb, h * A_V_DIM:(h + 1) * A_V_DIM] = _subln(
                    o[:rb] - lam * o[rb:], sub_ref[...], lam_init)


def _flash_prompt(q, k, v, lam_p, subln, lam_init, B, T, tq, tk, rb):
    nq, nk, ratio = T // tq, T // tk, tq // tk
    pairs = [(i, j) for i in range(nq) for j in range((i + 1) * ratio)]
    qi_tab = jnp.array([pq for pq, _ in pairs], jnp.int32)
    ki_tab = jnp.array([pk for _, pk in pairs], jnp.int32)
    kern = functools.partial(_flash_kernel, lam_init=lam_init, tq=tq, tk=tk, rb=rb)
    kv_spec = pl.BlockSpec((tk * A_HEADS, A_V_DIM), lambda b, s, qt, kt: (b * nk + kt[s], 0))
    q_spec = pl.BlockSpec((tq, A_WIDTH), lambda b, s, qt, kt: (b * nq + qt[s], 0))
    slabs = A_HEADS * (tq // rb)
    return pl.pallas_call(
        kern,
        grid_spec=pltpu.PrefetchScalarGridSpec(
            num_scalar_prefetch=2,
            grid=(B, len(pairs)),
            in_specs=[pl.BlockSpec(lam_p.shape, lambda b, s, qt, kt: (0, 0)),
                      pl.BlockSpec((1, A_V_DIM), lambda b, s, qt, kt: (0, 0)),
                      q_spec, kv_spec, kv_spec],
            out_specs=q_spec,
            scratch_shapes=[pltpu.VMEM((slabs, 2 * rb, LANES), F32), pltpu.VMEM((slabs, 2 * rb, LANES), F32),
                            pltpu.VMEM((slabs, 2 * rb, A_V_DIM), F32)]),
        out_shape=jax.ShapeDtypeStruct((B * T, A_WIDTH), F32),
        compiler_params=_cp("parallel", "arbitrary"),
        name="diff_attn_prompt",
    )(qi_tab, ki_tab, lam_p, subln, q, k, v)


def _paged_kernel(pt_ref, lam_ref, sub_ref, q_ref, kn_ref, vn_ref, *rest, lam_init, pp, tn):
    k_refs = rest[:pp]
    v_refs = rest[pp:2 * pp]
    o_ref, m_ref, l_ref, acc_ref = rest[2 * pp:]
    j = pl.program_id(1)
    hrows = 2 * SUBLANES

    @pl.when(j == 0)
    def _():
        m_ref[...] = jnp.full(m_ref.shape, -jnp.inf, F32)
        l_ref[...] = jnp.zeros(l_ref.shape, F32)
        acc_ref[...] = jnp.zeros(acc_ref.shape, F32)

    qb = (q_ref[0] * (A_QK_DIM ** -0.5)).astype(BF16)

    def update(keys_of, vals_of, mask=None):
        s = jnp.concatenate([_mm_nt(qb[h * hrows:(h + 1) * hrows], keys_of(h)) for h in range(A_HEADS)], axis=0)
        if mask is not None:
            s = jnp.where(mask(s.shape), s, -jnp.inf)
        m_prev = m_ref[...]
        m_new = jnp.maximum(m_prev, jnp.max(s, axis=1, keepdims=True))
        alpha = jnp.exp(m_prev - m_new)
        p = jnp.exp(s - m_new)
        l_ref[...] = alpha * l_ref[...] + jnp.sum(p, axis=1, keepdims=True)
        pv = jnp.concatenate([_mm(p[h * hrows:(h + 1) * hrows], vals_of(h)) for h in range(A_HEADS)], axis=0)
        acc_ref[...] = alpha * acc_ref[...] + pv
        m_ref[...] = m_new

    rows_of = lambda ref, h: ref[0, 0, pl.ds(h, PAGE_SIZE, stride=A_HEADS), :]
    update(lambda h: jnp.concatenate([rows_of(kr, h) for kr in k_refs], axis=0),
           lambda h: jnp.concatenate([rows_of(vr, h) for vr in v_refs], axis=0))

    @pl.when(j == pl.num_programs(1) - 1)
    def _():
        def causal(shape):
            tok = lax.broadcasted_iota(jnp.int32, shape, 0) % SUBLANES
            col = lax.broadcasted_iota(jnp.int32, shape, 1)
            return (col <= tok) & (col < tn)

        head = lambda ref, h: ref[0, pl.ds(h, SUBLANES, stride=A_HEADS), :]
        update(lambda h: head(kn_ref, h), lambda h: head(vn_ref, h), causal)
        o = acc_ref[...] / l_ref[...]
        lam = _diff_lambda(lam_ref, lam_init)
        for h in range(A_HEADS):
            d = o[h * hrows:h * hrows + SUBLANES] - lam * o[h * hrows + SUBLANES:(h + 1) * hrows]
            o_ref[0, h] = _subln(d, sub_ref[...], lam_init)


def _paged_sample(page_table, q, k_new, v_new, cache_k, cache_v, layer, lam_p, subln, lam_init, Bd, tn):
    n_pages = page_table.shape[1]
    pp = math.gcd(n_pages, PAGES_PER_STEP)
    tpad = SUBLANES - tn
    lane_map = jnp.arange(A_V_DIM) // A_QK_DIM
    keep = (lane_map[None, :] == jnp.arange(2)[:, None]).astype(F32)
    q4 = q.reshape(Bd, tn, A_HEADS, A_V_DIM).transpose(0, 2, 1, 3)
    qm = q4[:, :, None, :, :] * keep[None, None, :, None, :]
    qm = jnp.pad(qm, ((0, 0), (0, 0), (0, 0), (0, tpad), (0, 0))).reshape(Bd, A_HEADS * 2 * SUBLANES, A_V_DIM)
    pad = ((0, 0), (0, tpad * A_HEADS), (0, 0))
    kn = jnp.pad(k_new.reshape(Bd, tn * A_HEADS, A_V_DIM), pad)
    vn = jnp.pad(v_new.reshape(Bd, tn * A_HEADS, A_V_DIM), pad)

    page_rows = PAGE_SIZE * A_HEADS
    cache_k = cache_k.reshape(cache_k.shape[:2] + (page_rows, A_V_DIM))
    cache_v = cache_v.reshape(cache_v.shape[:2] + (page_rows, A_V_DIM))

    def page_spec(i):
        return pl.BlockSpec((1, 1, page_rows, A_V_DIM), lambda b, j, pt: (layer, pt[b, j * pp + i], 0, 0))

    per_b = lambda b, j, pt: (b, 0, 0)
    nrow = A_HEADS * 2 * SUBLANES
    kern = functools.partial(_paged_kernel, lam_init=lam_init, pp=pp, tn=tn)
    out = pl.pallas_call(
        kern,
        grid_spec=pltpu.PrefetchScalarGridSpec(
            num_scalar_prefetch=1,
            grid=(Bd, n_pages // pp),
            in_specs=[pl.BlockSpec(lam_p.shape, lambda b, j, pt: (0, 0)),
                      pl.BlockSpec((1, A_V_DIM), lambda b, j, pt: (0, 0)),
                      pl.BlockSpec((1, nrow, A_V_DIM), per_b),
                      pl.BlockSpec((1, SUBLANES * A_HEADS, A_V_DIM), per_b),
                      pl.BlockSpec((1, SUBLANES * A_HEADS, A_V_DIM), per_b)]
                     + [page_spec(i) for i in range(pp)] * 2,
            out_specs=pl.BlockSpec((1, A_HEADS, SUBLANES, A_V_DIM), lambda b, j, pt: (b, 0, 0, 0)),
            scratch_shapes=[pltpu.VMEM((nrow, 1), F32), pltpu.VMEM((nrow, 1), F32),
                            pltpu.VMEM((nrow, A_V_DIM), F32)]),
        out_shape=jax.ShapeDtypeStruct((Bd, A_HEADS, SUBLANES, A_V_DIM), F32),
        compiler_params=_cp("parallel", "arbitrary"),
        name="diff_attn_paged",
    )(page_table, lam_p, subln, qm, kn, vn, *([cache_k] * pp), *([cache_v] * pp))
    return out[:, :, :tn].transpose(0, 2, 1, 3).reshape(Bd * tn, A_WIDTH)


def _conv_post(y, b_ref, lg_ref, lb_ref):
    y = y + b_ref[...]
    mu = jnp.mean(y, axis=-1, keepdims=True)
    d = y - mu
    var = jnp.mean(d * d, axis=-1, keepdims=True)
    z = d * lax.rsqrt(var + LN_EPS) * lg_ref[...] + lb_ref[...]
    return z * _sigmoid(z)


def _conv_kernel(glu_ref, halo_ref, buf_ref, w_ref, b_ref, lg_ref, lb_ref, o_ref, xp_ref, xs_ref, *, tt, sub):
    j = pl.program_id(1)
    rows = CONV_HALO + tt
    xp_ref[0:CONV_HALO, :] = jnp.where(j == 0, buf_ref[0], halo_ref[...])
    xp_ref[CONV_HALO:, :] = glu_ref[...]
    for ph in range(SUBLANES):
        xs_ref[ph, 0:rows - ph, :] = xp_ref[ph:rows, :]
    lead = CONV_HALO - (CONV_WIDTH - 1)
    w = w_ref[...]
    for r0 in range(0, tt, sub):
        acc = None
        for tap in range(CONV_WIDTH):
            ph, base = (lead + tap) % SUBLANES, (lead + tap) // SUBLANES * SUBLANES
            term = xs_ref[ph, r0 + base:r0 + base + sub, :] * w[tap:tap + 1]
            acc = term if acc is None else acc + term
        o_ref[r0:r0 + sub, :] = _conv_post(acc, b_ref, lg_ref, lb_ref)


def _conv_prompt(glu, buf, w, b, lg, lb, B, T, tt):
    nt = T // tt
    hpt = tt // CONV_HALO
    bufp = jnp.pad(buf, ((0, 0), (CONV_HALO - (CONV_WIDTH - 1), 0), (0, 0)))
    vec = pl.BlockSpec((1, B_WIDTH), lambda bb, j: (0, 0))
    kern = functools.partial(_conv_kernel, tt=tt, sub=min(tt, 64))
    return pl.pallas_call(
        kern,
        grid=(B, nt),
        in_specs=[pl.BlockSpec((tt, B_WIDTH), lambda bb, j: (bb * nt + j, 0)),
                  pl.BlockSpec((CONV_HALO, B_WIDTH), lambda bb, j: (jnp.maximum((bb * nt + j) * hpt - 1, 0), 0)),
                  pl.BlockSpec((1, CONV_HALO, B_WIDTH), lambda bb, j: (bb, 0, 0)),
                  pl.BlockSpec((CONV_WIDTH, B_WIDTH), lambda bb, j: (0, 0)),
                  vec, vec, vec],
        out_specs=pl.BlockSpec((tt, B_WIDTH), lambda bb, j: (bb * nt + j, 0)),
        out_shape=jax.ShapeDtypeStruct((B * T, B_WIDTH), F32),
        scratch_shapes=[pltpu.VMEM((CONV_HALO + tt, B_WIDTH), F32),
                        pltpu.VMEM((SUBLANES, CONV_HALO + tt, B_WIDTH), F32)],
        compiler_params=_cp("parallel", "arbitrary"),
        name="conv_prompt",
    )(glu, glu, bufp, w, b, lg, lb)


def _conv_step_kernel(xp_ref, w_ref, b_ref, lg_ref, lb_ref, o_ref, *, tn):
    w = w_ref[...]
    for t in range(tn):
        acc = xp_ref[t] * w[0:1]
        for tap in range(1, CONV_WIDTH):
            acc = acc + xp_ref[t + tap] * w[tap:tap + 1]
        o_ref[t] = _conv_post(acc, b_ref, lg_ref, lb_ref)


def _conv_sample(glu, buf, w, b, lg, lb, Bd, tn):
    xp = jnp.concatenate([buf, glu.reshape(Bd, tn, B_WIDTH)], axis=1)
    out = pl.pallas_call(
        functools.partial(_conv_step_kernel, tn=tn),
        out_shape=jax.ShapeDtypeStruct((tn, Bd, B_WIDTH), F32),
        compiler_params=pltpu.CompilerParams(vmem_limit_bytes=VMEM_LIMIT),
        name="conv_sample",
    )(xp.transpose(1, 0, 2), w, b, lg, lb)
    return out.transpose(1, 0, 2).reshape(Bd * tn, B_WIDTH), xp[:, -(CONV_WIDTH - 1):]


def _mix_ffn_kernel(*refs, n_in, final):
    x_refs = refs[:n_in]
    h_ref, wo_ref, g_ref, wg_ref, wu_ref, wd_ref = refs[n_in:n_in + 6]
    rest = refs[n_in + 6:]
    h = h_ref[...]
    k0 = 0
    for x_ref in x_refs:
        kw = x_ref.shape[1]
        h = h + jnp.dot(x_ref[...].astype(BF16), wo_ref[k0:k0 + kw, :], preferred_element_type=F32)
        k0 += kw
    f = _rms(h, g_ref[...], NORM_EPS).astype(BF16)
    gate = jnp.dot(f, wg_ref[...], preferred_element_type=F32)
    up = jnp.dot(f, wu_ref[...], preferred_element_type=F32)
    act = (gate * _sigmoid(gate) * up).astype(BF16)
    out = h + jnp.dot(act, wd_ref[...], preferred_element_type=F32)
    if final:
        gf_ref, o_ref, y_ref = rest
        y_ref[...] = _rms(out, gf_ref[...], NORM_EPS)
    else:
        (o_ref,) = rest
    o_ref[...] = out


def _mix_ffn(xs, h, wo, g, wg, wu, wd, tm, g_final=None):
    M, D = h.shape
    final = g_final is not None
    row = pl.BlockSpec((tm, D), lambda i: (i, 0))
    vec = pl.BlockSpec((1, D), lambda i: (0, 0))
    full = lambda a: pl.BlockSpec(a.shape, lambda i: (0, 0), pipeline_mode=pl.Buffered(1))
    ins = list(xs) + [h, wo, g, wg, wu, wd] + ([g_final] if final else [])
    in_specs = ([pl.BlockSpec((tm, x.shape[1]), lambda i: (i, 0)) for x in xs]
                + [row, full(wo), vec, full(wg), full(wu), full(wd)] + ([vec] if final else []))
    sds = jax.ShapeDtypeStruct((M, D), F32)
    return pl.pallas_call(
        functools.partial(_mix_ffn_kernel, n_in=len(xs), final=final),
        grid=(M // tm,),
        in_specs=in_specs,
        out_specs=[row, row] if final else row,
        out_shape=[sds, sds] if final else sds,
        compiler_params=_cp("parallel"),
        name="mix_ffn",
    )(*ins)


def _rwkv_proj_kernel(*refs, has_vmix, seq_tiles, short_t):
    it = iter(refs)
    h_ref, hprev_ref, shift_ref, g_ref, mu_ref = next(it), next(it), next(it), next(it), next(it)
    wr_ref, wk_ref, wv_ref = next(it), next(it), next(it)
    w0_ref, w1_ref, w2_ref = next(it), next(it), next(it)
    a0_ref, a1_ref, a2_ref = next(it), next(it), next(it)
    g1_ref, g2_ref = next(it), next(it)
    if has_vmix:
        v0_ref, v1_ref, v2_ref, vfirst_ref = next(it), next(it), next(it), next(it)
    u_ref, r_ref, k_ref, v_ref, ld_ref, a_ref, gg_ref = (next(it) for _ in range(7))

    i = pl.program_id(0)
    u = _rms(h_ref[...], g_ref[...], NORM_EPS)
    keep = u_ref.shape[1]
    u_ref[0] = u[u.shape[0] - keep:]
    rolled = pltpu.roll(u, 1, 0)
    row = lax.broadcasted_iota(jnp.int32, u.shape, 0)
    if short_t:
        prev = jnp.where(row % short_t == 0, shift_ref[...], rolled)
    else:
        up8 = _rms(hprev_ref[...], g_ref[...], NORM_EPS)
        first = jnp.where(i % seq_tiles == 0, shift_ref[0], up8[SUBLANES - 1:SUBLANES])
        prev = jnp.where(row == 0, first, rolled)
    xx = prev - u
    mu = mu_ref[...]
    xr, xw, xk, xv, xa, xg = (u + xx * mu[j:j + 1] for j in range(6))

    r_ref[...] = _mm(xr, wr_ref[...])
    k_ref[...] = _mm(xk, wk_ref[...])
    v = _mm(xv, wv_ref[...])
    wl = w0_ref[...] + _mm(jnp.tanh(_mm(xw, w1_ref[...])), w2_ref[...])
    z = -wl
    w = -(jnp.maximum(z, 0.0) + jnp.log(1.0 + jnp.exp(-jnp.abs(z)))) - 0.5
    ld_ref[...] = -jnp.exp(w)
    if has_vmix:
        mix = _sigmoid(v0_ref[...] + _mm(_mm(xv, v1_ref[...]), v2_ref[...]))
        v = v + (vfirst_ref[...] - v) * mix
    v_ref[...] = v
    a_ref[...] = _sigmoid(a0_ref[...] + _mm(_mm(xa, a1_ref[...]), a2_ref[...]))
    gg_ref[...] = _mm(_sigmoid(_mm(xg, g1_ref[...])), g2_ref[...])


def _rwkv_proj(h, shift, g, mu, wts, vmix, tm, B, T):
    M, D = h.shape
    short = T < tm
    row = pl.BlockSpec((tm, D), lambda i: (i, 0))
    vec = pl.BlockSpec((1, D), lambda i: (0, 0))
    full = lambda a: pl.BlockSpec(a.shape, lambda i: (0, 0))
    if short:
        seq_tiles = 1
        shift_rows = jnp.zeros((B, T, D), F32).at[:, 0].set(shift).reshape(M, D)
        shift_in, shift_spec = shift_rows, row
        hprev_spec = pl.BlockSpec((SUBLANES, D), lambda i: (0, 0))
    else:
        seq_tiles = T // tm
        shift_in = shift.reshape(B, 1, D)
        shift_spec = pl.BlockSpec((1, 1, D), lambda i: (i // seq_tiles, 0, 0))
        hprev_spec = pl.BlockSpec((SUBLANES, D), lambda i: (jnp.maximum(i * (tm // SUBLANES) - 1, 0), 0))
    ins = [h, h, shift_in, g, mu] + list(wts)
    in_specs = [row, hprev_spec, shift_spec, vec, full(mu)] + [full(a) for a in wts]
    if vmix is not None:
        v0, v1, v2, v_first = vmix
        ins += [v0, v1, v2, v_first]
        in_specs += [vec, full(v1), full(v2), row]
    sds = jax.ShapeDtypeStruct((M, D), F32)
    keep = tm if short else SUBLANES
    kern = functools.partial(_rwkv_proj_kernel, has_vmix=vmix is not None, seq_tiles=seq_tiles,
                             short_t=T if short else 0)
    tail, *seqs = pl.pallas_call(
        kern,
        grid=(M // tm,),
        in_specs=in_specs,
        out_specs=[pl.BlockSpec((1, keep, D), lambda i: (i, 0, 0))] + [row] * 6,
        out_shape=[jax.ShapeDtypeStruct((M // tm, keep, D), F32)] + [sds] * 6,
        compiler_params=_cp("parallel"),
        name="rwkv_proj",
    )(*ins)
    if short:
        last_u = tail.reshape(B, T, D)[:, -1]
    else:
        last_u = tail.reshape(B, seq_tiles, keep, D)[:, -1, -1]
    return last_u, seqs


def _scan_kernel(r_ref, k_ref, v_ref, ld_ref, a_ref, g_ref, kkp_ref, ka_ref, rk_ref, lg_ref, lb_ref, h0_ref,
                 z_ref, hout_ref, bdh_ref, pad_ref, *, nb, nqd):
    c = pl.program_id(2)
    rq = lax.broadcasted_iota(jnp.int32, (QUAD, QUAD), 0)
    cq = lax.broadcasted_iota(jnp.int32, (QUAD, QUAD), 1)
    same_head = (rq // C_HEAD) == (cq // C_HEAD)
    units = [(i, qd) for i in range(nb) for qd in range(nqd)]

    @pl.when(c == 0)
    def _():
        for u, (i, qd) in enumerate(units):
            hc = h0_ref[i, qd]
            bdh_ref[u] = jnp.where(same_head, jnp.concatenate([hc] * 4, axis=1), 0.0)

    block_ones = jnp.where(same_head, 1.0, 0.0).astype(BF16)
    gens = [_scan_chunk(u, i, qd, same_head, r_ref, k_ref, v_ref, ld_ref, a_ref, g_ref, kkp_ref, ka_ref, rk_ref,
                        lg_ref, lb_ref, z_ref, bdh_ref, pad_ref) for u, (i, qd) in enumerate(units)]
    hnews = [None] * len(units)
    answers = [None] * len(units)
    while any(h is None for h in hnews):
        asks = {}
        for u in range(len(units)):
            if hnews[u] is None:
                try:
                    ask = gens[u].send(answers[u])
                    if ask is not None:
                        asks[u] = ask
                except StopIteration as done:
                    hnews[u] = done.value
        answers = [None] * len(units)
        if asks:
            sums = _mm(jnp.concatenate(list(asks.values()), axis=0), block_ones)
            row = 0
            for u, ask in asks.items():
                answers[u] = sums[row:row + ask.shape[0]]
                row += ask.shape[0]

    @pl.when(c == pl.num_programs(2) - 1)
    def _():
        for u, (i, qd) in enumerate(units):
            hc = hnews[u][:, 0:C_HEAD]
            for hh in range(1, 4):
                hc = hc + hnews[u][:, hh * C_HEAD:(hh + 1) * C_HEAD]
            hout_ref[i, qd] = hc


def _scan_chunk(u, i, qd, same_head, r_ref, k_ref, v_ref, ld_ref, a_ref, g_ref, kkp_ref, ka_ref, rk_ref, lg_ref,
                lb_ref, z_ref, bdh_ref, pad_ref):
    C = CHUNK
    lanes = slice(qd * QUAD, (qd + 1) * QUAD)
    t_valid = r_ref.shape[1]

    def bd(x):
        return jnp.where(same_head, jnp.concatenate([x] * 4, axis=0), 0.0)

    def load(j, ref):
        if t_valid == C:
            return ref[i, :, lanes]
        pad_ref[j, u] = jnp.zeros((C, QUAD), F32)
        pad_ref[j, u, 0:t_valid, :] = ref[i, :, lanes]
        return pad_ref[j, u]

    r, k, v, ld, a = (load(j, ref) for j, ref in enumerate((r_ref, k_ref, v_ref, ld_ref, a_ref)))
    kkp, ka, rkp, lg, lb = (ref[:, lanes] for ref in (kkp_ref, ka_ref, rk_ref, lg_ref, lb_ref))

    tt = lax.broadcasted_iota(jnp.int32, (C, C), 0)
    ts = lax.broadcasted_iota(jnp.int32, (C, C), 1)
    tri = jnp.where(ts <= tt, 1.0, 0.0).astype(BF16)
    lc = sum(jnp.dot(tri, p, preferred_element_type=F32) for p in _split2(ld))
    lend = lc[C - 1:C, :]
    gam = jnp.exp(lc)
    gam_prev = jnp.exp(lc - ld)
    gam_inv = jnp.exp(-lc)
    gam_end = jnp.exp(lend - lc)
    yield

    kk = k * kkp
    k2 = k * (1.0 + (a - 1.0) * ka)
    head_sums = yield jnp.concatenate([kk * kk, r * k2 * rkp], axis=0)
    ss, rk_sum = head_sums[:C], head_sums[C:]
    kkn = kk / jnp.maximum(jnp.sqrt(ss), 1e-12)
    bv = kkn * a
    at = -kkn * gam_prev
    rt = r * gam
    lhs = jnp.concatenate([at, rt], axis=0)
    ab = _mm_nt(lhs, bd(bv * gam_inv))
    ak = _mm_nt(lhs, bd(k2 * gam_inv))
    yield
    t_i = lax.broadcasted_iota(jnp.int32, (C, QUAD), 0)
    s_i = lax.broadcasted_iota(jnp.int32, (C, QUAD), 1) % C_HEAD
    strict = s_i < t_i
    incl = s_i <= t_i
    a_ab = jnp.where(strict, ab[:C], 0.0)
    m_rb = jnp.where(incl, ab[C:], 0.0)
    a_ak = jnp.where(strict, ak[:C], 0.0)
    m_rk = jnp.where(incl, ak[C:], 0.0)

    p = a_ab
    tinv = jnp.where(s_i == t_i, 1.0, 0.0) + p
    n_sq = C.bit_length() - 2
    p = _mm(p, bd(p))
    yield
    for jj in range(n_sq):
        w = bd(p)
        if jj < n_sq - 1:
            pt = _mm(jnp.concatenate([p, tinv], axis=0), w)
            p = pt[:C]
            tinv = tinv + pt[C:]
        else:
            tinv = tinv + _mm(tinv, w)
        yield

    avrv = _mm(jnp.concatenate([a_ak, m_rk], axis=0), bd(v))
    x2, rv = avrv[:C], avrv[C:]
    yield
    w1 = _mm(tinv, bd(at))
    w2 = _mm(tinv, bd(x2))
    yield
    y1 = rt + _mm(m_rb, bd(w1))
    y2 = rv + _mm(m_rb, bd(w2))
    yield
    be = bv * gam_end
    ke = k2 * gam_end
    fresh = _mm_tn(jnp.concatenate([w2, v], axis=0), jnp.concatenate([be, ke], axis=0))
    yield
    gmat = bdh_ref[u]
    y = _mm_nt(y1, gmat) + y2
    ut = _mm_nt(gmat, w1)
    yield
    carried = _mm(ut, be)
    hnew = gmat * jnp.exp(lend) + jnp.where(same_head, carried + fresh, 0.0)
    bdh_ref[u] = hnew
    yield

    inv_n = 1.0 / C_HEAD
    mean = (yield y) * inv_n
    d = y - mean
    var = (yield d * d) * inv_n
    yn = d * lax.rsqrt(var + GN_EPS) * lg + lb
    bonus = rk_sum * v
    z_ref[i, :, lanes] = ((yn + bonus)[0:t_valid]) * g_ref[i, :, lanes]
    return hnew


def _rwkv_scan(r, k, v, ld, a, g, kkp, ka, rk, lg, lb, state, B, T):
    D = r.shape[1]
    nq = D // QUAD
    tc = min(T, CHUNK)
    nc = T // tc
    nb = math.gcd(B, SCAN_SEQS)
    nqd = math.gcd(nq, SCAN_QUADS)
    h0 = state.reshape(B, nq, QUAD, C_HEAD)
    seq = pl.BlockSpec((nb, tc, nqd * QUAD), lambda b, q, c: (b, c, q))
    vec = pl.BlockSpec((1, nqd * QUAD), lambda b, q, c: (0, q))
    st = pl.BlockSpec((nb, nqd, QUAD, C_HEAD), lambda b, q, c: (b, q, 0, 0))
    z, hout = pl.pallas_call(
        functools.partial(_scan_kernel, nb=nb, nqd=nqd),
        grid=(B // nb, nq // nqd, nc),
        in_specs=[seq] * 6 + [vec] * 5 + [st],
        out_specs=[seq, st],
        out_shape=[jax.ShapeDtypeStruct((B, T, D), F32), jax.ShapeDtypeStruct((B, nq, QUAD, C_HEAD), F32)],
        scratch_shapes=[pltpu.VMEM((nb * nqd, QUAD, QUAD), F32),
                        pltpu.VMEM((5, nb * nqd, CHUNK, QUAD) if tc < CHUNK else (1, 1, SUBLANES, LANES), F32)],
        compiler_params=_cp("parallel", "parallel", "arbitrary"),
        name="rwkv_scan",
    )(*(s.reshape(B, T, D) for s in (r, k, v, ld, a, g)), kkp, ka, rk, lg, lb, h0)
    new_state = hout.reshape(B, D // C_HEAD, C_HEAD, C_HEAD)
    return z.reshape(B * T, D), new_state


def _row_tile(M, pref):
    return pref if M % pref == 0 else M


def _trunk(x, pos, conv_buf, wkv, shift, p, paged):
    B, T, D = x.shape
    M = B * T
    tm = _row_tile(M, ROW_TILE)
    tm_wide = _row_tile(M, WIDE_ROW_TILE)
    h = x.reshape(M, D)
    depth = p['norm_mix'].shape[0]
    tabs = _rope_tables(pos)
    if paged is not None:
        tabs = tuple(jnp.tile(t, (B, 1)) for t in tabs)
    vec = lambda a: a.reshape(1, -1)
    new_k, new_v, new_conv, new_wkv, new_shift = [], [], [], [], []
    v_first = None
    y = None
    for l in range(depth):
        i = l // 2
        if l % 2 == 0:
            q, k, v, glu = _in_proj_ab(h, vec(p['norm_mix'][l]), p['w_in_ab'][i], tabs, tm_wide)
            lam_init = 0.8 - 0.6 * math.exp(-0.3 * l)
            lam_p, sub = p['diff_lambda'][i], vec(p['subln'][i])
            cw, cb = p['conv_w'][i], vec(p['conv_b'][i])
            clg, clb = vec(p['conv_ln_g'][i]), vec(p['conv_ln_b'][i])
            if paged is None:
                tk = _row_tile(T, FLASH_TK)
                tq = _row_tile(T, FLASH_TQ)
                o = _flash_prompt(q, k, v, lam_p, sub, lam_init, B, T, tq, tk, min(tq, FLASH_ROWS))
                cm = _conv_prompt(glu, conv_buf[i], cw, cb, clg, clb, B, T, _row_tile(T, 256))
                buf = glu.reshape(B, T, B_WIDTH)[:, -(CONV_WIDTH - 1):]
            else:
                page_table, cache_k, cache_v = paged
                o = _paged_sample(page_table, q, k, v, cache_k, cache_v, i, lam_p, sub, lam_init, B, T)
                cm, buf = _conv_sample(glu, conv_buf[i], cw, cb, clg, clb, B, T)
            mix_in, w_mix = [o, cm], p['w_out_ab'][i]
            new_k.append(k.reshape(B, T, A_HEADS, 2 * A_QK_DIM))
            new_v.append(v.reshape(B, T, A_HEADS, A_V_DIM))
            new_conv.append(buf)
        else:
            wts = [p['rwkv_wr'][i], p['rwkv_wk'][i], p['rwkv_wv'][i],
                   vec(p['rwkv_w0'][i]), p['rwkv_w1'][i], p['rwkv_w2'][i],
                   vec(p['rwkv_a0'][i]), p['rwkv_a1'][i], p['rwkv_a2'][i],
                   p['rwkv_g1'][i], p['rwkv_g2'][i]]
            vmix = None
            if v_first is not None:
                vmix = (vec(p['rwkv_v0'][i - 1]), p['rwkv_v1'][i - 1], p['rwkv_v2'][i - 1], v_first)
            last_u, seqs = _rwkv_proj(h, shift[i], vec(p['norm_mix'][l]), p['rwkv_mu'][i], wts, vmix, tm, B, T)
            if v_first is None:
                v_first = seqs[2]
            z, s_new = _rwkv_scan(*seqs, vec(p['rwkv_kk'][i]), vec(p['rwkv_ka'][i]), vec(p['rwkv_rk'][i]),
                                  vec(p['rwkv_lnx_g'][i]), vec(p['rwkv_lnx_b'][i]), wkv[i], B, T)
            mix_in, w_mix = [z], p['rwkv_wo'][i]
            new_wkv.append(s_new)
            new_shift.append(last_u)
        last = l == depth - 1
        res = _mix_ffn(mix_in, h, w_mix, vec(p['norm_ffn'][l]), p['w_gate'][l], p['w_up'][l], p['w_down'][l], tm_wide,
                       vec(p['norm_final']) if last else None)
        if last:
            h, y = res
        else:
            h = res
    return (y.reshape(B, T, D), jnp.stack(new_k), jnp.stack(new_v), jnp.stack(new_conv), jnp.stack(new_wkv),
            jnp.stack(new_shift))


_MATMUL_WEIGHTS = ('w_in_ab', 'w_out_ab', 'rwkv_wr', 'rwkv_wk', 'rwkv_wv', 'rwkv_w1', 'rwkv_w2', 'rwkv_a1',
                   'rwkv_a2', 'rwkv_v1', 'rwkv_v2', 'rwkv_g1', 'rwkv_g2', 'rwkv_wo', 'w_gate', 'w_up', 'w_down')


def kernel(x_prompt, x_sample, cache_k, cache_v, state_conv, state_wkv, state_shift, page_table, norm_mix, norm_ffn, norm_final, w_in_ab, diff_lambda, subln, conv_w, conv_b, conv_ln_g, conv_ln_b, w_out_ab, rwkv_mu, rwkv_wr, rwkv_wk, rwkv_wv, rwkv_w0, rwkv_w1, rwkv_w2, rwkv_a0, rwkv_a1, rwkv_a2, rwkv_v0, rwkv_v1, rwkv_v2, rwkv_g1, rwkv_g2, rwkv_kk, rwkv_ka, rwkv_rk, rwkv_lnx_g, rwkv_lnx_b, rwkv_wo, w_gate, w_up, w_down):
    p = dict(norm_mix=norm_mix, norm_ffn=norm_ffn, norm_final=norm_final, w_in_ab=w_in_ab,
             diff_lambda=diff_lambda, subln=subln, conv_w=conv_w, conv_b=conv_b, conv_ln_g=conv_ln_g,
             conv_ln_b=conv_ln_b, w_out_ab=w_out_ab, rwkv_mu=rwkv_mu, rwkv_wr=rwkv_wr, rwkv_wk=rwkv_wk,
             rwkv_wv=rwkv_wv, rwkv_w0=rwkv_w0, rwkv_w1=rwkv_w1, rwkv_w2=rwkv_w2, rwkv_a0=rwkv_a0,
             rwkv_a1=rwkv_a1, rwkv_a2=rwkv_a2, rwkv_v0=rwkv_v0, rwkv_v1=rwkv_v1, rwkv_v2=rwkv_v2,
             rwkv_g1=rwkv_g1, rwkv_g2=rwkv_g2, rwkv_kk=rwkv_kk, rwkv_ka=rwkv_ka, rwkv_rk=rwkv_rk,
             rwkv_lnx_g=rwkv_lnx_g, rwkv_lnx_b=rwkv_lnx_b, rwkv_wo=rwkv_wo, w_gate=w_gate, w_up=w_up,
             w_down=w_down)
    for name in _MATMUL_WEIGHTS:
        p[name] = p[name].astype(BF16)

    Bp, Tp, D = x_prompt.shape
    n_a, n_c = state_conv.shape[0], state_wkv.shape[0]
    zero_conv = jnp.zeros((n_a, Bp, CONV_WIDTH - 1, B_WIDTH), F32)
    zero_wkv = jnp.zeros((n_c, Bp) + state_wkv.shape[2:], F32)
    zero_shift = jnp.zeros((n_c, Bp, D), F32)
    outs_p = _trunk(x_prompt, jnp.arange(Tp), zero_conv, zero_wkv, zero_shift, p, None)

    past = page_table.shape[1] * PAGE_SIZE
    outs_s = _trunk(x_sample, past + jnp.arange(x_sample.shape[1]), state_conv, state_wkv, state_shift, p,
                    (page_table, cache_k, cache_v))
    return (outs_p[0], outs_s[0]) + tuple(outs_p[1:]) + tuple(outs_s[1:])
```

```python
import functools
import math

import jax
import jax.numpy as jnp
from jax import lax
from jax.experimental import pallas as pl
from jax.experimental.pallas import tpu as pltpu

F32 = jnp.float32
BF16 = jnp.bfloat16

NORM_EPS = 1e-6
SUBLN_EPS = 1e-5
LN_EPS = 1e-5
GN_EPS = 64e-5
ROPE_THETA = 500000.0
ROPE_DIM = 16

A_HEADS = 4
A_QK_DIM = 64
A_V_DIM = 128
A_WIDTH = 512
B_WIDTH = 512
CONV_WIDTH = 31
C_HEAD = 64
PAGE_SIZE = 128

LANES = 128
SUBLANES = 8
QUAD = 4 * C_HEAD
CHUNK = 64
ROW_TILE = 256
WIDE_ROW_TILE = 512
FLASH_TQ = 1024
FLASH_TK = 1024
FLASH_ROWS = 512
PAGES_PER_STEP = 16
SCAN_SEQS = 2
SCAN_QUADS = 4
CONV_HALO = 32
VMEM_LIMIT = 56 * 1024 * 1024


def _cp(*sem):
    return pltpu.CompilerParams(dimension_semantics=sem, vmem_limit_bytes=VMEM_LIMIT)


def _mm(a, b):
    return jnp.dot(a.astype(BF16), b.astype(BF16), preferred_element_type=F32)


def _mm_nt(a, b):
    return lax.dot_general(a.astype(BF16), b.astype(BF16), (((1,), (1,)), ((), ())), preferred_element_type=F32)


def _mm_tn(a, b):
    return lax.dot_general(a.astype(BF16), b.astype(BF16), (((0,), (0,)), ((), ())), preferred_element_type=F32)


def _split2(x):
    hi = x.astype(BF16)
    lo = (x - hi.astype(F32)).astype(BF16)
    return hi, lo


def _sigmoid(x):
    return 1.0 / (1.0 + jnp.exp(-x))


def _rms(x, g, eps):
    ms = jnp.mean(x * x, axis=-1, keepdims=True)
    return x * lax.rsqrt(ms + eps) * g


def _inproj_kernel(x_ref, g_ref, w_ref, c_ref, s1_ref, s2_ref, *rest):
    q_ref, k_ref, v_ref, glu_ref = rest[-4:]
    u = _rms(x_ref[...], g_ref[...], NORM_EPS).astype(BF16)

    def proj(j):
        return jnp.dot(u, w_ref[:, j * A_WIDTH:(j + 1) * A_WIDTH], preferred_element_type=F32)

    ct = jnp.concatenate([c_ref[...]] * A_HEADS, axis=1)
    s1t = jnp.concatenate([s1_ref[...]] * A_HEADS, axis=1)
    s2t = jnp.concatenate([s2_ref[...]] * A_HEADS, axis=1)
    half = ROPE_DIM // 2

    def rope(z):
        return z * ct + pltpu.roll(z, A_WIDTH - half, 1) * s1t + pltpu.roll(z, half, 1) * s2t

    q_ref[...] = rope(proj(0))
    tm = x_ref.shape[0]
    n_prev = k_ref.shape[0] - 1
    if n_prev:
        k_ref[0:n_prev] = rest[0][...]
        v_ref[0:n_prev] = rest[1][...]
    kr = rope(proj(1))
    vv = proj(2)
    for h in range(A_HEADS):
        k_ref[n_prev, pl.ds(h, tm, stride=A_HEADS), :] = kr[:, h * A_V_DIM:(h + 1) * A_V_DIM]
        v_ref[n_prev, pl.ds(h, tm, stride=A_HEADS), :] = vv[:, h * A_V_DIM:(h + 1) * A_V_DIM]
    glu_ref[...] = proj(3) * _sigmoid(proj(4))


def _in_proj_ab(h, g, w_bf, tabs, tm, kv_prev):
    M, D = h.shape
    ntab = tabs[0].shape[0] // tm
    n_prev = 0 if kv_prev is None else kv_prev[0].shape[0]
    tab_spec = pl.BlockSpec((tm, LANES), lambda i: (i % ntab, 0))
    row_out = pl.BlockSpec((tm, A_WIDTH), lambda i: (i, 0))
    kv_out = pl.BlockSpec((n_prev + 1, tm * A_HEADS, A_V_DIM), lambda i: (0, i, 0))
    kv_sds = jax.ShapeDtypeStruct((n_prev + 1, M * A_HEADS, A_V_DIM), F32)
    row_sds = jax.ShapeDtypeStruct((M, A_WIDTH), F32)
    ins = [h, g, w_bf, *tabs]
    in_specs = [pl.BlockSpec((tm, D), lambda i: (i, 0)),
                pl.BlockSpec((1, D), lambda i: (0, 0)),
                pl.BlockSpec(w_bf.shape, lambda i: (0, 0)),
                tab_spec, tab_spec, tab_spec]
    if n_prev:
        ins += list(kv_prev)
        in_specs += [pl.BlockSpec((n_prev, tm * A_HEADS, A_V_DIM), lambda i: (0, i, 0))] * 2
    return pl.pallas_call(
        _inproj_kernel,
        grid=(M // tm,),
        in_specs=in_specs,
        out_specs=[row_out, kv_out, kv_out, row_out],
        out_shape=[row_sds, kv_sds, kv_sds, row_sds],
        compiler_params=_cp("parallel"),
        name="in_proj_ab",
    )(*ins)


def _rope_tables(pos):
    half = ROPE_DIM // 2
    inv = 1.0 / (ROPE_THETA ** (jnp.arange(0, ROPE_DIM, 2, dtype=F32) / ROPE_DIM))
    ang = pos.astype(F32)[:, None] * inv[None, :]
    cos, sin = jnp.cos(ang), jnp.sin(ang)
    n = pos.shape[0]
    rest = A_QK_DIM - ROPE_DIM
    c = jnp.concatenate([cos, cos, jnp.ones((n, rest), F32)], axis=1)
    s1 = jnp.concatenate([-sin, jnp.zeros((n, A_QK_DIM - half), F32)], axis=1)
    s2 = jnp.concatenate([jnp.zeros((n, half), F32), sin, jnp.zeros((n, rest), F32)], axis=1)
    return tuple(jnp.concatenate([t, t], axis=1) for t in (c, s1, s2))


def _diff_lambda(lam_ref, lam_init):
    lp = lam_ref[...]
    s1 = jnp.sum(lp[0:1] * lp[1:2], axis=-1, keepdims=True)
    s2 = jnp.sum(lp[2:3] * lp[3:4], axis=-1, keepdims=True)
    return jnp.exp(s1) - jnp.exp(s2) + lam_init


def _subln(o, g, lam_init):
    return _rms(o, g, SUBLN_EPS) * (1.0 - lam_init)


def _flash_kernel(qi_ref, ki_ref, lam_ref, sub_ref, q_ref, k_ref, v_ref, o_ref, m_ref, l_ref, acc_ref, *,
                  lam_init, tq, tk, rb):
    pair = pl.program_id(1)
    qi = qi_ref[pair]
    ki = ki_ref[pair]
    nrb = tq // rb
    reps = tk // LANES

    @pl.when(ki == 0)
    def _():
        m_ref[...] = jnp.full(m_ref.shape, -jnp.inf, F32)
        l_ref[...] = jnp.zeros(l_ref.shape, F32)
        acc_ref[...] = jnp.zeros(acc_ref.shape, F32)

    def chain(h, r, masked, kv):
        slab = h * nrb + r
        kb, vb = kv[h]
        q = q_ref[r * rb:(r + 1) * rb, h * A_V_DIM:(h + 1) * A_V_DIM] * (A_QK_DIM ** -0.5 * math.log2(math.e))
        lane = lax.broadcasted_iota(jnp.int32, q.shape, 1)
        q12 = jnp.concatenate([jnp.where(lane < A_QK_DIM, q, 0.0), jnp.where(lane >= A_QK_DIM, q, 0.0)], axis=0)
        s = _mm_nt(q12, kb)
        yield
        if masked is not None:
            row = lax.broadcasted_iota(jnp.int32, s.shape, 0)
            col = lax.broadcasted_iota(jnp.int32, s.shape, 1)
            row = jnp.where(row >= rb, row - rb, row)
            s = jnp.where(col + masked <= row, s, -jnp.inf)
        m_prev = m_ref[slab]
        m_new = jnp.maximum(m_prev, jnp.max(s, axis=1, keepdims=True))
        alpha = jnp.exp2(m_prev - m_new)
        p = jnp.exp2(s - jnp.tile(m_new, (1, reps)))
        pv = jnp.dot(p.astype(BF16), vb, preferred_element_type=F32)
        l_ref[slab] = alpha * l_ref[slab] + pv[:, A_V_DIM:]
        acc_ref[slab] = alpha * acc_ref[slab] + pv[:, :A_V_DIM]
        m_ref[slab] = m_new

    def step(rel):
        ones = jnp.ones((tk, LANES), BF16)
        kv = [(k_ref[0, pl.ds(h, tk, stride=A_HEADS), :].astype(BF16),
               jnp.concatenate([v_ref[0, pl.ds(h, tk, stride=A_HEADS), :].astype(BF16), ones], axis=1))
              for h in range(A_HEADS)]
        chains = []
        for h in range(A_HEADS):
            for r in range(nrb):
                off = None if rel is None else rel * tk - r * rb
                if off is not None and off > rb - 1:
                    continue
                if off is not None and off + tk - 1 <= 0:
                    off = None
                chains.append(chain(h, r, off, kv))
        next(chains[0])
        for i, ch in enumerate(chains):
            if i + 1 < len(chains):
                next(chains[i + 1])
            for _ in ch:
                pass

    ratio = tq // tk
    for rel in range(ratio):
        @pl.when(ki == qi * ratio + rel)
        def _(rel=rel):
            step(rel)

    @pl.when(ki < qi * ratio)
    def _():
        step(None)

    @pl.when(ki == (qi + 1) * ratio - 1)
    def _():
        lam = _diff_lambda(lam_ref, lam_init)
        for h in range(A_HEADS):
            for r in range(nrb):
                o = acc_ref[h * nrb + r] / l_ref[h * nrb + r]
                o_ref[r * rb:(r + 1) * rb, h * A_V_DIM:(h + 1) * A_V_DIM] = _subln(
                    o[:rb] - lam * o[rb:], sub_ref[...], lam_init)


def _flash_prompt(q, k, v, lam_p, subln, lam_init, B, T, tq, tk, rb):
    layer = k.shape[0] - 1
    nq, nk, ratio = T // tq, T // tk, tq // tk
    pairs = [(i, j) for i in range(nq) for j in range((i + 1) * ratio)]
    qi_tab = jnp.array([pq for pq, _ in pairs], jnp.int32)
    ki_tab = jnp.array([pk for _, pk in pairs], jnp.int32)
    kern = functools.partial(_flash_kernel, lam_init=lam_init, tq=tq, tk=tk, rb=rb)
    kv_spec = pl.BlockSpec((1, tk * A_HEADS, A_V_DIM), lambda b, s, qt, kt: (layer, b * nk + kt[s], 0))
    q_spec = pl.BlockSpec((tq, A_WIDTH), lambda b, s, qt, kt: (b * nq + qt[s], 0))
    slabs = A_HEADS * (tq // rb)
    return pl.pallas_call(
        kern,
        grid_spec=pltpu.PrefetchScalarGridSpec(
            num_scalar_prefetch=2,
            grid=(B, len(pairs)),
            in_specs=[pl.BlockSpec(lam_p.shape, lambda b, s, qt, kt: (0, 0)),
                      pl.BlockSpec((1, A_V_DIM), lambda b, s, qt, kt: (0, 0)),
                      q_spec, kv_spec, kv_spec],
            out_specs=q_spec,
            scratch_shapes=[pltpu.VMEM((slabs, 2 * rb, LANES), F32), pltpu.VMEM((slabs, 2 * rb, LANES), F32),
                            pltpu.VMEM((slabs, 2 * rb, A_V_DIM), F32)]),
        out_shape=jax.ShapeDtypeStruct((B * T, A_WIDTH), F32),
        compiler_params=_cp("parallel", "arbitrary"),
        name="diff_attn_prompt",
    )(qi_tab, ki_tab, lam_p, subln, q, k, v)


def _paged_kernel(pt_ref, lam_ref, sub_ref, q_ref, kn_ref, vn_ref, *rest, lam_init, pp, tn):
    k_refs = rest[:pp]
    v_refs = rest[pp:2 * pp]
    o_ref, m_ref, l_ref, acc_ref = rest[2 * pp:]
    j = pl.program_id(1)
    hrows = 2 * SUBLANES

    @pl.when(j == 0)
    def _():
        m_ref[...] = jnp.full(m_ref.shape, -jnp.inf, F32)
        l_ref[...] = jnp.zeros(l_ref.shape, F32)
        acc_ref[...] = jnp.zeros(acc_ref.shape, F32)

    qb = (q_ref[0] * (A_QK_DIM ** -0.5)).astype(BF16)

    def scores(keys_of):
        return jnp.concatenate([_mm_nt(qb[h * hrows:(h + 1) * hrows], keys_of(h)) for h in range(A_HEADS)], axis=0)

    def update(s, vals_of, mask=None):
        if mask is not None:
            s = jnp.where(mask(s.shape), s, -jnp.inf)
        m_prev = m_ref[...]
        m_new = jnp.maximum(m_prev, jnp.max(s, axis=1, keepdims=True))
        alpha = jnp.exp(m_prev - m_new)
        p = jnp.exp(s - m_new)
        l_ref[...] = alpha * l_ref[...] + jnp.sum(p, axis=1, keepdims=True)
        pv = jnp.concatenate([_mm(p[h * hrows:(h + 1) * hrows], vals_of(h)) for h in range(A_HEADS)], axis=0)
        acc_ref[...] = alpha * acc_ref[...] + pv
        m_ref[...] = m_new

    rows_of = lambda ref, h: ref[0, 0, pl.ds(h, PAGE_SIZE, stride=A_HEADS), :]
    gather = lambda refs: (lambda h: jnp.concatenate([rows_of(r, h) for r in refs], axis=0))
    half = max(pp // 2, 1)
    groups = [(k_refs[lo:lo + half], v_refs[lo:lo + half]) for lo in range(0, pp, half)]
    all_scores = [scores(gather(kg)) for kg, _ in groups]
    for s_g, (_, vg) in zip(all_scores, groups):
        update(s_g, gather(vg))

    @pl.when(j == pl.num_programs(1) - 1)
    def _():
        def causal(shape):
            tok = lax.broadcasted_iota(jnp.int32, shape, 0) % SUBLANES
            col = lax.broadcasted_iota(jnp.int32, shape, 1)
            return (col <= tok) & (col < tn)

        head = lambda ref, h: ref[0, pl.ds(h, SUBLANES, stride=A_HEADS), :]
        update(scores(lambda h: head(kn_ref, h)), lambda h: head(vn_ref, h), causal)
        o = acc_ref[...] / l_ref[...]
        lam = _diff_lambda(lam_ref, lam_init)
        for h in range(A_HEADS):
            d = o[h * hrows:h * hrows + SUBLANES] - lam * o[h * hrows + SUBLANES:(h + 1) * hrows]
            o_ref[0, h] = _subln(d, sub_ref[...], lam_init)


def _paged_sample(page_table, q, k_new, v_new, cache_k, cache_v, layer, lam_p, subln, lam_init, Bd, tn):
    n_pages = page_table.shape[1]
    pp = math.gcd(n_pages, PAGES_PER_STEP)
    tpad = SUBLANES - tn
    lane_map = jnp.arange(A_V_DIM) // A_QK_DIM
    keep = (lane_map[None, :] == jnp.arange(2)[:, None]).astype(F32)
    q4 = q.reshape(Bd, tn, A_HEADS, A_V_DIM).transpose(0, 2, 1, 3)
    qm = q4[:, :, None, :, :] * keep[None, None, :, None, :]
    qm = jnp.pad(qm, ((0, 0), (0, 0), (0, 0), (0, tpad), (0, 0))).reshape(Bd, A_HEADS * 2 * SUBLANES, A_V_DIM)
    pad = ((0, 0), (0, tpad * A_HEADS), (0, 0))
    kn = jnp.pad(k_new.reshape(Bd, tn * A_HEADS, A_V_DIM), pad)
    vn = jnp.pad(v_new.reshape(Bd, tn * A_HEADS, A_V_DIM), pad)

    page_rows = PAGE_SIZE * A_HEADS
    cache_k = cache_k.reshape(cache_k.shape[:2] + (page_rows, A_V_DIM))
    cache_v = cache_v.reshape(cache_v.shape[:2] + (page_rows, A_V_DIM))

    def page_spec(i):
        return pl.BlockSpec((1, 1, page_rows, A_V_DIM), lambda b, j, pt: (layer, pt[b, j * pp + i], 0, 0))

    per_b = lambda b, j, pt: (b, 0, 0)
    nrow = A_HEADS * 2 * SUBLANES
    kern = functools.partial(_paged_kernel, lam_init=lam_init, pp=pp, tn=tn)
    out = pl.pallas_call(
        kern,
        grid_spec=pltpu.PrefetchScalarGridSpec(
            num_scalar_prefetch=1,
            grid=(Bd, n_pages // pp),
            in_specs=[pl.BlockSpec(lam_p.shape, lambda b, j, pt: (0, 0)),
                      pl.BlockSpec((1, A_V_DIM), lambda b, j, pt: (0, 0)),
                      pl.BlockSpec((1, nrow, A_V_DIM), per_b),
                      pl.BlockSpec((1, SUBLANES * A_HEADS, A_V_DIM), per_b),
                      pl.BlockSpec((1, SUBLANES * A_HEADS, A_V_DIM), per_b)]
                     + [page_spec(i) for i in range(pp)] * 2,
            out_specs=pl.BlockSpec((1, A_HEADS, SUBLANES, A_V_DIM), lambda b, j, pt: (b, 0, 0, 0)),
            scratch_shapes=[pltpu.VMEM((nrow, 1), F32), pltpu.VMEM((nrow, 1), F32),
                            pltpu.VMEM((nrow, A_V_DIM), F32)]),
        out_shape=jax.ShapeDtypeStruct((Bd, A_HEADS, SUBLANES, A_V_DIM), F32),
        compiler_params=_cp("parallel", "arbitrary"),
        name="diff_attn_paged",
    )(page_table, lam_p, subln, qm, kn, vn, *([cache_k] * pp), *([cache_v] * pp))
    return out[:, :, :tn].transpose(0, 2, 1, 3).reshape(Bd * tn, A_WIDTH)


def _conv_post(y, b_ref, lg_ref, lb_ref):
    y = y + b_ref[...]
    mu = jnp.mean(y, axis=-1, keepdims=True)
    d = y - mu
    var = jnp.mean(d * d, axis=-1, keepdims=True)
    z = d * lax.rsqrt(var + LN_EPS) * lg_ref[...] + lb_ref[...]
    return z * _sigmoid(z)


def _conv_kernel(glu_ref, halo_ref, buf_ref, w_ref, b_ref, lg_ref, lb_ref, o_ref, xp_ref, xs_ref, *, tt, sub):
    j = pl.program_id(1)
    rows = CONV_HALO + tt
    xp_ref[0:CONV_HALO, :] = jnp.where(j == 0, buf_ref[0], halo_ref[...])
    xp_ref[CONV_HALO:, :] = glu_ref[...]
    for ph in range(SUBLANES):
        xs_ref[ph, 0:rows - ph, :] = xp_ref[ph:rows, :]
    lead = CONV_HALO - (CONV_WIDTH - 1)
    w = w_ref[...]
    for r0 in range(0, tt, sub):
        acc = None
        for tap in range(CONV_WIDTH):
            ph, base = (lead + tap) % SUBLANES, (lead + tap) // SUBLANES * SUBLANES
            term = xs_ref[ph, r0 + base:r0 + base + sub, :] * w[tap:tap + 1]
            acc = term if acc is None else acc + term
        o_ref[r0:r0 + sub, :] = _conv_post(acc, b_ref, lg_ref, lb_ref)


def _conv_prompt(glu, buf, w, b, lg, lb, B, T, tt):
    nt = T // tt
    hpt = tt // CONV_HALO
    bufp = jnp.pad(buf, ((0, 0), (CONV_HALO - (CONV_WIDTH - 1), 0), (0, 0)))
    vec = pl.BlockSpec((1, B_WIDTH), lambda bb, j: (0, 0))
    kern = functools.partial(_conv_kernel, tt=tt, sub=min(tt, 64))
    return pl.pallas_call(
        kern,
        grid=(B, nt),
        in_specs=[pl.BlockSpec((tt, B_WIDTH), lambda bb, j: (bb * nt + j, 0)),
                  pl.BlockSpec((CONV_HALO, B_WIDTH), lambda bb, j: (jnp.maximum((bb * nt + j) * hpt - 1, 0), 0)),
                  pl.BlockSpec((1, CONV_HALO, B_WIDTH), lambda bb, j: (bb, 0, 0)),
                  pl.BlockSpec((CONV_WIDTH, B_WIDTH), lambda bb, j: (0, 0)),
                  vec, vec, vec],
        out_specs=pl.BlockSpec((tt, B_WIDTH), lambda bb, j: (bb * nt + j, 0)),
        out_shape=jax.ShapeDtypeStruct((B * T, B_WIDTH), F32),
        scratch_shapes=[pltpu.VMEM((CONV_HALO + tt, B_WIDTH), F32),
                        pltpu.VMEM((SUBLANES, CONV_HALO + tt, B_WIDTH), F32)],
        compiler_params=_cp("parallel", "arbitrary"),
        name="conv_prompt",
    )(glu, glu, bufp, w, b, lg, lb)


def _conv_step_kernel(xp_ref, w_ref, b_ref, lg_ref, lb_ref, o_ref, *, tn):
    w = w_ref[...]
    for t in range(tn):
        acc = xp_ref[t] * w[0:1]
        for tap in range(1, CONV_WIDTH):
            acc = acc + xp_ref[t + tap] * w[tap:tap + 1]
        o_ref[t] = _conv_post(acc, b_ref, lg_ref, lb_ref)


def _conv_sample(glu, buf, w, b, lg, lb, Bd, tn):
    xp = jnp.concatenate([buf, glu.reshape(Bd, tn, B_WIDTH)], axis=1)
    out = pl.pallas_call(
        functools.partial(_conv_step_kernel, tn=tn),
        out_shape=jax.ShapeDtypeStruct((tn, Bd, B_WIDTH), F32),
        compiler_params=pltpu.CompilerParams(vmem_limit_bytes=VMEM_LIMIT),
        name="conv_sample",
    )(xp.transpose(1, 0, 2), w, b, lg, lb)
    return out.transpose(1, 0, 2).reshape(Bd * tn, B_WIDTH), xp[:, -(CONV_WIDTH - 1):]


def _mix_ffn_kernel(*refs, n_in, final):
    x_refs = refs[:n_in]
    h_ref, wo_ref, g_ref, wg_ref, wu_ref, wd_ref = refs[n_in:n_in + 6]
    rest = refs[n_in + 6:]
    h = h_ref[...]
    k0 = 0
    for x_ref in x_refs:
        kw = x_ref.shape[1]
        h = h + jnp.dot(x_ref[...].astype(BF16), wo_ref[k0:k0 + kw, :], preferred_element_type=F32)
        k0 += kw
    f = _rms(h, g_ref[...], NORM_EPS).astype(BF16)
    gate = jnp.dot(f, wg_ref[...], preferred_element_type=F32)
    up = jnp.dot(f, wu_ref[...], preferred_element_type=F32)
    act = (gate * _sigmoid(gate) * up).astype(BF16)
    out = h + jnp.dot(act, wd_ref[...], preferred_element_type=F32)
    if final:
        gf_ref, o_ref, y_ref = rest
        y_ref[...] = _rms(out, gf_ref[...], NORM_EPS)
    else:
        (o_ref,) = rest
    o_ref[...] = out


def _mix_ffn(xs, h, wo, g, wg, wu, wd, tm, g_final=None):
    M, D = h.shape
    final = g_final is not None
    row = pl.BlockSpec((tm, D), lambda i: (i, 0))
    vec = pl.BlockSpec((1, D), lambda i: (0, 0))
    full = lambda a: pl.BlockSpec(a.shape, lambda i: (0, 0), pipeline_mode=pl.Buffered(1))
    ins = list(xs) + [h, wo, g, wg, wu, wd] + ([g_final] if final else [])
    in_specs = ([pl.BlockSpec((tm, x.shape[1]), lambda i: (i, 0)) for x in xs]
                + [row, full(wo), vec, full(wg), full(wu), full(wd)] + ([vec] if final else []))
    sds = jax.ShapeDtypeStruct((M, D), F32)
    return pl.pallas_call(
        functools.partial(_mix_ffn_kernel, n_in=len(xs), final=final),
        grid=(M // tm,),
        in_specs=in_specs,
        out_specs=[row, row] if final else row,
        out_shape=[sds, sds] if final else sds,
        compiler_params=_cp("parallel"),
        name="mix_ffn",
    )(*ins)


def _rwkv_proj_kernel(*refs, has_vmix, seq_tiles, short_t):
    it = iter(refs)
    h_ref, hprev_ref, shift_ref, g_ref, mu_ref = next(it), next(it), next(it), next(it), next(it)
    wr_ref, wk_ref, wv_ref = next(it), next(it), next(it)
    w0_ref, w1_ref, w2_ref = next(it), next(it), next(it)
    a0_ref, a1_ref, a2_ref = next(it), next(it), next(it)
    g1_ref, g2_ref = next(it), next(it)
    if has_vmix:
        v0_ref, v1_ref, v2_ref, vfirst_ref = next(it), next(it), next(it), next(it)
    u_ref, r_ref, k_ref, v_ref, ld_ref, a_ref, gg_ref = (next(it) for _ in range(7))

    i = pl.program_id(0)
    u = _rms(h_ref[...], g_ref[...], NORM_EPS)
    keep = u_ref.shape[1]
    u_ref[0] = u[u.shape[0] - keep:]
    rolled = pltpu.roll(u, 1, 0)
    row = lax.broadcasted_iota(jnp.int32, u.shape, 0)
    if short_t:
        prev = jnp.where(row % short_t == 0, shift_ref[...], rolled)
    else:
        up8 = _rms(hprev_ref[...], g_ref[...], NORM_EPS)
        first = jnp.where(i % seq_tiles == 0, shift_ref[0], up8[SUBLANES - 1:SUBLANES])
        prev = jnp.where(row == 0, first, rolled)
    xx = prev - u
    mu = mu_ref[...]
    xr, xw, xk, xv, xa, xg = (u + xx * mu[j:j + 1] for j in range(6))

    r_ref[...] = _mm(xr, wr_ref[...])
    k_ref[...] = _mm(xk, wk_ref[...])
    v = _mm(xv, wv_ref[...])
    wl = w0_ref[...] + _mm(jnp.tanh(_mm(xw, w1_ref[...])), w2_ref[...])
    z = -wl
    w = -(jnp.maximum(z, 0.0) + jnp.log(1.0 + jnp.exp(-jnp.abs(z)))) - 0.5
    ld_ref[...] = -jnp.exp(w)
    if has_vmix:
        mix = _sigmoid(v0_ref[...] + _mm(_mm(xv, v1_ref[...]), v2_ref[...]))
        v = v + (vfirst_ref[...] - v) * mix
    v_ref[...] = v
    a_ref[...] = _sigmoid(a0_ref[...] + _mm(_mm(xa, a1_ref[...]), a2_ref[...]))
    gg_ref[...] = _mm(_sigmoid(_mm(xg, g1_ref[...])), g2_ref[...])


def _rwkv_proj(h, shift, g, mu, wts, vmix, tm, B, T):
    M, D = h.shape
    short = T < tm
    row = pl.BlockSpec((tm, D), lambda i: (i, 0))
    vec = pl.BlockSpec((1, D), lambda i: (0, 0))
    full = lambda a: pl.BlockSpec(a.shape, lambda i: (0, 0))
    if short:
        seq_tiles = 1
        shift_rows = jnp.zeros((B, T, D), F32).at[:, 0].set(shift).reshape(M, D)
        shift_in, shift_spec = shift_rows, row
        hprev_spec = pl.BlockSpec((SUBLANES, D), lambda i: (0, 0))
    else:
        seq_tiles = T // tm
        shift_in = shift.reshape(B, 1, D)
        shift_spec = pl.BlockSpec((1, 1, D), lambda i: (i // seq_tiles, 0, 0))
        hprev_spec = pl.BlockSpec((SUBLANES, D), lambda i: (jnp.maximum(i * (tm // SUBLANES) - 1, 0), 0))
    ins = [h, h, shift_in, g, mu] + list(wts)
    in_specs = [row, hprev_spec, shift_spec, vec, full(mu)] + [full(a) for a in wts]
    if vmix is not None:
        v0, v1, v2, v_first = vmix
        ins += [v0, v1, v2, v_first]
        in_specs += [vec, full(v1), full(v2), row]
    sds = jax.ShapeDtypeStruct((M, D), F32)
    keep = tm if short else SUBLANES
    kern = functools.partial(_rwkv_proj_kernel, has_vmix=vmix is not None, seq_tiles=seq_tiles,
                             short_t=T if short else 0)
    tail, *seqs = pl.pallas_call(
        kern,
        grid=(M // tm,),
        in_specs=in_specs,
        out_specs=[pl.BlockSpec((1, keep, D), lambda i: (i, 0, 0))] + [row] * 6,
        out_shape=[jax.ShapeDtypeStruct((M // tm, keep, D), F32)] + [sds] * 6,
        compiler_params=_cp("parallel"),
        name="rwkv_proj",
    )(*ins)
    if short:
        last_u = tail.reshape(B, T, D)[:, -1]
    else:
        last_u = tail.reshape(B, seq_tiles, keep, D)[:, -1, -1]
    return last_u, seqs


def _scan_kernel(r_ref, k_ref, v_ref, ld_ref, a_ref, g_ref, kkp_ref, ka_ref, rk_ref, lg_ref, lb_ref, h0_ref,
                 z_ref, hout_ref, bdh_ref, pad_ref, *, nb, nqd):
    c = pl.program_id(2)
    rq = lax.broadcasted_iota(jnp.int32, (QUAD, QUAD), 0)
    cq = lax.broadcasted_iota(jnp.int32, (QUAD, QUAD), 1)
    same_head = (rq // C_HEAD) == (cq // C_HEAD)
    units = [(i, qd) for i in range(nb) for qd in range(nqd)]

    @pl.when(c == 0)
    def _():
        for u, (i, qd) in enumerate(units):
            hc = h0_ref[i, qd]
            bdh_ref[u] = jnp.where(same_head, jnp.concatenate([hc] * 4, axis=1), 0.0)

    block_ones = jnp.where(same_head, 1.0, 0.0).astype(BF16)
    gens = [_scan_chunk(u, i, qd, same_head, r_ref, k_ref, v_ref, ld_ref, a_ref, g_ref, kkp_ref, ka_ref, rk_ref,
                        lg_ref, lb_ref, z_ref, bdh_ref, pad_ref) for u, (i, qd) in enumerate(units)]
    hnews = [None] * len(units)
    answers = [None] * len(units)
    while any(h is None for h in hnews):
        asks = {}
        for u in range(len(units)):
            if hnews[u] is None:
                try:
                    ask = gens[u].send(answers[u])
                    if ask is not None:
                        asks[u] = ask
                except StopIteration as done:
                    hnews[u] = done.value
        answers = [None] * len(units)
        if asks:
            sums = _mm(jnp.concatenate(list(asks.values()), axis=0), block_ones)
            row = 0
            for u, ask in asks.items():
                answers[u] = sums[row:row + ask.shape[0]]
                row += ask.shape[0]

    @pl.when(c == pl.num_programs(2) - 1)
    def _():
        for u, (i, qd) in enumerate(units):
            hc = hnews[u][:, 0:C_HEAD]
            for hh in range(1, 4):
                hc = hc + hnews[u][:, hh * C_HEAD:(hh + 1) * C_HEAD]
            hout_ref[i, qd] = hc


def _scan_chunk(u, i, qd, same_head, r_ref, k_ref, v_ref, ld_ref, a_ref, g_ref, kkp_ref, ka_ref, rk_ref, lg_ref,
                lb_ref, z_ref, bdh_ref, pad_ref):
    C = CHUNK
    lanes = slice(qd * QUAD, (qd + 1) * QUAD)
    t_valid = r_ref.shape[1]

    def bd(x):
        return jnp.where(same_head, jnp.concatenate([x] * 4, axis=0), 0.0)

    def load(j, ref):
        if t_valid == C:
            return ref[i, :, lanes]
        pad_ref[j, u] = jnp.zeros((C, QUAD), F32)
        pad_ref[j, u, 0:t_valid, :] = ref[i, :, lanes]
        return pad_ref[j, u]

    r, k, v, ld, a = (load(j, ref) for j, ref in enumerate((r_ref, k_ref, v_ref, ld_ref, a_ref)))
    kkp, ka, rkp, lg, lb = (ref[:, lanes] for ref in (kkp_ref, ka_ref, rk_ref, lg_ref, lb_ref))

    tt = lax.broadcasted_iota(jnp.int32, (C, C), 0)
    ts = lax.broadcasted_iota(jnp.int32, (C, C), 1)
    tri = jnp.where(ts <= tt, 1.0, 0.0).astype(BF16)
    lc = sum(jnp.dot(tri, p, preferred_element_type=F32) for p in _split2(ld))
    lend = lc[C - 1:C, :]
    gam = jnp.exp(lc)
    gam_prev = jnp.exp(lc - ld)
    gam_inv = jnp.exp(-lc)
    gam_end = jnp.exp(lend - lc)
    yield

    kk = k * kkp
    k2 = k * (1.0 + (a - 1.0) * ka)
    head_sums = yield jnp.concatenate([kk * kk, r * k2 * rkp], axis=0)
    ss, rk_sum = head_sums[:C], head_sums[C:]
    kkn = kk / jnp.maximum(jnp.sqrt(ss), 1e-12)
    bv = kkn * a
    at = -kkn * gam_prev
    rt = r * gam
    lhs = jnp.concatenate([at, rt], axis=0)
    ab = _mm_nt(lhs, bd(bv * gam_inv))
    ak = _mm_nt(lhs, bd(k2 * gam_inv))
    yield
    t_i = lax.broadcasted_iota(jnp.int32, (C, QUAD), 0)
    s_i = lax.broadcasted_iota(jnp.int32, (C, QUAD), 1) % C_HEAD
    strict = s_i < t_i
    incl = s_i <= t_i
    a_ab = jnp.where(strict, ab[:C], 0.0)
    m_rb = jnp.where(incl, ab[C:], 0.0)
    a_ak = jnp.where(strict, ak[:C], 0.0)
    m_rk = jnp.where(incl, ak[C:], 0.0)

    p = a_ab
    tinv = jnp.where(s_i == t_i, 1.0, 0.0) + p
    n_sq = max((t_valid - 1).bit_length(), 1) - 1
    p = _mm(p, bd(p))
    yield
    for jj in range(n_sq):
        w = bd(p)
        if jj < n_sq - 1:
            pt = _mm(jnp.concatenate([p, tinv], axis=0), w)
            p = pt[:C]
            tinv = tinv + pt[C:]
        else:
            tinv = tinv + _mm(tinv, w)
        yield

    avrv = _mm(jnp.concatenate([a_ak, m_rk], axis=0), bd(v))
    x2, rv = avrv[:C], avrv[C:]
    yield
    w1 = _mm(tinv, bd(at))
    w2 = _mm(tinv, bd(x2))
    yield
    y1 = rt + _mm(m_rb, bd(w1))
    y2 = rv + _mm(m_rb, bd(w2))
    yield
    be = bv * gam_end
    ke = k2 * gam_end
    fresh = _mm_tn(jnp.concatenate([w2, v], axis=0), jnp.concatenate([be, ke], axis=0))
    yield
    gmat = bdh_ref[u]
    y = _mm_nt(y1, gmat) + y2
    ut = _mm_nt(gmat, w1)
    yield
    carried = _mm(ut, be)
    hnew = gmat * jnp.exp(lend) + jnp.where(same_head, carried + fresh, 0.0)
    bdh_ref[u] = hnew
    yield

    inv_n = 1.0 / C_HEAD
    mean = (yield y) * inv_n
    d = y - mean
    var = (yield d * d) * inv_n
    yn = d * lax.rsqrt(var + GN_EPS) * lg + lb
    bonus = rk_sum * v
    z_ref[i, :, lanes] = ((yn + bonus)[0:t_valid]) * g_ref[i, :, lanes]
    return hnew


def _rwkv_scan(r, k, v, ld, a, g, kkp, ka, rk, lg, lb, state, B, T):
    D = r.shape[1]
    nq = D // QUAD
    tc = min(T, CHUNK)
    nc = T // tc
    nb = math.gcd(B, SCAN_SEQS)
    nqd = math.gcd(nq, SCAN_QUADS)
    h0 = state.reshape(B, nq, QUAD, C_HEAD)
    seq = pl.BlockSpec((nb, tc, nqd * QUAD), lambda b, q, c: (b, c, q))
    vec = pl.BlockSpec((1, nqd * QUAD), lambda b, q, c: (0, q))
    st = pl.BlockSpec((nb, nqd, QUAD, C_HEAD), lambda b, q, c: (b, q, 0, 0))
    z, hout = pl.pallas_call(
        functools.partial(_scan_kernel, nb=nb, nqd=nqd),
        grid=(B // nb, nq // nqd, nc),
        in_specs=[seq] * 6 + [vec] * 5 + [st],
        out_specs=[seq, st],
        out_shape=[jax.ShapeDtypeStruct((B, T, D), F32), jax.ShapeDtypeStruct((B, nq, QUAD, C_HEAD), F32)],
        scratch_shapes=[pltpu.VMEM((nb * nqd, QUAD, QUAD), F32),
                        pltpu.VMEM((5, nb * nqd, CHUNK, QUAD) if tc < CHUNK else (1, 1, SUBLANES, LANES), F32)],
        compiler_params=_cp("parallel", "parallel", "arbitrary"),
        name="rwkv_scan",
    )(*(s.reshape(B, T, D) for s in (r, k, v, ld, a, g)), kkp, ka, rk, lg, lb, h0)
    new_state = hout.reshape(B, D // C_HEAD, C_HEAD, C_HEAD)
    return z.reshape(B * T, D), new_state


def _row_tile(M, pref):
    return pref if M % pref == 0 else M


def _trunk(x, pos, conv_buf, wkv, shift, p, paged):
    B, T, D = x.shape
    M = B * T
    tm = _row_tile(M, ROW_TILE)
    tm_wide = _row_tile(T if paged is None else M, WIDE_ROW_TILE)
    h = x.reshape(M, D)
    depth = p['norm_mix'].shape[0]
    tabs = _rope_tables(pos)
    if paged is not None:
        tabs = tuple(jnp.tile(t, (B, 1)) for t in tabs)
    vec = lambda a: a.reshape(1, -1)
    new_conv, new_wkv, new_shift = [], [], []
    kv_stack = None
    v_first = None
    y = None
    for l in range(depth):
        i = l // 2
        if l % 2 == 0:
            q, k, v, glu = _in_proj_ab(h, vec(p['norm_mix'][l]), p['w_in_ab'][i], tabs, tm_wide, kv_stack)
            kv_stack = (k, v)
            lam_init = 0.8 - 0.6 * math.exp(-0.3 * l)
            lam_p, sub = p['diff_lambda'][i], vec(p['subln'][i])
            cw, cb = p['conv_w'][i], vec(p['conv_b'][i])
            clg, clb = vec(p['conv_ln_g'][i]), vec(p['conv_ln_b'][i])
            if paged is None:
                tk = _row_tile(T, FLASH_TK)
                tq = _row_tile(T, FLASH_TQ)
                o = _flash_prompt(q, k, v, lam_p, sub, lam_init, B, T, tq, tk, min(tq, FLASH_ROWS))
                cm = _conv_prompt(glu, conv_buf[i], cw, cb, clg, clb, B, T, _row_tile(T, 256))
                buf = glu.reshape(B, T, B_WIDTH)[:, -(CONV_WIDTH - 1):]
            else:
                page_table, cache_k, cache_v = paged
                o = _paged_sample(page_table, q, k[-1], v[-1], cache_k, cache_v, i, lam_p, sub, lam_init, B, T)
                cm, buf = _conv_sample(glu, conv_buf[i], cw, cb, clg, clb, B, T)
            mix_in, w_mix = [o, cm], p['w_out_ab'][i]
            new_conv.append(buf)
        else:
            wts = [p['rwkv_wr'][i], p['rwkv_wk'][i], p['rwkv_wv'][i],
                   vec(p['rwkv_w0'][i]), p['rwkv_w1'][i], p['rwkv_w2'][i],
                   vec(p['rwkv_a0'][i]), p['rwkv_a1'][i], p['rwkv_a2'][i],
                   p['rwkv_g1'][i], p['rwkv_g2'][i]]
            vmix = None
            if v_first is not None:
                vmix = (vec(p['rwkv_v0'][i - 1]), p['rwkv_v1'][i - 1], p['rwkv_v2'][i - 1], v_first)
            last_u, seqs = _rwkv_proj(h, shift[i], vec(p['norm_mix'][l]), p['rwkv_mu'][i], wts, vmix, tm, B, T)
            if v_first is None:
                v_first = seqs[2]
            z, s_new = _rwkv_scan(*seqs, vec(p['rwkv_kk'][i]), vec(p['rwkv_ka'][i]), vec(p['rwkv_rk'][i]),
                                  vec(p['rwkv_lnx_g'][i]), vec(p['rwkv_lnx_b'][i]), wkv[i], B, T)
            mix_in, w_mix = [z], p['rwkv_wo'][i]
            new_wkv.append(s_new)
            new_shift.append(last_u)
        last = l == depth - 1
        res = _mix_ffn(mix_in, h, w_mix, vec(p['norm_ffn'][l]), p['w_gate'][l], p['w_up'][l], p['w_down'][l], tm_wide,
                       vec(p['norm_final']) if last else None)
        if last:
            h, y = res
        else:
            h = res
    k_all, v_all = (s.reshape(s.shape[0], B, T, A_HEADS, A_V_DIM) for s in kv_stack)
    return (y.reshape(B, T, D), k_all, v_all, jnp.stack(new_conv), jnp.stack(new_wkv), jnp.stack(new_shift))


_MATMUL_WEIGHTS = ('w_in_ab', 'w_out_ab', 'rwkv_wr', 'rwkv_wk', 'rwkv_wv', 'rwkv_w1', 'rwkv_w2', 'rwkv_a1',
                   'rwkv_a2', 'rwkv_v1', 'rwkv_v2', 'rwkv_g1', 'rwkv_g2', 'rwkv_wo', 'w_gate', 'w_up', 'w_down')


def kernel(x_prompt, x_sample, cache_k, cache_v, state_conv, state_wkv, state_shift, page_table, norm_mix, norm_ffn, norm_final, w_in_ab, diff_lambda, subln, conv_w, conv_b, conv_ln_g, conv_ln_b, w_out_ab, rwkv_mu, rwkv_wr, rwkv_wk, rwkv_wv, rwkv_w0, rwkv_w1, rwkv_w2, rwkv_a0, rwkv_a1, rwkv_a2, rwkv_v0, rwkv_v1, rwkv_v2, rwkv_g1, rwkv_g2, rwkv_kk, rwkv_ka, rwkv_rk, rwkv_lnx_g, rwkv_lnx_b, rwkv_wo, w_gate, w_up, w_down):
    p = dict(norm_mix=norm_mix, norm_ffn=norm_ffn, norm_final=norm_final, w_in_ab=w_in_ab,
             diff_lambda=diff_lambda, subln=subln, conv_w=conv_w, conv_b=conv_b, conv_ln_g=conv_ln_g,
             conv_ln_b=conv_ln_b, w_out_ab=w_out_ab, rwkv_mu=rwkv_mu, rwkv_wr=rwkv_wr, rwkv_wk=rwkv_wk,
             rwkv_wv=rwkv_wv, rwkv_w0=rwkv_w0, rwkv_w1=rwkv_w1, rwkv_w2=rwkv_w2, rwkv_a0=rwkv_a0,
             rwkv_a1=rwkv_a1, rwkv_a2=rwkv_a2, rwkv_v0=rwkv_v0, rwkv_v1=rwkv_v1, rwkv_v2=rwkv_v2,
             rwkv_g1=rwkv_g1, rwkv_g2=rwkv_g2, rwkv_kk=rwkv_kk, rwkv_ka=rwkv_ka, rwkv_rk=rwkv_rk,
             rwkv_lnx_g=rwkv_lnx_g, rwkv_lnx_b=rwkv_lnx_b, rwkv_wo=rwkv_wo, w_gate=w_gate, w_up=w_up,
             w_down=w_down)
    for name in _MATMUL_WEIGHTS:
        p[name] = p[name].astype(BF16)

    Bp, Tp, D = x_prompt.shape
    n_a, n_c = state_conv.shape[0], state_wkv.shape[0]
    zero_conv = jnp.zeros((n_a, Bp, CONV_WIDTH - 1, B_WIDTH), F32)
    zero_wkv = jnp.zeros((n_c, Bp) + state_wkv.shape[2:], F32)
    zero_shift = jnp.zeros((n_c, Bp, D), F32)
    outs_p = _trunk(x_prompt, jnp.arange(Tp), zero_conv, zero_wkv, zero_shift, p, None)

    past = page_table.shape[1] * PAGE_SIZE
    outs_s = _trunk(x_sample, past + jnp.arange(x_sample.shape[1]), state_conv, state_wkv, state_shift, p,
                    (page_table, cache_k, cache_v))
    return (outs_p[0], outs_s[0]) + tuple(outs_p[1:]) + tuple(outs_s[1:])
```

```python
import functools
import math

import jax
import jax.numpy as jnp
from jax import lax
from jax.experimental import pallas as pl
from jax.experimental.pallas import tpu as pltpu

F32 = jnp.float32
BF16 = jnp.bfloat16

NORM_EPS = 1e-6
SUBLN_EPS = 1e-5
LN_EPS = 1e-5
GN_EPS = 64e-5
ROPE_THETA = 500000.0
ROPE_DIM = 16

A_HEADS = 4
A_QK_DIM = 64
A_V_DIM = 128
A_WIDTH = 512
B_WIDTH = 512
CONV_WIDTH = 31
C_HEAD = 64
PAGE_SIZE = 128

LANES = 128
SUBLANES = 8
QUAD = 4 * C_HEAD
CHUNK = 64
ROW_TILE = 256
WIDE_ROW_TILE = 512
FLASH_TQ = 1024
FLASH_TK = 1024
FLASH_ROWS = 512
PAGES_PER_STEP = 16
SCAN_SEQS = 2
SCAN_QUADS = 4
CONV_HALO = 32
VMEM_LIMIT = 56 * 1024 * 1024


def _cp(*sem):
    return pltpu.CompilerParams(dimension_semantics=sem, vmem_limit_bytes=VMEM_LIMIT)


def _mm(a, b):
    return jnp.dot(a.astype(BF16), b.astype(BF16), preferred_element_type=F32)


def _mm_nt(a, b):
    return lax.dot_general(a.astype(BF16), b.astype(BF16), (((1,), (1,)), ((), ())), preferred_element_type=F32)


def _mm_tn(a, b):
    return lax.dot_general(a.astype(BF16), b.astype(BF16), (((0,), (0,)), ((), ())), preferred_element_type=F32)


def _split2(x):
    hi = x.astype(BF16)
    lo = (x - hi.astype(F32)).astype(BF16)
    return hi, lo


def _sigmoid(x):
    return 1.0 / (1.0 + jnp.exp(-x))


def _rms(x, g, eps):
    ms = jnp.mean(x * x, axis=-1, keepdims=True)
    return x * lax.rsqrt(ms + eps) * g


def _inproj_kernel(x_ref, g_ref, w_ref, c_ref, s1_ref, s2_ref, *rest):
    q_ref, k_ref, v_ref, glu_ref = rest[-4:]
    u = _rms(x_ref[...], g_ref[...], NORM_EPS).astype(BF16)

    def proj(j):
        return jnp.dot(u, w_ref[:, j * A_WIDTH:(j + 1) * A_WIDTH], preferred_element_type=F32)

    ct = jnp.concatenate([c_ref[...]] * A_HEADS, axis=1)
    s1t = jnp.concatenate([s1_ref[...]] * A_HEADS, axis=1)
    s2t = jnp.concatenate([s2_ref[...]] * A_HEADS, axis=1)
    half = ROPE_DIM // 2

    def rope(z):
        return z * ct + pltpu.roll(z, A_WIDTH - half, 1) * s1t + pltpu.roll(z, half, 1) * s2t

    q_ref[...] = rope(proj(0))
    tm = x_ref.shape[0]
    n_prev = k_ref.shape[0] - 1
    if n_prev:
        k_ref[0:n_prev] = rest[0][...]
        v_ref[0:n_prev] = rest[1][...]
    kr = rope(proj(1))
    vv = proj(2)
    for h in range(A_HEADS):
        k_ref[n_prev, pl.ds(h, tm, stride=A_HEADS), :] = kr[:, h * A_V_DIM:(h + 1) * A_V_DIM]
        v_ref[n_prev, pl.ds(h, tm, stride=A_HEADS), :] = vv[:, h * A_V_DIM:(h + 1) * A_V_DIM]
    glu_ref[...] = proj(3) * _sigmoid(proj(4))


def _in_proj_ab(h, g, w_bf, tabs, tm, kv_prev):
    M, D = h.shape
    ntab = tabs[0].shape[0] // tm
    n_prev = 0 if kv_prev is None else kv_prev[0].shape[0]
    tab_spec = pl.BlockSpec((tm, LANES), lambda i: (i % ntab, 0))
    row_out = pl.BlockSpec((tm, A_WIDTH), lambda i: (i, 0))
    kv_out = pl.BlockSpec((n_prev + 1, tm * A_HEADS, A_V_DIM), lambda i: (0, i, 0))
    kv_sds = jax.ShapeDtypeStruct((n_prev + 1, M * A_HEADS, A_V_DIM), F32)
    row_sds = jax.ShapeDtypeStruct((M, A_WIDTH), F32)
    ins = [h, g, w_bf, *tabs]
    in_specs = [pl.BlockSpec((tm, D), lambda i: (i, 0)),
                pl.BlockSpec((1, D), lambda i: (0, 0)),
                pl.BlockSpec(w_bf.shape, lambda i: (0, 0)),
                tab_spec, tab_spec, tab_spec]
    if n_prev:
        ins += list(kv_prev)
        in_specs += [pl.BlockSpec((n_prev, tm * A_HEADS, A_V_DIM), lambda i: (0, i, 0))] * 2
    return pl.pallas_call(
        _inproj_kernel,
        grid=(M // tm,),
        in_specs=in_specs,
        out_specs=[row_out, kv_out, kv_out, row_out],
        out_shape=[row_sds, kv_sds, kv_sds, row_sds],
        compiler_params=_cp("parallel"),
        name="in_proj_ab",
    )(*ins)


def _rope_tables(pos):
    half = ROPE_DIM // 2
    inv = 1.0 / (ROPE_THETA ** (jnp.arange(0, ROPE_DIM, 2, dtype=F32) / ROPE_DIM))
    d = jnp.arange(A_V_DIM) % A_QK_DIM
    ang = pos.astype(F32)[:, None] * inv[d % half][None, :]
    cos, sin = jnp.cos(ang), jnp.sin(ang)
    d = d[None, :]
    c = jnp.where(d < ROPE_DIM, cos, 1.0)
    s1 = jnp.where(d < half, -sin, 0.0)
    s2 = jnp.where((d >= half) & (d < ROPE_DIM), sin, 0.0)
    return c, s1, s2


def _diff_lambda(lam_ref, lam_init):
    lp = lam_ref[...]
    s1 = jnp.sum(lp[0:1] * lp[1:2], axis=-1, keepdims=True)
    s2 = jnp.sum(lp[2:3] * lp[3:4], axis=-1, keepdims=True)
    return jnp.exp(s1) - jnp.exp(s2) + lam_init


def _subln(o, g, lam_init):
    return _rms(o, g, SUBLN_EPS) * (1.0 - lam_init)


def _flash_kernel(qi_ref, ki_ref, lam_ref, sub_ref, q_ref, k_ref, v_ref, o_ref, m_ref, l_ref, acc_ref, *,
                  lam_init, tq, tk, rb):
    pair = pl.program_id(1)
    qi = qi_ref[pair]
    ki = ki_ref[pair]
    nrb = tq // rb

    @pl.when(ki == 0)
    def _():
        m_ref[...] = jnp.full(m_ref.shape, -jnp.inf, F32)
        l_ref[...] = jnp.zeros(l_ref.shape, F32)
        acc_ref[...] = jnp.zeros(acc_ref.shape, F32)

    def chain(h, r, masked, kv):
        slab = h * nrb + r
        kb, vb = kv[h]
        if masked is not None and rb - masked < tk:
            kb, vb = kb[:rb - masked], vb[:rb - masked]
        q = q_ref[r * rb:(r + 1) * rb, h * A_V_DIM:(h + 1) * A_V_DIM] * (A_QK_DIM ** -0.5 * math.log2(math.e))
        lane = lax.broadcasted_iota(jnp.int32, q.shape, 1)
        q12 = jnp.concatenate([jnp.where(lane < A_QK_DIM, q, 0.0), jnp.where(lane >= A_QK_DIM, q, 0.0)], axis=0)
        s = _mm_nt(q12, kb)
        yield
        if masked is not None:
            row = lax.broadcasted_iota(jnp.int32, s.shape, 0)
            col = lax.broadcasted_iota(jnp.int32, s.shape, 1)
            row = jnp.where(row >= rb, row - rb, row)
            s = jnp.where(col + masked <= row, s, -jnp.inf)
        m_prev = m_ref[slab]
        m_new = jnp.maximum(m_prev, jnp.max(s, axis=1, keepdims=True))
        alpha = jnp.exp2(m_prev - m_new)
        p = jnp.exp2(s - jnp.tile(m_new, (1, s.shape[1] // LANES)))
        pv = jnp.dot(p.astype(BF16), vb, preferred_element_type=F32)
        l_ref[slab] = alpha * l_ref[slab] + pv[:, A_V_DIM:]
        acc_ref[slab] = alpha * acc_ref[slab] + pv[:, :A_V_DIM]
        m_ref[slab] = m_new

    def step(rel):
        ones = jnp.ones((tk, LANES), BF16)
        kv = [(k_ref[0, pl.ds(h, tk, stride=A_HEADS), :].astype(BF16),
               jnp.concatenate([v_ref[0, pl.ds(h, tk, stride=A_HEADS), :].astype(BF16), ones], axis=1))
              for h in range(A_HEADS)]
        chains = []
        for h in range(A_HEADS):
            for r in range(nrb):
                off = None if rel is None else rel * tk - r * rb
                if off is not None and off > rb - 1:
                    continue
                if off is not None and off + tk - 1 <= 0:
                    off = None
                chains.append(chain(h, r, off, kv))
        next(chains[0])
        for i, ch in enumerate(chains):
            if i + 1 < len(chains):
                next(chains[i + 1])
            for _ in ch:
                pass

    ratio = tq // tk
    for rel in range(ratio):
        @pl.when(ki == qi * ratio + rel)
        def _(rel=rel):
            step(rel)

    @pl.when(ki < qi * ratio)
    def _():
        step(None)

    @pl.when(ki == (qi + 1) * ratio - 1)
    def _():
        lam = _diff_lambda(lam_ref, lam_init)
        for h in range(A_HEADS):
            for r in range(nrb):
                o = acc_ref[h * nrb + r] / l_ref[h * nrb + r]
                o_ref[r * rb:(r + 1) * rb, h * A_V_DIM:(h + 1) * A_V_DIM] = _subln(
                    o[:rb] - lam * o[rb:], sub_ref[...], lam_init)


def _flash_prompt(q, k, v, lam_p, subln, lam_init, B, T, tq, tk, rb):
    layer = k.shape[0] - 1
    nq, nk, ratio = T // tq, T // tk, tq // tk
    pairs = [(i, j) for i in range(nq) for j in range((i + 1) * ratio)]
    qi_tab = jnp.array([pq for pq, _ in pairs], jnp.int32)
    ki_tab = jnp.array([pk for _, pk in pairs], jnp.int32)
    kern = functools.partial(_flash_kernel, lam_init=lam_init, tq=tq, tk=tk, rb=rb)
    kv_spec = pl.BlockSpec((1, tk * A_HEADS, A_V_DIM), lambda b, s, qt, kt: (layer, b * nk + kt[s], 0))
    q_spec = pl.BlockSpec((tq, A_WIDTH), lambda b, s, qt, kt: (b * nq + qt[s], 0))
    slabs = A_HEADS * (tq // rb)
    return pl.pallas_call(
        kern,
        grid_spec=pltpu.PrefetchScalarGridSpec(
            num_scalar_prefetch=2,
            grid=(B, len(pairs)),
            in_specs=[pl.BlockSpec(lam_p.shape, lambda b, s, qt, kt: (0, 0)),
                      pl.BlockSpec((1, A_V_DIM), lambda b, s, qt, kt: (0, 0)),
                      q_spec, kv_spec, kv_spec],
            out_specs=q_spec,
            scratch_shapes=[pltpu.VMEM((slabs, 2 * rb, LANES), F32), pltpu.VMEM((slabs, 2 * rb, LANES), F32),
                            pltpu.VMEM((slabs, 2 * rb, A_V_DIM), F32)]),
        out_shape=jax.ShapeDtypeStruct((B * T, A_WIDTH), F32),
        compiler_params=_cp("parallel", "arbitrary"),
        name="diff_attn_prompt",
    )(qi_tab, ki_tab, lam_p, subln, q, k, v)


def _paged_kernel(pt_ref, lam_ref, sub_ref, q_ref, kn_ref, vn_ref, *rest, lam_init, pp, tn):
    k_refs = rest[:pp]
    v_refs = rest[pp:2 * pp]
    o_ref, m_ref, l_ref, acc_ref = rest[2 * pp:]
    j = pl.program_id(1)
    hrows = 2 * SUBLANES

    @pl.when(j == 0)
    def _():
        m_ref[...] = jnp.full(m_ref.shape, -jnp.inf, F32)
        l_ref[...] = jnp.zeros(l_ref.shape, F32)
        acc_ref[...] = jnp.zeros(acc_ref.shape, F32)

    qb = (q_ref[0] * (A_QK_DIM ** -0.5)).astype(BF16)

    def scores(keys_of):
        return jnp.concatenate([_mm_nt(qb[h * hrows:(h + 1) * hrows], keys_of(h)) for h in range(A_HEADS)], axis=0)

    def update(s, vals_of, mask=None):
        if mask is not None:
            s = jnp.where(mask(s.shape), s, -jnp.inf)
        m_prev = m_ref[...]
        m_new = jnp.maximum(m_prev, jnp.max(s, axis=1, keepdims=True))
        alpha = jnp.exp(m_prev - m_new)
        p = jnp.exp(s - m_new)
        l_ref[...] = alpha * l_ref[...] + jnp.sum(p, axis=1, keepdims=True)
        pv = jnp.concatenate([_mm(p[h * hrows:(h + 1) * hrows], vals_of(h)) for h in range(A_HEADS)], axis=0)
        acc_ref[...] = alpha * acc_ref[...] + pv
        m_ref[...] = m_new

    rows_of = lambda ref, h: ref[0, 0, pl.ds(h, PAGE_SIZE, stride=A_HEADS), :]
    gather = lambda refs: (lambda h: jnp.concatenate([rows_of(r, h) for r in refs], axis=0))
    half = max(pp // 2, 1)
    groups = [(k_refs[lo:lo + half], v_refs[lo:lo + half]) for lo in range(0, pp, half)]
    all_scores = [scores(gather(kg)) for kg, _ in groups]
    for s_g, (_, vg) in zip(all_scores, groups):
        update(s_g, gather(vg))

    @pl.when(j == pl.num_programs(1) - 1)
    def _():
        def causal(shape):
            tok = lax.broadcasted_iota(jnp.int32, shape, 0) % SUBLANES
            col = lax.broadcasted_iota(jnp.int32, shape, 1)
            return (col <= tok) & (col < tn)

        head = lambda ref, h: ref[0, pl.ds(h, SUBLANES, stride=A_HEADS), :]
        update(scores(lambda h: head(kn_ref, h)), lambda h: head(vn_ref, h), causal)
        o = acc_ref[...] / l_ref[...]
        lam = _diff_lambda(lam_ref, lam_init)
        for h in range(A_HEADS):
            d = o[h * hrows:h * hrows + SUBLANES] - lam * o[h * hrows + SUBLANES:(h + 1) * hrows]
            o_ref[0, h] = _subln(d, sub_ref[...], lam_init)


def _paged_sample(page_table, q, k_new, v_new, cache_k, cache_v, layer, lam_p, subln, lam_init, Bd, tn):
    n_pages = page_table.shape[1]
    pp = math.gcd(n_pages, PAGES_PER_STEP)
    tpad = SUBLANES - tn
    lane_map = jnp.arange(A_V_DIM) // A_QK_DIM
    keep = (lane_map[None, :] == jnp.arange(2)[:, None]).astype(F32)
    q4 = q.reshape(Bd, tn, A_HEADS, A_V_DIM).transpose(0, 2, 1, 3)
    qm = q4[:, :, None, :, :] * keep[None, None, :, None, :]
    qm = jnp.pad(qm, ((0, 0), (0, 0), (0, 0), (0, tpad), (0, 0))).reshape(Bd, A_HEADS * 2 * SUBLANES, A_V_DIM)
    pad = ((0, 0), (0, tpad * A_HEADS), (0, 0))
    kn = jnp.pad(k_new.reshape(Bd, tn * A_HEADS, A_V_DIM), pad)
    vn = jnp.pad(v_new.reshape(Bd, tn * A_HEADS, A_V_DIM), pad)

    page_rows = PAGE_SIZE * A_HEADS
    cache_k = cache_k.reshape(cache_k.shape[:2] + (page_rows, A_V_DIM))
    cache_v = cache_v.reshape(cache_v.shape[:2] + (page_rows, A_V_DIM))

    def page_spec(i):
        return pl.BlockSpec((1, 1, page_rows, A_V_DIM), lambda b, j, pt: (layer, pt[b, j * pp + i], 0, 0))

    per_b = lambda b, j, pt: (b, 0, 0)
    nrow = A_HEADS * 2 * SUBLANES
    kern = functools.partial(_paged_kernel, lam_init=lam_init, pp=pp, tn=tn)
    out = pl.pallas_call(
        kern,
        grid_spec=pltpu.PrefetchScalarGridSpec(
            num_scalar_prefetch=1,
            grid=(Bd, n_pages // pp),
            in_specs=[pl.BlockSpec(lam_p.shape, lambda b, j, pt: (0, 0)),
                      pl.BlockSpec((1, A_V_DIM), lambda b, j, pt: (0, 0)),
                      pl.BlockSpec((1, nrow, A_V_DIM), per_b),
                      pl.BlockSpec((1, SUBLANES * A_HEADS, A_V_DIM), per_b),
                      pl.BlockSpec((1, SUBLANES * A_HEADS, A_V_DIM), per_b)]
                     + [page_spec(i) for i in range(pp)] * 2,
            out_specs=pl.BlockSpec((1, A_HEADS, SUBLANES, A_V_DIM), lambda b, j, pt: (b, 0, 0, 0)),
            scratch_shapes=[pltpu.VMEM((nrow, 1), F32), pltpu.VMEM((nrow, 1), F32),
                            pltpu.VMEM((nrow, A_V_DIM), F32)]),
        out_shape=jax.ShapeDtypeStruct((Bd, A_HEADS, SUBLANES, A_V_DIM), F32),
        compiler_params=_cp("parallel", "arbitrary"),
        name="diff_attn_paged",
    )(page_table, lam_p, subln, qm, kn, vn, *([cache_k] * pp), *([cache_v] * pp))
    return out[:, :, :tn].transpose(0, 2, 1, 3).reshape(Bd * tn, A_WIDTH)


def _conv_post(y, b_ref, lg_ref, lb_ref):
    y = y + b_ref[...]
    mu = jnp.mean(y, axis=-1, keepdims=True)
    d = y - mu
    var = jnp.mean(d * d, axis=-1, keepdims=True)
    z = d * lax.rsqrt(var + LN_EPS) * lg_ref[...] + lb_ref[...]
    return z * _sigmoid(z)


def _conv_kernel(glu_ref, halo_ref, buf_ref, w_ref, b_ref, lg_ref, lb_ref, o_ref, xp_ref, xs_ref, *, tt, sub):
    j = pl.program_id(1)
    rows = CONV_HALO + tt
    xp_ref[0:CONV_HALO, :] = jnp.where(j == 0, buf_ref[0], halo_ref[...])
    xp_ref[CONV_HALO:, :] = glu_ref[...]
    for ph in range(SUBLANES):
        xs_ref[ph, 0:rows - ph, :] = xp_ref[ph:rows, :]
    lead = CONV_HALO - (CONV_WIDTH - 1)
    w = w_ref[...]
    for r0 in range(0, tt, sub):
        acc = None
        for tap in range(CONV_WIDTH):
            ph, base = (lead + tap) % SUBLANES, (lead + tap) // SUBLANES * SUBLANES
            term = xs_ref[ph, r0 + base:r0 + base + sub, :] * w[tap:tap + 1]
            acc = term if acc is None else acc + term
        o_ref[r0:r0 + sub, :] = _conv_post(acc, b_ref, lg_ref, lb_ref)


def _conv_prompt(glu, buf, w, b, lg, lb, B, T, tt):
    nt = T // tt
    hpt = tt // CONV_HALO
    bufp = jnp.pad(buf, ((0, 0), (CONV_HALO - (CONV_WIDTH - 1), 0), (0, 0)))
    vec = pl.BlockSpec((1, B_WIDTH), lambda bb, j: (0, 0))
    kern = functools.partial(_conv_kernel, tt=tt, sub=min(tt, 64))
    return pl.pallas_call(
        kern,
        grid=(B, nt),
        in_specs=[pl.BlockSpec((tt, B_WIDTH), lambda bb, j: (bb * nt + j, 0)),
                  pl.BlockSpec((CONV_HALO, B_WIDTH), lambda bb, j: (jnp.maximum((bb * nt + j) * hpt - 1, 0), 0)),
                  pl.BlockSpec((1, CONV_HALO, B_WIDTH), lambda bb, j: (bb, 0, 0)),
                  pl.BlockSpec((CONV_WIDTH, B_WIDTH), lambda bb, j: (0, 0)),
                  vec, vec, vec],
        out_specs=pl.BlockSpec((tt, B_WIDTH), lambda bb, j: (bb * nt + j, 0)),
        out_shape=jax.ShapeDtypeStruct((B * T, B_WIDTH), F32),
        scratch_shapes=[pltpu.VMEM((CONV_HALO + tt, B_WIDTH), F32),
                        pltpu.VMEM((SUBLANES, CONV_HALO + tt, B_WIDTH), F32)],
        compiler_params=_cp("parallel", "arbitrary"),
        name="conv_prompt",
    )(glu, glu, bufp, w, b, lg, lb)


def _conv_step_kernel(xp_ref, w_ref, b_ref, lg_ref, lb_ref, o_ref, *, tn):
    w = w_ref[...]
    for t in range(tn):
        acc = xp_ref[t] * w[0:1]
        for tap in range(1, CONV_WIDTH):
            acc = acc + xp_ref[t + tap] * w[tap:tap + 1]
        o_ref[t] = _conv_post(acc, b_ref, lg_ref, lb_ref)


def _conv_sample(glu, buf, w, b, lg, lb, Bd, tn):
    xp = jnp.concatenate([buf, glu.reshape(Bd, tn, B_WIDTH)], axis=1)
    out = pl.pallas_call(
        functools.partial(_conv_step_kernel, tn=tn),
        out_shape=jax.ShapeDtypeStruct((tn, Bd, B_WIDTH), F32),
        compiler_params=pltpu.CompilerParams(vmem_limit_bytes=VMEM_LIMIT),
        name="conv_sample",
    )(xp.transpose(1, 0, 2), w, b, lg, lb)
    return out.transpose(1, 0, 2).reshape(Bd * tn, B_WIDTH), xp[:, -(CONV_WIDTH - 1):]


def _mix_ffn_kernel(*refs, n_in, final):
    x_refs = refs[:n_in]
    h_ref, wo_ref, g_ref, wg_ref, wu_ref, wd_ref = refs[n_in:n_in + 6]
    rest = refs[n_in + 6:]
    h = h_ref[...]
    k0 = 0
    for x_ref in x_refs:
        kw = x_ref.shape[1]
        h = h + jnp.dot(x_ref[...].astype(BF16), wo_ref[k0:k0 + kw, :], preferred_element_type=F32)
        k0 += kw
    f = _rms(h, g_ref[...], NORM_EPS).astype(BF16)
    gate = jnp.dot(f, wg_ref[...], preferred_element_type=F32)
    up = jnp.dot(f, wu_ref[...], preferred_element_type=F32)
    act = (gate * _sigmoid(gate) * up).astype(BF16)
    out = h + jnp.dot(act, wd_ref[...], preferred_element_type=F32)
    if final:
        gf_ref, o_ref, y_ref = rest
        y_ref[...] = _rms(out, gf_ref[...], NORM_EPS)
    else:
        (o_ref,) = rest
    o_ref[...] = out


def _mix_ffn(xs, h, wo, g, wg, wu, wd, tm, g_final=None):
    M, D = h.shape
    final = g_final is not None
    row = pl.BlockSpec((tm, D), lambda i: (i, 0))
    vec = pl.BlockSpec((1, D), lambda i: (0, 0))
    full = lambda a: pl.BlockSpec(a.shape, lambda i: (0, 0), pipeline_mode=pl.Buffered(1))
    ins = list(xs) + [h, wo, g, wg, wu, wd] + ([g_final] if final else [])
    in_specs = ([pl.BlockSpec((tm, x.shape[1]), lambda i: (i, 0)) for x in xs]
                + [row, full(wo), vec, full(wg), full(wu), full(wd)] + ([vec] if final else []))
    sds = jax.ShapeDtypeStruct((M, D), F32)
    return pl.pallas_call(
        functools.partial(_mix_ffn_kernel, n_in=len(xs), final=final),
        grid=(M // tm,),
        in_specs=in_specs,
        out_specs=[row, row] if final else row,
        out_shape=[sds, sds] if final else sds,
        compiler_params=_cp("parallel"),
        name="mix_ffn",
    )(*ins)


def _rwkv_proj_kernel(*refs, has_vmix, seq_tiles, short_t):
    it = iter(refs)
    h_ref, hprev_ref, shift_ref, g_ref, mu_ref = next(it), next(it), next(it), next(it), next(it)
    wr_ref, wk_ref, wv_ref = next(it), next(it), next(it)
    w0_ref, w1_ref, w2_ref = next(it), next(it), next(it)
    a0_ref, a1_ref, a2_ref = next(it), next(it), next(it)
    g1_ref, g2_ref = next(it), next(it)
    if has_vmix:
        v0_ref, v1_ref, v2_ref, vfirst_ref = next(it), next(it), next(it), next(it)
    u_ref, r_ref, k_ref, v_ref, ld_ref, a_ref, gg_ref = (next(it) for _ in range(7))

    i = pl.program_id(0)
    u = _rms(h_ref[...], g_ref[...], NORM_EPS)
    keep = u_ref.shape[1]
    u_ref[0] = u[u.shape[0] - keep:]
    rolled = pltpu.roll(u, 1, 0)
    row = lax.broadcasted_iota(jnp.int32, u.shape, 0)
    if short_t:
        prev = jnp.where(row % short_t == 0, shift_ref[...], rolled)
    else:
        up8 = _rms(hprev_ref[...], g_ref[...], NORM_EPS)
        first = jnp.where(i % seq_tiles == 0, shift_ref[0], up8[SUBLANES - 1:SUBLANES])
        prev = jnp.where(row == 0, first, rolled)
    xx = prev - u
    mu = mu_ref[...]
    xr, xw, xk, xv, xa, xg = (u + xx * mu[j:j + 1] for j in range(6))

    r_ref[...] = _mm(xr, wr_ref[...])
    k_ref[...] = _mm(xk, wk_ref[...])
    v = _mm(xv, wv_ref[...])
    wl = w0_ref[...] + _mm(jnp.tanh(_mm(xw, w1_ref[...])), w2_ref[...])
    z = -wl
    w = -(jnp.maximum(z, 0.0) + jnp.log(1.0 + jnp.exp(-jnp.abs(z)))) - 0.5
    ld_ref[...] = -jnp.exp(w)
    if has_vmix:
        mix = _sigmoid(v0_ref[...] + _mm(_mm(xv, v1_ref[...]), v2_ref[...]))
        v = v + (vfirst_ref[...] - v) * mix
    v_ref[...] = v
    a_ref[...] = _sigmoid(a0_ref[...] + _mm(_mm(xa, a1_ref[...]), a2_ref[...]))
    gg_ref[...] = _mm(_sigmoid(_mm(xg, g1_ref[...])), g2_ref[...])


def _rwkv_proj(h, shift, g, mu, wts, vmix, tm, B, T):
    M, D = h.shape
    short = T < tm
    row = pl.BlockSpec((tm, D), lambda i: (i, 0))
    vec = pl.BlockSpec((1, D), lambda i: (0, 0))
    full = lambda a: pl.BlockSpec(a.shape, lambda i: (0, 0))
    if short:
        seq_tiles = 1
        shift_rows = jnp.zeros((B, T, D), F32).at[:, 0].set(shift).reshape(M, D)
        shift_in, shift_spec = shift_rows, row
        hprev_spec = pl.BlockSpec((SUBLANES, D), lambda i: (0, 0))
    else:
        seq_tiles = T // tm
        shift_in = shift.reshape(B, 1, D)
        shift_spec = pl.BlockSpec((1, 1, D), lambda i: (i // seq_tiles, 0, 0))
        hprev_spec = pl.BlockSpec((SUBLANES, D), lambda i: (jnp.maximum(i * (tm // SUBLANES) - 1, 0), 0))
    ins = [h, h, shift_in, g, mu] + list(wts)
    in_specs = [row, hprev_spec, shift_spec, vec, full(mu)] + [full(a) for a in wts]
    if vmix is not None:
        v0, v1, v2, v_first = vmix
        ins += [v0, v1, v2, v_first]
        in_specs += [vec, full(v1), full(v2), row]
    sds = jax.ShapeDtypeStruct((M, D), F32)
    keep = tm if short else SUBLANES
    kern = functools.partial(_rwkv_proj_kernel, has_vmix=vmix is not None, seq_tiles=seq_tiles,
                             short_t=T if short else 0)
    tail, *seqs = pl.pallas_call(
        kern,
        grid=(M // tm,),
        in_specs=in_specs,
        out_specs=[pl.BlockSpec((1, keep, D), lambda i: (i, 0, 0))] + [row] * 6,
        out_shape=[jax.ShapeDtypeStruct((M // tm, keep, D), F32)] + [sds] * 6,
        compiler_params=_cp("parallel"),
        name="rwkv_proj",
    )(*ins)
    if short:
        last_u = tail.reshape(B, T, D)[:, -1]
    else:
        last_u = tail.reshape(B, seq_tiles, keep, D)[:, -1, -1]
    return last_u, seqs


def _scan_kernel(r_ref, k_ref, v_ref, ld_ref, a_ref, g_ref, kkp_ref, ka_ref, rk_ref, lg_ref, lb_ref, h0_ref,
                 z_ref, hout_ref, bdh_ref, pad_ref, *, nb, nqd):
    c = pl.program_id(2)
    rq = lax.broadcasted_iota(jnp.int32, (QUAD, QUAD), 0)
    cq = lax.broadcasted_iota(jnp.int32, (QUAD, QUAD), 1)
    same_head = (rq // C_HEAD) == (cq // C_HEAD)
    units = [(i, qd) for i in range(nb) for qd in range(nqd)]

    @pl.when(c == 0)
    def _():
        for u, (i, qd) in enumerate(units):
            hc = h0_ref[i, qd]
            bdh_ref[u] = jnp.where(same_head, jnp.concatenate([hc] * 4, axis=1), 0.0)

    block_ones = jnp.where(same_head, 1.0, 0.0).astype(BF16)
    gens = [_scan_chunk(u, i, qd, same_head, r_ref, k_ref, v_ref, ld_ref, a_ref, g_ref, kkp_ref, ka_ref, rk_ref,
                        lg_ref, lb_ref, z_ref, bdh_ref, pad_ref) for u, (i, qd) in enumerate(units)]
    hnews = [None] * len(units)
    answers = [None] * len(units)
    while any(h is None for h in hnews):
        asks = {}
        for u in range(len(units)):
            if hnews[u] is None:
                try:
                    ask = gens[u].send(answers[u])
                    if ask is not None:
                        asks[u] = ask
                except StopIteration as done:
                    hnews[u] = done.value
        answers = [None] * len(units)
        if asks:
            sums = _mm(jnp.concatenate(list(asks.values()), axis=0), block_ones)
            row = 0
            for u, ask in asks.items():
                answers[u] = sums[row:row + ask.shape[0]]
                row += ask.shape[0]

    @pl.when(c == pl.num_programs(2) - 1)
    def _():
        for u, (i, qd) in enumerate(units):
            hc = hnews[u][:, 0:C_HEAD]
            for hh in range(1, 4):
                hc = hc + hnews[u][:, hh * C_HEAD:(hh + 1) * C_HEAD]
            hout_ref[i, qd] = hc


def _scan_chunk(u, i, qd, same_head, r_ref, k_ref, v_ref, ld_ref, a_ref, g_ref, kkp_ref, ka_ref, rk_ref, lg_ref,
                lb_ref, z_ref, bdh_ref, pad_ref):
    C = CHUNK
    lanes = slice(qd * QUAD, (qd + 1) * QUAD)
    t_valid = r_ref.shape[1]

    def bd(x):
        return jnp.where(same_head, jnp.concatenate([x] * 4, axis=0), 0.0)

    def load(j, ref):
        if t_valid == C:
            return ref[i, :, lanes]
        pad_ref[j, u] = jnp.zeros((C, QUAD), F32)
        pad_ref[j, u, 0:t_valid, :] = ref[i, :, lanes]
        return pad_ref[j, u]

    r, k, v, ld, a = (load(j, ref) for j, ref in enumerate((r_ref, k_ref, v_ref, ld_ref, a_ref)))
    kkp, ka, rkp, lg, lb = (ref[:, lanes] for ref in (kkp_ref, ka_ref, rk_ref, lg_ref, lb_ref))

    tt = lax.broadcasted_iota(jnp.int32, (C, C), 0)
    ts = lax.broadcasted_iota(jnp.int32, (C, C), 1)
    tri = jnp.where(ts <= tt, 1.0, 0.0).astype(BF16)
    lc = sum(jnp.dot(tri, p, preferred_element_type=F32) for p in _split2(ld))
    lend = lc[C - 1:C, :]
    gam = jnp.exp(lc)
    gam_prev = jnp.exp(lc - ld)
    gam_inv = jnp.exp(-lc)
    gam_end = jnp.exp(lend - lc)
    yield

    kk = k * kkp
    k2 = k * (1.0 + (a - 1.0) * ka)
    head_sums = yield jnp.concatenate([kk * kk, r * k2 * rkp], axis=0)
    ss, rk_sum = head_sums[:C], head_sums[C:]
    kkn = kk / jnp.maximum(jnp.sqrt(ss), 1e-12)
    bv = kkn * a
    at = -kkn * gam_prev
    rt = r * gam
    lhs = jnp.concatenate([at, rt], axis=0)
    ab = _mm_nt(lhs, bd(bv * gam_inv))
    ak = _mm_nt(lhs, bd(k2 * gam_inv))
    yield
    t_i = lax.broadcasted_iota(jnp.int32, (C, QUAD), 0)
    s_i = lax.broadcasted_iota(jnp.int32, (C, QUAD), 1) % C_HEAD
    strict = s_i < t_i
    incl = s_i <= t_i
    a_ab = jnp.where(strict, ab[:C], 0.0)
    m_rb = jnp.where(incl, ab[C:], 0.0)
    a_ak = jnp.where(strict, ak[:C], 0.0)
    m_rk = jnp.where(incl, ak[C:], 0.0)

    p = a_ab
    tinv = jnp.where(s_i == t_i, 1.0, 0.0) + p
    n_sq = max((t_valid - 1).bit_length(), 1) - 1
    p = _mm(p, bd(p))
    yield
    for jj in range(n_sq):
        w = bd(p)
        if jj < n_sq - 1:
            pt = _mm(jnp.concatenate([p, tinv], axis=0), w)
            p = pt[:C]
            tinv = tinv + pt[C:]
        else:
            tinv = tinv + _mm(tinv, w)
        yield

    avrv = _mm(jnp.concatenate([a_ak, m_rk], axis=0), bd(v))
    x2, rv = avrv[:C], avrv[C:]
    yield
    w1 = _mm(tinv, bd(at))
    w2 = _mm(tinv, bd(x2))
    yield
    y1 = rt + _mm(m_rb, bd(w1))
    y2 = rv + _mm(m_rb, bd(w2))
    yield
    be = bv * gam_end
    ke = k2 * gam_end
    fresh = _mm_tn(jnp.concatenate([w2, v], axis=0), jnp.concatenate([be, ke], axis=0))
    yield
    gmat = bdh_ref[u]
    y = _mm_nt(y1, gmat) + y2
    ut = _mm_nt(gmat, w1)
    yield
    carried = _mm(ut, be)
    hnew = gmat * jnp.exp(lend) + jnp.where(same_head, carried + fresh, 0.0)
    bdh_ref[u] = hnew
    yield

    inv_n = 1.0 / C_HEAD
    mean = (yield y) * inv_n
    d = y - mean
    var = (yield d * d) * inv_n
    yn = d * lax.rsqrt(var + GN_EPS) * lg + lb
    bonus = rk_sum * v
    z_ref[i, :, lanes] = ((yn + bonus)[0:t_valid]) * g_ref[i, :, lanes]
    return hnew


def _rwkv_scan(r, k, v, ld, a, g, kkp, ka, rk, lg, lb, state, B, T):
    D = r.shape[1]
    nq = D // QUAD
    tc = min(T, CHUNK)
    nc = T // tc
    nb = math.gcd(B, SCAN_SEQS)
    nqd = math.gcd(nq, SCAN_QUADS)
    h0 = state.reshape(B, nq, QUAD, C_HEAD)
    seq = pl.BlockSpec((nb, tc, nqd * QUAD), lambda b, q, c: (b, c, q))
    vec = pl.BlockSpec((1, nqd * QUAD), lambda b, q, c: (0, q))
    st = pl.BlockSpec((nb, nqd, QUAD, C_HEAD), lambda b, q, c: (b, q, 0, 0))
    z, hout = pl.pallas_call(
        functools.partial(_scan_kernel, nb=nb, nqd=nqd),
        grid=(B // nb, nq // nqd, nc),
        in_specs=[seq] * 6 + [vec] * 5 + [st],
        out_specs=[seq, st],
        out_shape=[jax.ShapeDtypeStruct((B, T, D), F32), jax.ShapeDtypeStruct((B, nq, QUAD, C_HEAD), F32)],
        scratch_shapes=[pltpu.VMEM((nb * nqd, QUAD, QUAD), F32),
                        pltpu.VMEM((5, nb * nqd, CHUNK, QUAD) if tc < CHUNK else (1, 1, SUBLANES, LANES), F32)],
        compiler_params=_cp("parallel", "parallel", "arbitrary"),
        name="rwkv_scan",
    )(*(s.reshape(B, T, D) for s in (r, k, v, ld, a, g)), kkp, ka, rk, lg, lb, h0)
    new_state = hout.reshape(B, D // C_HEAD, C_HEAD, C_HEAD)
    return z.reshape(B * T, D), new_state


def _row_tile(M, pref):
    return pref if M % pref == 0 else M


def _trunk(x, pos, conv_buf, wkv, shift, p, paged):
    B, T, D = x.shape
    M = B * T
    tm = _row_tile(M, ROW_TILE)
    tm_wide = _row_tile(T if paged is None else M, WIDE_ROW_TILE)
    h = x.reshape(M, D)
    depth = p['norm_mix'].shape[0]
    tabs = _rope_tables(pos)
    if paged is not None:
        tabs = tuple(jnp.tile(t, (B, 1)) for t in tabs)
    vec = lambda a: a.reshape(1, -1)
    new_conv, new_wkv, new_shift = [], [], []
    kv_stack = None
    v_first = None
    y = None
    for l in range(depth):
        i = l // 2
        if l % 2 == 0:
            q, k, v, glu = _in_proj_ab(h, vec(p['norm_mix'][l]), p['w_in_ab'][i], tabs, tm_wide, kv_stack)
            kv_stack = (k, v)
            lam_init = 0.8 - 0.6 * math.exp(-0.3 * l)
            lam_p, sub = p['diff_lambda'][i], vec(p['subln'][i])
            cw, cb = p['conv_w'][i], vec(p['conv_b'][i])
            clg, clb = vec(p['conv_ln_g'][i]), vec(p['conv_ln_b'][i])
            if paged is None:
                tk = _row_tile(T, FLASH_TK)
                tq = _row_tile(T, FLASH_TQ)
                o = _flash_prompt(q, k, v, lam_p, sub, lam_init, B, T, tq, tk, min(tq, FLASH_ROWS))
                cm = _conv_prompt(glu, conv_buf[i], cw, cb, clg, clb, B, T, _row_tile(T, 256))
                buf = glu.reshape(B, T, B_WIDTH)[:, -(CONV_WIDTH - 1):]
            else:
                page_table, cache_k, cache_v = paged
                o = _paged_sample(page_table, q, k[-1], v[-1], cache_k, cache_v, i, lam_p, sub, lam_init, B, T)
                cm, buf = _conv_sample(glu, conv_buf[i], cw, cb, clg, clb, B, T)
            mix_in, w_mix = [o, cm], p['w_out_ab'][i]
            new_conv.append(buf)
        else:
            wts = [p['rwkv_wr'][i], p['rwkv_wk'][i], p['rwkv_wv'][i],
                   vec(p['rwkv_w0'][i]), p['rwkv_w1'][i], p['rwkv_w2'][i],
                   vec(p['rwkv_a0'][i]), p['rwkv_a1'][i], p['rwkv_a2'][i],
                   p['rwkv_g1'][i], p['rwkv_g2'][i]]
            vmix = None
            if v_first is not None:
                vmix = (vec(p['rwkv_v0'][i - 1]), p['rwkv_v1'][i - 1], p['rwkv_v2'][i - 1], v_first)
            last_u, seqs = _rwkv_proj(h, shift[i], vec(p['norm_mix'][l]), p['rwkv_mu'][i], wts, vmix, tm, B, T)
            if v_first is None:
                v_first = seqs[2]
            z, s_new = _rwkv_scan(*seqs, vec(p['rwkv_kk'][i]), vec(p['rwkv_ka'][i]), vec(p['rwkv_rk'][i]),
                                  vec(p['rwkv_lnx_g'][i]), vec(p['rwkv_lnx_b'][i]), wkv[i], B, T)
            mix_in, w_mix = [z], p['rwkv_wo'][i]
            new_wkv.append(s_new)
            new_shift.append(last_u)
        last = l == depth - 1
        res = _mix_ffn(mix_in, h, w_mix, vec(p['norm_ffn'][l]), p['w_gate'][l], p['w_up'][l], p['w_down'][l], tm_wide,
                       vec(p['norm_final']) if last else None)
        if last:
            h, y = res
        else:
            h = res
    k_all, v_all = (s.reshape(s.shape[0], B, T, A_HEADS, A_V_DIM) for s in kv_stack)
    return (y.reshape(B, T, D), k_all, v_all, jnp.stack(new_conv), jnp.stack(new_wkv), jnp.stack(new_shift))


_MATMUL_WEIGHTS = ('w_in_ab', 'w_out_ab', 'rwkv_wr', 'rwkv_wk', 'rwkv_wv', 'rwkv_w1', 'rwkv_w2', 'rwkv_a1',
                   'rwkv_a2', 'rwkv_v1', 'rwkv_v2', 'rwkv_g1', 'rwkv_g2', 'rwkv_wo', 'w_gate', 'w_up', 'w_down')


def kernel(x_prompt, x_sample, cache_k, cache_v, state_conv, state_wkv, state_shift, page_table, norm_mix, norm_ffn, norm_final, w_in_ab, diff_lambda, subln, conv_w, conv_b, conv_ln_g, conv_ln_b, w_out_ab, rwkv_mu, rwkv_wr, rwkv_wk, rwkv_wv, rwkv_w0, rwkv_w1, rwkv_w2, rwkv_a0, rwkv_a1, rwkv_a2, rwkv_v0, rwkv_v1, rwkv_v2, rwkv_g1, rwkv_g2, rwkv_kk, rwkv_ka, rwkv_rk, rwkv_lnx_g, rwkv_lnx_b, rwkv_wo, w_gate, w_up, w_down):
    p = dict(norm_mix=norm_mix, norm_ffn=norm_ffn, norm_final=norm_final, w_in_ab=w_in_ab,
             diff_lambda=diff_lambda, subln=subln, conv_w=conv_w, conv_b=conv_b, conv_ln_g=conv_ln_g,
             conv_ln_b=conv_ln_b, w_out_ab=w_out_ab, rwkv_mu=rwkv_mu, rwkv_wr=rwkv_wr, rwkv_wk=rwkv_wk,
             rwkv_wv=rwkv_wv, rwkv_w0=rwkv_w0, rwkv_w1=rwkv_w1, rwkv_w2=rwkv_w2, rwkv_a0=rwkv_a0,
             rwkv_a1=rwkv_a1, rwkv_a2=rwkv_a2, rwkv_v0=rwkv_v0, rwkv_v1=rwkv_v1, rwkv_v2=rwkv_v2,
             rwkv_g1=rwkv_g1, rwkv_g2=rwkv_g2, rwkv_kk=rwkv_kk, rwkv_ka=rwkv_ka, rwkv_rk=rwkv_rk,
             rwkv_lnx_g=rwkv_lnx_g, rwkv_lnx_b=rwkv_lnx_b, rwkv_wo=rwkv_wo, w_gate=w_gate, w_up=w_up,
             w_down=w_down)
    for name in _MATMUL_WEIGHTS:
        p[name] = p[name].astype(BF16)

    Bp, Tp, D = x_prompt.shape
    n_a, n_c = state_conv.shape[0], state_wkv.shape[0]
    zero_conv = jnp.zeros((n_a, Bp, CONV_WIDTH - 1, B_WIDTH), F32)
    zero_wkv = jnp.zeros((n_c, Bp) + state_wkv.shape[2:], F32)
    zero_shift = jnp.zeros((n_c, Bp, D), F32)
    outs_p = _trunk(x_prompt, jnp.arange(Tp), zero_conv, zero_wkv, zero_shift, p, None)

    past = page_table.shape[1] * PAGE_SIZE
    outs_s = _trunk(x_sample, past + jnp.arange(x_sample.shape[1]), state_conv, state_wkv, state_shift, p,
                    (page_table, cache_k, cache_v))
    return (outs_p[0], outs_s[0]) + tuple(outs_p[1:]) + tuple(outs_s[1:])
```

```python
import functools
import math

import jax
import jax.numpy as jnp
from jax import lax
from jax.experimental import pallas as pl
from jax.experimental.pallas import tpu as pltpu

F32 = jnp.float32
BF16 = jnp.bfloat16

NORM_EPS = 1e-6
SUBLN_EPS = 1e-5
LN_EPS = 1e-5
GN_EPS = 64e-5
ROPE_THETA = 500000.0
ROPE_DIM = 16

A_HEADS = 4
A_QK_DIM = 64
A_V_DIM = 128
A_WIDTH = 512
B_WIDTH = 512
CONV_WIDTH = 31
C_HEAD = 64
PAGE_SIZE = 128

LANES = 128
SUBLANES = 8
QUAD = 4 * C_HEAD
CHUNK = 64
ROW_TILE = 256
WIDE_ROW_TILE = 512
FLASH_TQ = 1024
FLASH_TK = 1024
FLASH_ROWS = 512
PAGES_PER_STEP = 32
SCAN_SEQS = 2
SCAN_QUADS = 4
CONV_HALO = 32
VMEM_LIMIT = 56 * 1024 * 1024


def _cp(*sem):
    return pltpu.CompilerParams(dimension_semantics=sem, vmem_limit_bytes=VMEM_LIMIT)


def _mm(a, b):
    return jnp.dot(a.astype(BF16), b.astype(BF16), preferred_element_type=F32)


def _mm_nt(a, b):
    return lax.dot_general(a.astype(BF16), b.astype(BF16), (((1,), (1,)), ((), ())), preferred_element_type=F32)


def _mm_tn(a, b):
    return lax.dot_general(a.astype(BF16), b.astype(BF16), (((0,), (0,)), ((), ())), preferred_element_type=F32)


def _split2(x):
    hi = x.astype(BF16)
    lo = (x - hi.astype(F32)).astype(BF16)
    return hi, lo


def _sigmoid(x):
    return 1.0 / (1.0 + jnp.exp(-x))


def _rms(x, g, eps):
    ms = jnp.mean(x * x, axis=-1, keepdims=True)
    return x * lax.rsqrt(ms + eps) * g


def _inproj_kernel(x_ref, g_ref, w_ref, c_ref, s1_ref, s2_ref, *rest):
    q_ref, k_ref, v_ref, glu_ref = rest[-4:]
    u = _rms(x_ref[...], g_ref[...], NORM_EPS).astype(BF16)

    def proj(j):
        return jnp.dot(u, w_ref[:, j * A_WIDTH:(j + 1) * A_WIDTH], preferred_element_type=F32)

    ct = jnp.concatenate([c_ref[...]] * A_HEADS, axis=1)
    s1t = jnp.concatenate([s1_ref[...]] * A_HEADS, axis=1)
    s2t = jnp.concatenate([s2_ref[...]] * A_HEADS, axis=1)
    half = ROPE_DIM // 2

    def rope(z):
        return z * ct + pltpu.roll(z, A_WIDTH - half, 1) * s1t + pltpu.roll(z, half, 1) * s2t

    q_ref[...] = rope(proj(0))
    tm = x_ref.shape[0]
    n_prev = k_ref.shape[0] - 1
    if n_prev:
        k_ref[0:n_prev] = rest[0][...]
        v_ref[0:n_prev] = rest[1][...]
    kr = rope(proj(1))
    vv = proj(2)
    for h in range(A_HEADS):
        k_ref[n_prev, pl.ds(h, tm, stride=A_HEADS), :] = kr[:, h * A_V_DIM:(h + 1) * A_V_DIM]
        v_ref[n_prev, pl.ds(h, tm, stride=A_HEADS), :] = vv[:, h * A_V_DIM:(h + 1) * A_V_DIM]
    glu_ref[...] = proj(3) * _sigmoid(proj(4))


def _in_proj_ab(h, g, w_bf, tabs, tm, kv_prev):
    M, D = h.shape
    ntab = tabs[0].shape[0] // tm
    n_prev = 0 if kv_prev is None else kv_prev[0].shape[0]
    tab_spec = pl.BlockSpec((tm, LANES), lambda i: (i % ntab, 0))
    row_out = pl.BlockSpec((tm, A_WIDTH), lambda i: (i, 0))
    kv_out = pl.BlockSpec((n_prev + 1, tm * A_HEADS, A_V_DIM), lambda i: (0, i, 0))
    kv_sds = jax.ShapeDtypeStruct((n_prev + 1, M * A_HEADS, A_V_DIM), F32)
    row_sds = jax.ShapeDtypeStruct((M, A_WIDTH), F32)
    ins = [h, g, w_bf, *tabs]
    in_specs = [pl.BlockSpec((tm, D), lambda i: (i, 0)),
                pl.BlockSpec((1, D), lambda i: (0, 0)),
                pl.BlockSpec(w_bf.shape, lambda i: (0, 0)),
                tab_spec, tab_spec, tab_spec]
    if n_prev:
        ins += list(kv_prev)
        in_specs += [pl.BlockSpec((n_prev, tm * A_HEADS, A_V_DIM), lambda i: (0, i, 0))] * 2
    return pl.pallas_call(
        _inproj_kernel,
        grid=(M // tm,),
        in_specs=in_specs,
        out_specs=[row_out, kv_out, kv_out, row_out],
        out_shape=[row_sds, kv_sds, kv_sds, row_sds],
        compiler_params=_cp("parallel"),
        name="in_proj_ab",
    )(*ins)


def _rope_tables(pos):
    half = ROPE_DIM // 2
    inv = 1.0 / (ROPE_THETA ** (jnp.arange(0, ROPE_DIM, 2, dtype=F32) / ROPE_DIM))
    d = jnp.arange(A_V_DIM) % A_QK_DIM
    ang = pos.astype(F32)[:, None] * inv[d % half][None, :]
    cos, sin = jnp.cos(ang), jnp.sin(ang)
    d = d[None, :]
    c = jnp.where(d < ROPE_DIM, cos, 1.0)
    s1 = jnp.where(d < half, -sin, 0.0)
    s2 = jnp.where((d >= half) & (d < ROPE_DIM), sin, 0.0)
    return c, s1, s2


def _diff_lambda(lam_ref, lam_init):
    lp = lam_ref[...]
    s1 = jnp.sum(lp[0:1] * lp[1:2], axis=-1, keepdims=True)
    s2 = jnp.sum(lp[2:3] * lp[3:4], axis=-1, keepdims=True)
    return jnp.exp(s1) - jnp.exp(s2) + lam_init


def _subln(o, g, lam_init):
    return _rms(o, g, SUBLN_EPS) * (1.0 - lam_init)


def _flash_kernel(qi_ref, ki_ref, lam_ref, sub_ref, q_ref, k_ref, v_ref, o_ref, m_ref, l_ref, acc_ref, *,
                  lam_init, tq, tk, rb):
    pair = pl.program_id(1)
    qi = qi_ref[pair]
    ki = ki_ref[pair]
    nrb = tq // rb

    @pl.when(ki == 0)
    def _():
        m_ref[...] = jnp.full(m_ref.shape, -jnp.inf, F32)
        l_ref[...] = jnp.zeros(l_ref.shape, F32)
        acc_ref[...] = jnp.zeros(acc_ref.shape, F32)

    def chain(h, r, masked, kv):
        slab = h * nrb + r
        kb, vb = kv[h]
        if masked is not None and rb - masked < tk:
            kb, vb = kb[:rb - masked], vb[:rb - masked]
        q = q_ref[r * rb:(r + 1) * rb, h * A_V_DIM:(h + 1) * A_V_DIM] * (A_QK_DIM ** -0.5 * math.log2(math.e))
        lane = lax.broadcasted_iota(jnp.int32, q.shape, 1)
        q12 = jnp.concatenate([jnp.where(lane < A_QK_DIM, q, 0.0), jnp.where(lane >= A_QK_DIM, q, 0.0)], axis=0)
        s = _mm_nt(q12, kb)
        yield
        if masked is not None:
            row = lax.broadcasted_iota(jnp.int32, s.shape, 0)
            col = lax.broadcasted_iota(jnp.int32, s.shape, 1)
            row = jnp.where(row >= rb, row - rb, row)
            s = jnp.where(col + masked <= row, s, -jnp.inf)
        m_prev = m_ref[slab]
        m_new = jnp.maximum(m_prev, jnp.max(s, axis=1, keepdims=True))
        alpha = jnp.exp2(m_prev - m_new)
        p = jnp.exp2(s - jnp.tile(m_new, (1, s.shape[1] // LANES)))
        pv = jnp.dot(p.astype(BF16), vb, preferred_element_type=F32)
        l_ref[slab] = alpha * l_ref[slab] + pv[:, A_V_DIM:]
        acc_ref[slab] = alpha * acc_ref[slab] + pv[:, :A_V_DIM]
        m_ref[slab] = m_new

    def step(rel):
        ones = jnp.ones((tk, LANES), BF16)
        kv = [(k_ref[0, pl.ds(h, tk, stride=A_HEADS), :].astype(BF16),
               jnp.concatenate([v_ref[0, pl.ds(h, tk, stride=A_HEADS), :].astype(BF16), ones], axis=1))
              for h in range(A_HEADS)]
        chains = []
        for h in range(A_HEADS):
            for r in range(nrb):
                off = None if rel is None else rel * tk - r * rb
                if off is not None and off > rb - 1:
                    continue
                if off is not None and off + tk - 1 <= 0:
                    off = None
                chains.append(chain(h, r, off, kv))
        next(chains[0])
        for i, ch in enumerate(chains):
            if i + 1 < len(chains):
                next(chains[i + 1])
            for _ in ch:
                pass

    ratio = tq // tk
    for rel in range(ratio):
        @pl.when(ki == qi * ratio + rel)
        def _(rel=rel):
            step(rel)

    @pl.when(ki < qi * ratio)
    def _():
        step(None)

    @pl.when(ki == (qi + 1) * ratio - 1)
    def _():
        lam = _diff_lambda(lam_ref, lam_init)
        for h in range(A_HEADS):
            for r in range(nrb):
                o = acc_ref[h * nrb + r] / l_ref[h * nrb + r]
                o_ref[r * rb:(r + 1) * rb, h * A_V_DIM:(h + 1) * A_V_DIM] = _subln(
                    o[:rb] - lam * o[rb:], sub_ref[...], lam_init)


def _flash_prompt(q, k, v, lam_p, subln, lam_init, B, T, tq, tk, rb):
    layer = k.shape[0] - 1
    nq, nk, ratio = T // tq, T // tk, tq // tk
    pairs = [(i, j) for i in range(nq) for j in range((i + 1) * ratio)]
    qi_tab = jnp.array([pq for pq, _ in pairs], jnp.int32)
    ki_tab = jnp.array([pk for _, pk in pairs], jnp.int32)
    kern = functools.partial(_flash_kernel, lam_init=lam_init, tq=tq, tk=tk, rb=rb)
    kv_spec = pl.BlockSpec((1, tk * A_HEADS, A_V_DIM), lambda b, s, qt, kt: (layer, b * nk + kt[s], 0))
    q_spec = pl.BlockSpec((tq, A_WIDTH), lambda b, s, qt, kt: (b * nq + qt[s], 0))
    slabs = A_HEADS * (tq // rb)
    return pl.pallas_call(
        kern,
        grid_spec=pltpu.PrefetchScalarGridSpec(
            num_scalar_prefetch=2,
            grid=(B, len(pairs)),
            in_specs=[pl.BlockSpec(lam_p.shape, lambda b, s, qt, kt: (0, 0)),
                      pl.BlockSpec((1, A_V_DIM), lambda b, s, qt, kt: (0, 0)),
                      q_spec, kv_spec, kv_spec],
            out_specs=q_spec,
            scratch_shapes=[pltpu.VMEM((slabs, 2 * rb, LANES), F32), pltpu.VMEM((slabs, 2 * rb, LANES), F32),
                            pltpu.VMEM((slabs, 2 * rb, A_V_DIM), F32)]),
        out_shape=jax.ShapeDtypeStruct((B * T, A_WIDTH), F32),
        compiler_params=_cp("parallel", "arbitrary"),
        name="diff_attn_prompt",
    )(qi_tab, ki_tab, lam_p, subln, q, k, v)


def _paged_kernel(pt_ref, lam_ref, sub_ref, q_ref, kn_ref, vn_ref, *rest, lam_init, pp, tn):
    k_refs = rest[:pp]
    v_refs = rest[pp:2 * pp]
    o_ref, m_ref, l_ref, acc_ref = rest[2 * pp:]
    j = pl.program_id(1)
    hrows = 2 * SUBLANES

    @pl.when(j == 0)
    def _():
        m_ref[...] = jnp.full(m_ref.shape, -jnp.inf, F32)
        l_ref[...] = jnp.zeros(l_ref.shape, F32)
        acc_ref[...] = jnp.zeros(acc_ref.shape, F32)

    qb = (q_ref[0] * (A_QK_DIM ** -0.5)).astype(BF16)

    def scores(keys_of):
        return jnp.concatenate([_mm_nt(qb[h * hrows:(h + 1) * hrows], keys_of(h)) for h in range(A_HEADS)], axis=0)

    def update(s, vals_of, mask=None):
        if mask is not None:
            s = jnp.where(mask(s.shape), s, -jnp.inf)
        m_prev = m_ref[...]
        m_new = jnp.maximum(m_prev, jnp.max(s, axis=1, keepdims=True))
        alpha = jnp.exp(m_prev - m_new)
        p = jnp.exp(s - m_new)
        l_ref[...] = alpha * l_ref[...] + jnp.sum(p, axis=1, keepdims=True)
        pv = jnp.concatenate([_mm(p[h * hrows:(h + 1) * hrows], vals_of(h)) for h in range(A_HEADS)], axis=0)
        acc_ref[...] = alpha * acc_ref[...] + pv
        m_ref[...] = m_new

    rows_of = lambda ref, h: ref[0, 0, pl.ds(h, PAGE_SIZE, stride=A_HEADS), :]
    gather = lambda refs: (lambda h: jnp.concatenate([rows_of(r, h) for r in refs], axis=0))
    half = max(pp // 2, 1)
    groups = [(k_refs[lo:lo + half], v_refs[lo:lo + half]) for lo in range(0, pp, half)]
    all_scores = [scores(gather(kg)) for kg, _ in groups]
    for s_g, (_, vg) in zip(all_scores, groups):
        update(s_g, gather(vg))

    @pl.when(j == pl.num_programs(1) - 1)
    def _():
        def causal(shape):
            tok = lax.broadcasted_iota(jnp.int32, shape, 0) % SUBLANES
            col = lax.broadcasted_iota(jnp.int32, shape, 1)
            return (col <= tok) & (col < tn)

        head = lambda ref, h: ref[0, pl.ds(h, SUBLANES, stride=A_HEADS), :]
        update(scores(lambda h: head(kn_ref, h)), lambda h: head(vn_ref, h), causal)
        o = acc_ref[...] / l_ref[...]
        lam = _diff_lambda(lam_ref, lam_init)
        for h in range(A_HEADS):
            d = o[h * hrows:h * hrows + SUBLANES] - lam * o[h * hrows + SUBLANES:(h + 1) * hrows]
            o_ref[0, h] = _subln(d, sub_ref[...], lam_init)


def _paged_sample(page_table, q, k_new, v_new, cache_k, cache_v, layer, lam_p, subln, lam_init, Bd, tn):
    n_pages = page_table.shape[1]
    pp = math.gcd(n_pages, PAGES_PER_STEP)
    tpad = SUBLANES - tn
    lane_map = jnp.arange(A_V_DIM) // A_QK_DIM
    keep = (lane_map[None, :] == jnp.arange(2)[:, None]).astype(F32)
    q4 = q.reshape(Bd, tn, A_HEADS, A_V_DIM).transpose(0, 2, 1, 3)
    qm = q4[:, :, None, :, :] * keep[None, None, :, None, :]
    qm = jnp.pad(qm, ((0, 0), (0, 0), (0, 0), (0, tpad), (0, 0))).reshape(Bd, A_HEADS * 2 * SUBLANES, A_V_DIM)
    pad = ((0, 0), (0, tpad * A_HEADS), (0, 0))
    kn = jnp.pad(k_new.reshape(Bd, tn * A_HEADS, A_V_DIM), pad)
    vn = jnp.pad(v_new.reshape(Bd, tn * A_HEADS, A_V_DIM), pad)

    page_rows = PAGE_SIZE * A_HEADS
    cache_k = cache_k.reshape(cache_k.shape[:2] + (page_rows, A_V_DIM))
    cache_v = cache_v.reshape(cache_v.shape[:2] + (page_rows, A_V_DIM))

    def page_spec(i):
        return pl.BlockSpec((1, 1, page_rows, A_V_DIM), lambda b, j, pt: (layer, pt[b, j * pp + i], 0, 0))

    per_b = lambda b, j, pt: (b, 0, 0)
    nrow = A_HEADS * 2 * SUBLANES
    kern = functools.partial(_paged_kernel, lam_init=lam_init, pp=pp, tn=tn)
    out = pl.pallas_call(
        kern,
        grid_spec=pltpu.PrefetchScalarGridSpec(
            num_scalar_prefetch=1,
            grid=(Bd, n_pages // pp),
            in_specs=[pl.BlockSpec(lam_p.shape, lambda b, j, pt: (0, 0)),
                      pl.BlockSpec((1, A_V_DIM), lambda b, j, pt: (0, 0)),
                      pl.BlockSpec((1, nrow, A_V_DIM), per_b),
                      pl.BlockSpec((1, SUBLANES * A_HEADS, A_V_DIM), per_b),
                      pl.BlockSpec((1, SUBLANES * A_HEADS, A_V_DIM), per_b)]
                     + [page_spec(i) for i in range(pp)] * 2,
            out_specs=pl.BlockSpec((1, A_HEADS, SUBLANES, A_V_DIM), lambda b, j, pt: (b, 0, 0, 0)),
            scratch_shapes=[pltpu.VMEM((nrow, 1), F32), pltpu.VMEM((nrow, 1), F32),
                            pltpu.VMEM((nrow, A_V_DIM), F32)]),
        out_shape=jax.ShapeDtypeStruct((Bd, A_HEADS, SUBLANES, A_V_DIM), F32),
        compiler_params=_cp("parallel", "arbitrary"),
        name="diff_attn_paged",
    )(page_table, lam_p, subln, qm, kn, vn, *([cache_k] * pp), *([cache_v] * pp))
    return out[:, :, :tn].transpose(0, 2, 1, 3).reshape(Bd * tn, A_WIDTH)


def _conv_post(y, b_ref, lg_ref, lb_ref):
    y = y + b_ref[...]
    mu = jnp.mean(y, axis=-1, keepdims=True)
    d = y - mu
    var = jnp.mean(d * d, axis=-1, keepdims=True)
    z = d * lax.rsqrt(var + LN_EPS) * lg_ref[...] + lb_ref[...]
    return z * _sigmoid(z)


def _conv_kernel(glu_ref, halo_ref, buf_ref, w_ref, b_ref, lg_ref, lb_ref, o_ref, xp_ref, xs_ref, *, tt, sub):
    j = pl.program_id(1)
    rows = CONV_HALO + tt
    xp_ref[0:CONV_HALO, :] = jnp.where(j == 0, buf_ref[0], halo_ref[...])
    xp_ref[CONV_HALO:, :] = glu_ref[...]
    for ph in range(SUBLANES):
        xs_ref[ph, 0:rows - ph, :] = xp_ref[ph:rows, :]
    lead = CONV_HALO - (CONV_WIDTH - 1)
    w = w_ref[...]
    for r0 in range(0, tt, sub):
        acc = None
        for tap in range(CONV_WIDTH):
            ph, base = (lead + tap) % SUBLANES, (lead + tap) // SUBLANES * SUBLANES
            term = xs_ref[ph, r0 + base:r0 + base + sub, :] * w[tap:tap + 1]
            acc = term if acc is None else acc + term
        o_ref[r0:r0 + sub, :] = _conv_post(acc, b_ref, lg_ref, lb_ref)


def _conv_prompt(glu, buf, w, b, lg, lb, B, T, tt):
    nt = T // tt
    hpt = tt // CONV_HALO
    bufp = jnp.pad(buf, ((0, 0), (CONV_HALO - (CONV_WIDTH - 1), 0), (0, 0)))
    vec = pl.BlockSpec((1, B_WIDTH), lambda bb, j: (0, 0))
    kern = functools.partial(_conv_kernel, tt=tt, sub=min(tt, 64))
    return pl.pallas_call(
        kern,
        grid=(B, nt),
        in_specs=[pl.BlockSpec((tt, B_WIDTH), lambda bb, j: (bb * nt + j, 0)),
                  pl.BlockSpec((CONV_HALO, B_WIDTH), lambda bb, j: (jnp.maximum((bb * nt + j) * hpt - 1, 0), 0)),
                  pl.BlockSpec((1, CONV_HALO, B_WIDTH), lambda bb, j: (bb, 0, 0)),
                  pl.BlockSpec((CONV_WIDTH, B_WIDTH), lambda bb, j: (0, 0)),
                  vec, vec, vec],
        out_specs=pl.BlockSpec((tt, B_WIDTH), lambda bb, j: (bb * nt + j, 0)),
        out_shape=jax.ShapeDtypeStruct((B * T, B_WIDTH), F32),
        scratch_shapes=[pltpu.VMEM((CONV_HALO + tt, B_WIDTH), F32),
                        pltpu.VMEM((SUBLANES, CONV_HALO + tt, B_WIDTH), F32)],
        compiler_params=_cp("parallel", "arbitrary"),
        name="conv_prompt",
    )(glu, glu, bufp, w, b, lg, lb)


def _conv_step_kernel(xp_ref, w_ref, b_ref, lg_ref, lb_ref, o_ref, *, tn):
    w = w_ref[...]
    for t in range(tn):
        acc = xp_ref[t] * w[0:1]
        for tap in range(1, CONV_WIDTH):
            acc = acc + xp_ref[t + tap] * w[tap:tap + 1]
        o_ref[t] = _conv_post(acc, b_ref, lg_ref, lb_ref)


def _conv_sample(glu, buf, w, b, lg, lb, Bd, tn):
    xp = jnp.concatenate([buf, glu.reshape(Bd, tn, B_WIDTH)], axis=1)
    out = pl.pallas_call(
        functools.partial(_conv_step_kernel, tn=tn),
        out_shape=jax.ShapeDtypeStruct((tn, Bd, B_WIDTH), F32),
        compiler_params=pltpu.CompilerParams(vmem_limit_bytes=VMEM_LIMIT),
        name="conv_sample",
    )(xp.transpose(1, 0, 2), w, b, lg, lb)
    return out.transpose(1, 0, 2).reshape(Bd * tn, B_WIDTH), xp[:, -(CONV_WIDTH - 1):]


def _mix_ffn_kernel(*refs, n_in, final):
    x_refs = refs[:n_in]
    h_ref, wo_ref, g_ref, wg_ref, wu_ref, wd_ref = refs[n_in:n_in + 6]
    rest = refs[n_in + 6:]
    h = h_ref[...]
    k0 = 0
    for x_ref in x_refs:
        kw = x_ref.shape[1]
        h = h + jnp.dot(x_ref[...].astype(BF16), wo_ref[k0:k0 + kw, :], preferred_element_type=F32)
        k0 += kw
    f = _rms(h, g_ref[...], NORM_EPS).astype(BF16)
    gate = jnp.dot(f, wg_ref[...], preferred_element_type=F32)
    up = jnp.dot(f, wu_ref[...], preferred_element_type=F32)
    act = (gate * _sigmoid(gate) * up).astype(BF16)
    out = h + jnp.dot(act, wd_ref[...], preferred_element_type=F32)
    if final:
        gf_ref, o_ref, y_ref = rest
        y_ref[...] = _rms(out, gf_ref[...], NORM_EPS)
    else:
        (o_ref,) = rest
    o_ref[...] = out


def _mix_ffn(xs, h, wo, g, wg, wu, wd, tm, g_final=None):
    M, D = h.shape
    final = g_final is not None
    row = pl.BlockSpec((tm, D), lambda i: (i, 0))
    vec = pl.BlockSpec((1, D), lambda i: (0, 0))
    full = lambda a: pl.BlockSpec(a.shape, lambda i: (0, 0), pipeline_mode=pl.Buffered(1))
    ins = list(xs) + [h, wo, g, wg, wu, wd] + ([g_final] if final else [])
    in_specs = ([pl.BlockSpec((tm, x.shape[1]), lambda i: (i, 0)) for x in xs]
                + [row, full(wo), vec, full(wg), full(wu), full(wd)] + ([vec] if final else []))
    sds = jax.ShapeDtypeStruct((M, D), F32)
    return pl.pallas_call(
        functools.partial(_mix_ffn_kernel, n_in=len(xs), final=final),
        grid=(M // tm,),
        in_specs=in_specs,
        out_specs=[row, row] if final else row,
        out_shape=[sds, sds] if final else sds,
        compiler_params=_cp("parallel"),
        name="mix_ffn",
    )(*ins)


def _rwkv_proj_kernel(*refs, has_vmix, seq_tiles, short_t):
    it = iter(refs)
    h_ref, hprev_ref, shift_ref, g_ref, mu_ref = next(it), next(it), next(it), next(it), next(it)
    wr_ref, wk_ref, wv_ref = next(it), next(it), next(it)
    w0_ref, w1_ref, w2_ref = next(it), next(it), next(it)
    a0_ref, a1_ref, a2_ref = next(it), next(it), next(it)
    g1_ref, g2_ref = next(it), next(it)
    if has_vmix:
        v0_ref, v1_ref, v2_ref, vfirst_ref = next(it), next(it), next(it), next(it)
    u_ref, r_ref, k_ref, v_ref, ld_ref, a_ref, gg_ref = (next(it) for _ in range(7))

    i = pl.program_id(0)
    u = _rms(h_ref[...], g_ref[...], NORM_EPS)
    keep = u_ref.shape[1]
    u_ref[0] = u[u.shape[0] - keep:]
    rolled = pltpu.roll(u, 1, 0)
    row = lax.broadcasted_iota(jnp.int32, u.shape, 0)
    if short_t:
        prev = jnp.where(row % short_t == 0, shift_ref[...], rolled)
    else:
        up8 = _rms(hprev_ref[...], g_ref[...], NORM_EPS)
        first = jnp.where(i % seq_tiles == 0, shift_ref[0], up8[SUBLANES - 1:SUBLANES])
        prev = jnp.where(row == 0, first, rolled)
    xx = prev - u
    mu = mu_ref[...]
    xr, xw, xk, xv, xa, xg = (u + xx * mu[j:j + 1] for j in range(6))

    r_ref[...] = _mm(xr, wr_ref[...])
    k_ref[...] = _mm(xk, wk_ref[...])
    v = _mm(xv, wv_ref[...])
    wl = w0_ref[...] + _mm(jnp.tanh(_mm(xw, w1_ref[...])), w2_ref[...])
    z = -wl
    w = -(jnp.maximum(z, 0.0) + jnp.log(1.0 + jnp.exp(-jnp.abs(z)))) - 0.5
    ld_ref[...] = -jnp.exp(w)
    if has_vmix:
        mix = _sigmoid(v0_ref[...] + _mm(_mm(xv, v1_ref[...]), v2_ref[...]))
        v = v + (vfirst_ref[...] - v) * mix
    v_ref[...] = v
    a_ref[...] = _sigmoid(a0_ref[...] + _mm(_mm(xa, a1_ref[...]), a2_ref[...]))
    gg_ref[...] = _mm(_sigmoid(_mm(xg, g1_ref[...])), g2_ref[...])


def _rwkv_proj(h, shift, g, mu, wts, vmix, tm, B, T):
    M, D = h.shape
    short = T < tm
    row = pl.BlockSpec((tm, D), lambda i: (i, 0))
    vec = pl.BlockSpec((1, D), lambda i: (0, 0))
    full = lambda a: pl.BlockSpec(a.shape, lambda i: (0, 0))
    if short:
        seq_tiles = 1
        shift_rows = jnp.zeros((B, T, D), F32).at[:, 0].set(shift).reshape(M, D)
        shift_in, shift_spec = shift_rows, row
        hprev_spec = pl.BlockSpec((SUBLANES, D), lambda i: (0, 0))
    else:
        seq_tiles = T // tm
        shift_in = shift.reshape(B, 1, D)
        shift_spec = pl.BlockSpec((1, 1, D), lambda i: (i // seq_tiles, 0, 0))
        hprev_spec = pl.BlockSpec((SUBLANES, D), lambda i: (jnp.maximum(i * (tm // SUBLANES) - 1, 0), 0))
    ins = [h, h, shift_in, g, mu] + list(wts)
    in_specs = [row, hprev_spec, shift_spec, vec, full(mu)] + [full(a) for a in wts]
    if vmix is not None:
        v0, v1, v2, v_first = vmix
        ins += [v0, v1, v2, v_first]
        in_specs += [vec, full(v1), full(v2), row]
    sds = jax.ShapeDtypeStruct((M, D), F32)
    keep = tm if short else SUBLANES
    kern = functools.partial(_rwkv_proj_kernel, has_vmix=vmix is not None, seq_tiles=seq_tiles,
                             short_t=T if short else 0)
    tail, *seqs = pl.pallas_call(
        kern,
        grid=(M // tm,),
        in_specs=in_specs,
        out_specs=[pl.BlockSpec((1, keep, D), lambda i: (i, 0, 0))] + [row] * 6,
        out_shape=[jax.ShapeDtypeStruct((M // tm, keep, D), F32)] + [sds] * 6,
        compiler_params=_cp("parallel"),
        name="rwkv_proj",
    )(*ins)
    if short:
        last_u = tail.reshape(B, T, D)[:, -1]
    else:
        last_u = tail.reshape(B, seq_tiles, keep, D)[:, -1, -1]
    return last_u, seqs


def _scan_kernel(r_ref, k_ref, v_ref, ld_ref, a_ref, g_ref, kkp_ref, ka_ref, rk_ref, lg_ref, lb_ref, h0_ref,
                 z_ref, hout_ref, bdh_ref, pad_ref, *, nb, nqd):
    c = pl.program_id(2)
    rq = lax.broadcasted_iota(jnp.int32, (QUAD, QUAD), 0)
    cq = lax.broadcasted_iota(jnp.int32, (QUAD, QUAD), 1)
    same_head = (rq // C_HEAD) == (cq // C_HEAD)
    units = [(i, qd) for i in range(nb) for qd in range(nqd)]

    @pl.when(c == 0)
    def _():
        for u, (i, qd) in enumerate(units):
            hc = h0_ref[i, qd]
            bdh_ref[u] = jnp.where(same_head, jnp.concatenate([hc] * 4, axis=1), 0.0)

    block_ones = jnp.where(same_head, 1.0, 0.0).astype(BF16)
    gens = [_scan_chunk(u, i, qd, same_head, r_ref, k_ref, v_ref, ld_ref, a_ref, g_ref, kkp_ref, ka_ref, rk_ref,
                        lg_ref, lb_ref, z_ref, bdh_ref, pad_ref) for u, (i, qd) in enumerate(units)]
    hnews = [None] * len(units)
    answers = [None] * len(units)
    while any(h is None for h in hnews):
        asks = {}
        for u in range(len(units)):
            if hnews[u] is None:
                try:
                    ask = gens[u].send(answers[u])
                    if ask is not None:
                        asks[u] = ask
                except StopIteration as done:
                    hnews[u] = done.value
        answers = [None] * len(units)
        if asks:
            sums = _mm(jnp.concatenate(list(asks.values()), axis=0), block_ones)
            row = 0
            for u, ask in asks.items():
                answers[u] = sums[row:row + ask.shape[0]]
                row += ask.shape[0]

    @pl.when(c == pl.num_programs(2) - 1)
    def _():
        for u, (i, qd) in enumerate(units):
            hc = hnews[u][:, 0:C_HEAD]
            for hh in range(1, 4):
                hc = hc + hnews[u][:, hh * C_HEAD:(hh + 1) * C_HEAD]
            hout_ref[i, qd] = hc


def _scan_chunk(u, i, qd, same_head, r_ref, k_ref, v_ref, ld_ref, a_ref, g_ref, kkp_ref, ka_ref, rk_ref, lg_ref,
                lb_ref, z_ref, bdh_ref, pad_ref):
    C = CHUNK
    lanes = slice(qd * QUAD, (qd + 1) * QUAD)
    t_valid = r_ref.shape[1]

    def bd(x):
        return jnp.where(same_head, jnp.concatenate([x] * 4, axis=0), 0.0)

    def load(j, ref):
        if t_valid == C:
            return ref[i, :, lanes]
        pad_ref[j, u] = jnp.zeros((C, QUAD), F32)
        pad_ref[j, u, 0:t_valid, :] = ref[i, :, lanes]
        return pad_ref[j, u]

    r, k, v, ld, a = (load(j, ref) for j, ref in enumerate((r_ref, k_ref, v_ref, ld_ref, a_ref)))
    kkp, ka, rkp, lg, lb = (ref[:, lanes] for ref in (kkp_ref, ka_ref, rk_ref, lg_ref, lb_ref))

    tt = lax.broadcasted_iota(jnp.int32, (C, C), 0)
    ts = lax.broadcasted_iota(jnp.int32, (C, C), 1)
    tri = jnp.where(ts <= tt, 1.0, 0.0).astype(BF16)
    lc = sum(jnp.dot(tri, p, preferred_element_type=F32) for p in _split2(ld))
    lend = lc[C - 1:C, :]
    gam = jnp.exp(lc)
    gam_prev = jnp.exp(lc - ld)
    gam_inv = jnp.exp(-lc)
    gam_end = jnp.exp(lend - lc)
    yield

    kk = k * kkp
    k2 = k * (1.0 + (a - 1.0) * ka)
    head_sums = yield jnp.concatenate([kk * kk, r * k2 * rkp], axis=0)
    ss, rk_sum = head_sums[:C], head_sums[C:]
    kkn = kk / jnp.maximum(jnp.sqrt(ss), 1e-12)
    bv = kkn * a
    at = -kkn * gam_prev
    rt = r * gam
    lhs = jnp.concatenate([at, rt], axis=0)
    ab = _mm_nt(lhs, bd(bv * gam_inv))
    ak = _mm_nt(lhs, bd(k2 * gam_inv))
    yield
    t_i = lax.broadcasted_iota(jnp.int32, (C, QUAD), 0)
    s_i = lax.broadcasted_iota(jnp.int32, (C, QUAD), 1) % C_HEAD
    strict = s_i < t_i
    incl = s_i <= t_i
    a_ab = jnp.where(strict, ab[:C], 0.0)
    m_rb = jnp.where(incl, ab[C:], 0.0)
    a_ak = jnp.where(strict, ak[:C], 0.0)
    m_rk = jnp.where(incl, ak[C:], 0.0)

    p = a_ab
    tinv = jnp.where(s_i == t_i, 1.0, 0.0) + p
    n_sq = max((t_valid - 1).bit_length(), 1) - 1
    p = _mm(p, bd(p))
    yield
    for jj in range(n_sq):
        w = bd(p)
        if jj < n_sq - 1:
            pt = _mm(jnp.concatenate([p, tinv], axis=0), w)
            p = pt[:C]
            tinv = tinv + pt[C:]
        else:
            tinv = tinv + _mm(tinv, w)
        yield

    avrv = _mm(jnp.concatenate([a_ak, m_rk], axis=0), bd(v))
    x2, rv = avrv[:C], avrv[C:]
    yield
    w1 = _mm(tinv, bd(at))
    w2 = _mm(tinv, bd(x2))
    yield
    y1 = rt + _mm(m_rb, bd(w1))
    y2 = rv + _mm(m_rb, bd(w2))
    yield
    be = bv * gam_end
    ke = k2 * gam_end
    fresh = _mm_tn(jnp.concatenate([w2, v], axis=0), jnp.concatenate([be, ke], axis=0))
    yield
    gmat = bdh_ref[u]
    y = _mm_nt(y1, gmat) + y2
    ut = _mm_nt(gmat, w1)
    yield
    carried = _mm(ut, be)
    hnew = gmat * jnp.exp(lend) + jnp.where(same_head, carried + fresh, 0.0)
    bdh_ref[u] = hnew
    yield

    inv_n = 1.0 / C_HEAD
    mean = (yield y) * inv_n
    d = y - mean
    var = (yield d * d) * inv_n
    yn = d * lax.rsqrt(var + GN_EPS) * lg + lb
    bonus = rk_sum * v
    z_ref[i, :, lanes] = ((yn + bonus)[0:t_valid]) * g_ref[i, :, lanes]
    return hnew


def _rwkv_scan(r, k, v, ld, a, g, kkp, ka, rk, lg, lb, state, B, T):
    D = r.shape[1]
    nq = D // QUAD
    tc = min(T, CHUNK)
    nc = T // tc
    nb = math.gcd(B, SCAN_SEQS)
    nqd = math.gcd(nq, SCAN_QUADS)
    h0 = state.reshape(B, nq, QUAD, C_HEAD)
    seq = pl.BlockSpec((nb, tc, nqd * QUAD), lambda b, q, c: (b, c, q))
    vec = pl.BlockSpec((1, nqd * QUAD), lambda b, q, c: (0, q))
    st = pl.BlockSpec((nb, nqd, QUAD, C_HEAD), lambda b, q, c: (b, q, 0, 0))
    z, hout = pl.pallas_call(
        functools.partial(_scan_kernel, nb=nb, nqd=nqd),
        grid=(B // nb, nq // nqd, nc),
        in_specs=[seq] * 6 + [vec] * 5 + [st],
        out_specs=[seq, st],
        out_shape=[jax.ShapeDtypeStruct((B, T, D), F32), jax.ShapeDtypeStruct((B, nq, QUAD, C_HEAD), F32)],
        scratch_shapes=[pltpu.VMEM((nb * nqd, QUAD, QUAD), F32),
                        pltpu.VMEM((5, nb * nqd, CHUNK, QUAD) if tc < CHUNK else (1, 1, SUBLANES, LANES), F32)],
        compiler_params=_cp("parallel", "parallel", "arbitrary"),
        name="rwkv_scan",
    )(*(s.reshape(B, T, D) for s in (r, k, v, ld, a, g)), kkp, ka, rk, lg, lb, h0)
    new_state = hout.reshape(B, D // C_HEAD, C_HEAD, C_HEAD)
    return z.reshape(B * T, D), new_state


def _row_tile(M, pref):
    return pref if M % pref == 0 else M


def _trunk(x, pos, conv_buf, wkv, shift, p, paged):
    B, T, D = x.shape
    M = B * T
    tm = _row_tile(M, ROW_TILE)
    tm_wide = _row_tile(T if paged is None else M, WIDE_ROW_TILE)
    h = x.reshape(M, D)
    depth = p['norm_mix'].shape[0]
    tabs = _rope_tables(pos)
    if paged is not None:
        tabs = tuple(jnp.tile(t, (B, 1)) for t in tabs)
    vec = lambda a: a.reshape(1, -1)
    new_conv, new_wkv, new_shift = [], [], []
    kv_stack = None
    v_first = None
    y = None
    for l in range(depth):
        i = l // 2
        if l % 2 == 0:
            q, k, v, glu = _in_proj_ab(h, vec(p['norm_mix'][l]), p['w_in_ab'][i], tabs, tm_wide, kv_stack)
            kv_stack = (k, v)
            lam_init = 0.8 - 0.6 * math.exp(-0.3 * l)
            lam_p, sub = p['diff_lambda'][i], vec(p['subln'][i])
            cw, cb = p['conv_w'][i], vec(p['conv_b'][i])
            clg, clb = vec(p['conv_ln_g'][i]), vec(p['conv_ln_b'][i])
            if paged is None:
                tk = _row_tile(T, FLASH_TK)
                tq = _row_tile(T, FLASH_TQ)
                o = _flash_prompt(q, k, v, lam_p, sub, lam_init, B, T, tq, tk, min(tq, FLASH_ROWS))
                cm = _conv_prompt(glu, conv_buf[i], cw, cb, clg, clb, B, T, _row_tile(T, 256))
                buf = glu.reshape(B, T, B_WIDTH)[:, -(CONV_WIDTH - 1):]
            else:
                page_table, cache_k, cache_v = paged
                o = _paged_sample(page_table, q, k[-1], v[-1], cache_k, cache_v, i, lam_p, sub, lam_init, B, T)
                cm, buf = _conv_sample(glu, conv_buf[i], cw, cb, clg, clb, B, T)
            mix_in, w_mix = [o, cm], p['w_out_ab'][i]
            new_conv.append(buf)
        else:
            wts = [p['rwkv_wr'][i], p['rwkv_wk'][i], p['rwkv_wv'][i],
                   vec(p['rwkv_w0'][i]), p['rwkv_w1'][i], p['rwkv_w2'][i],
                   vec(p['rwkv_a0'][i]), p['rwkv_a1'][i], p['rwkv_a2'][i],
                   p['rwkv_g1'][i], p['rwkv_g2'][i]]
            vmix = None
            if v_first is not None:
                vmix = (vec(p['rwkv_v0'][i - 1]), p['rwkv_v1'][i - 1], p['rwkv_v2'][i - 1], v_first)
            last_u, seqs = _rwkv_proj(h, shift[i], vec(p['norm_mix'][l]), p['rwkv_mu'][i], wts, vmix, tm, B, T)
            if v_first is None:
                v_first = seqs[2]
            z, s_new = _rwkv_scan(*seqs, vec(p['rwkv_kk'][i]), vec(p['rwkv_ka'][i]), vec(p['rwkv_rk'][i]),
                                  vec(p['rwkv_lnx_g'][i]), vec(p['rwkv_lnx_b'][i]), wkv[i], B, T)
            mix_in, w_mix = [z], p['rwkv_wo'][i]
            new_wkv.append(s_new)
            new_shift.append(last_u)
        last = l == depth - 1
        res = _mix_ffn(mix_in, h, w_mix, vec(p['norm_ffn'][l]), p['w_gate'][l], p['w_up'][l], p['w_down'][l], tm_wide,
                       vec(p['norm_final']) if last else None)
        if last:
            h, y = res
        else:
            h = res
    k_all, v_all = (s.reshape(s.shape[0], B, T, A_HEADS, A_V_DIM) for s in kv_stack)
    return (y.reshape(B, T, D), k_all, v_all, jnp.stack(new_conv), jnp.stack(new_wkv), jnp.stack(new_shift))


_MATMUL_WEIGHTS = ('w_in_ab', 'w_out_ab', 'rwkv_wr', 'rwkv_wk', 'rwkv_wv', 'rwkv_w1', 'rwkv_w2', 'rwkv_a1',
                   'rwkv_a2', 'rwkv_v1', 'rwkv_v2', 'rwkv_g1', 'rwkv_g2', 'rwkv_wo', 'w_gate', 'w_up', 'w_down')


def kernel(x_prompt, x_sample, cache_k, cache_v, state_conv, state_wkv, state_shift, page_table, norm_mix, norm_ffn, norm_final, w_in_ab, diff_lambda, subln, conv_w, conv_b, conv_ln_g, conv_ln_b, w_out_ab, rwkv_mu, rwkv_wr, rwkv_wk, rwkv_wv, rwkv_w0, rwkv_w1, rwkv_w2, rwkv_a0, rwkv_a1, rwkv_a2, rwkv_v0, rwkv_v1, rwkv_v2, rwkv_g1, rwkv_g2, rwkv_kk, rwkv_ka, rwkv_rk, rwkv_lnx_g, rwkv_lnx_b, rwkv_wo, w_gate, w_up, w_down):
    p = dict(norm_mix=norm_mix, norm_ffn=norm_ffn, norm_final=norm_final, w_in_ab=w_in_ab,
             diff_lambda=diff_lambda, subln=subln, conv_w=conv_w, conv_b=conv_b, conv_ln_g=conv_ln_g,
             conv_ln_b=conv_ln_b, w_out_ab=w_out_ab, rwkv_mu=rwkv_mu, rwkv_wr=rwkv_wr, rwkv_wk=rwkv_wk,
             rwkv_wv=rwkv_wv, rwkv_w0=rwkv_w0, rwkv_w1=rwkv_w1, rwkv_w2=rwkv_w2, rwkv_a0=rwkv_a0,
             rwkv_a1=rwkv_a1, rwkv_a2=rwkv_a2, rwkv_v0=rwkv_v0, rwkv_v1=rwkv_v1, rwkv_v2=rwkv_v2,
             rwkv_g1=rwkv_g1, rwkv_g2=rwkv_g2, rwkv_kk=rwkv_kk, rwkv_ka=rwkv_ka, rwkv_rk=rwkv_rk,
             rwkv_lnx_g=rwkv_lnx_g, rwkv_lnx_b=rwkv_lnx_b, rwkv_wo=rwkv_wo, w_gate=w_gate, w_up=w_up,
             w_down=w_down)
    for name in _MATMUL_WEIGHTS:
        p[name] = p[name].astype(BF16)

    Bp, Tp, D = x_prompt.shape
    n_a, n_c = state_conv.shape[0], state_wkv.shape[0]
    zero_conv = jnp.zeros((n_a, Bp, CONV_WIDTH - 1, B_WIDTH), F32)
    zero_wkv = jnp.zeros((n_c, Bp) + state_wkv.shape[2:], F32)
    zero_shift = jnp.zeros((n_c, Bp, D), F32)
    outs_p = _trunk(x_prompt, jnp.arange(Tp), zero_conv, zero_wkv, zero_shift, p, None)

    past = page_table.shape[1] * PAGE_SIZE
    outs_s = _trunk(x_sample, past + jnp.arange(x_sample.shape[1]), state_conv, state_wkv, state_shift, p,
                    (page_table, cache_k, cache_v))
    return (outs_p[0], outs_s[0]) + tuple(outs_p[1:]) + tuple(outs_s[1:])
```
